```python
import math
import jax, jax.numpy as jnp
from jax import lax
import numpy as np

D_MODEL = 1024
BATCH = 8
SEQ = 4096
DEPTH = 4

A_WIDTH = D_MODEL // 2
A_HEAD_DIM = 128
A_HEADS = A_WIDTH // A_HEAD_DIM
A_CHUNK = 64
B_WIDTH = D_MODEL - A_WIDTH
B_HEAD_DIM = 64
B_HEADS = B_WIDTH // B_HEAD_DIM
B_CONFIGS = ((128, 1), (512, 4), (2048, 16))
EVEN_IN = 5 * A_WIDTH + 3 * B_WIDTH
C_HEADS = 8
C_HEAD_DIM = D_MODEL // C_HEADS // 2
C_Q_BLOCK = 128
ODD_IN = 3 * D_MODEL
D_FF = 2816
N_EXPERTS = 8
TOP_K = 2
ROPE_THETA = 10000.0
LN_EPS = 1e-5
RMS_EPS = 1e-5
MASK_VALUE = -1e30
MIN_FORGET = 1e-30
DEEPNORM_ALPHA = (2 * DEPTH) ** 0.25
DEEPNORM_BETA = (8 * DEPTH) ** -0.25
N_EVEN = (DEPTH + 1) // 2
N_ODD = DEPTH // 2
F32 = jnp.float32

kernel_name = 'hybrid_hgrn2_dilated_diffattn_moe_encoder'


def layer_norm(x, g, b):
    xf = x.astype(F32)
    mu = jnp.mean(xf, axis=-1, keepdims=True)
    var = jnp.mean(jnp.square(xf - mu), axis=-1, keepdims=True)
    return ((xf - mu) * lax.rsqrt(var + LN_EPS)).astype(x.dtype) * g + b


def rms_norm(x, w):
    xf = x.astype(F32)
    return (xf * lax.rsqrt(jnp.mean(xf * xf, axis=-1, keepdims=True) + RMS_EPS)).astype(x.dtype) * w


def rope_tables(seq, dim):
    inv = ROPE_THETA ** (-jnp.arange(0, dim, 2, dtype=F32) / dim)
    ang = jnp.arange(seq, dtype=F32)[:, None] * inv[None, :]
    return jnp.cos(ang), jnp.sin(ang)


def apply_rope(x, cos, sin):
    x1, x2 = jnp.split(x, 2, axis=-1)
    c = cos.astype(x.dtype)
    s = sin.astype(x.dtype)
    return jnp.concatenate([x1 * c - x2 * s, x2 * c + x1 * s], axis=-1)


def heads(t, n, hd):
    bsz, s, _ = t.shape
    return t.reshape(bsz, s, n, hd).transpose(0, 2, 1, 3)


def merge_heads(t):
    bsz, h, s, hd = t.shape
    return t.transpose(0, 2, 1, 3).reshape(bsz, s, h * hd)


def hgrn2_direction(q, k, v, log_f):
    bsz, h, s, dk = q.shape
    dv = v.shape[-1]
    n = s // A_CHUNK

    def chunks(t):
        return t.reshape(bsz, h, n, A_CHUNK, t.shape[-1]).transpose(2, 0, 1, 3, 4)

    seen = jnp.tril(jnp.ones((A_CHUNK, A_CHUNK), dtype=bool))[:, :, None]

    def step(state, inp):
        qc, kc, vc, gc = inp
        b = jnp.cumsum(gc, axis=2)
        pair = b[:, :, :, None, :] - b[:, :, None, :, :]
        decay = jnp.where(seen, jnp.exp(jnp.where(seen, pair, 0.0)), 0.0)
        scores = jnp.einsum('bhtsk,bhsk->bhts', decay * qc[:, :, :, None, :], kc)
        o = jnp.einsum('bhts,bhsv->bhtv', scores, vc) + jnp.einsum('bhtk,bhkv->bhtv', qc * jnp.exp(b), state)
        b_last = b[:, :, -1, :]
        new_state = jnp.exp(b_last)[..., None] * state + jnp.einsum(
            'bhsk,bhsv->bhkv', kc * jnp.exp(b_last[:, :, None, :] - b), vc)
        return new_state, o

    init = jnp.zeros((bsz, h, dk, dv), F32)
    _, outs = lax.scan(step, init, (chunks(q), chunks(k), chunks(v), chunks(log_f)))
    return outs.transpose(1, 2, 0, 3, 4).reshape(bsz, h, s, dv)


def dilated_band(q, k, v, dil, radius):
    bsz, h, s, hd = q.shape
    L = s // dil
    nb = -(-L // radius)
    lp = nb * radius

    def residues(t):
        return t.reshape(bsz, h, L, dil, hd).transpose(0, 1, 3, 2, 4)

    qr = jnp.pad(residues(q), ((0, 0), (0, 0), (0, 0), (0, lp - L), (0, 0)))
    qr = qr.reshape(bsz, h, dil, nb, radius, hd)

    def key_blocks(t):
        tp = jnp.pad(residues(t), ((0, 0), (0, 0), (0, 0), (radius, lp - L + radius), (0, 0)))
        tp = tp.reshape(bsz, h, dil, nb + 2, radius, hd)
        return jnp.concatenate([tp[:, :, :, 0:nb], tp[:, :, :, 1:nb + 1], tp[:, :, :, 2:nb + 2]], axis=4)

    kb = key_blocks(k)
    vb = key_blocks(v)
    scores = jnp.einsum('bhrnqd,bhrnkd->bhrnqk', qr, kb).astype(F32) * (hd ** -0.5)
    qi = jnp.arange(radius)[:, None]
    kj = jnp.arange(3 * radius)[None, :]
    key_l = jnp.arange(nb)[:, None, None] * radius + kj[None] - radius
    valid = (jnp.abs(kj - radius - qi) <= radius)[None] & (key_l >= 0) & (key_l < L)
    scores = jnp.where(valid, scores, MASK_VALUE)
    lse = jax.nn.logsumexp(scores, axis=-1)
    probs = jnp.where(valid, jnp.exp(scores - lse[..., None]), 0.0).astype(v.dtype)
    out = jnp.einsum('bhrnqk,bhrnkd->bhrnqd', probs, vb)
    out = out.reshape(bsz, h, dil, lp, hd)[:, :, :, :L].transpose(0, 1, 3, 2, 4).reshape(bsz, h, s, hd)
    lse = lse.reshape(bsz, h, dil, lp)[..., :L].transpose(0, 1, 3, 2).reshape(bsz, h, s)
    return out, lse


def dilated_attention(q, k, v):
    outs, lses = [], []
    for window, dil in B_CONFIGS:
        o, l = dilated_band(q, k, v, dil, window // (2 * dil))
        outs.append(o)
        lses.append(l)
    w = jax.nn.softmax(jnp.stack(lses, axis=0), axis=0)
    out = jnp.einsum('cbhs,cbhsd->bhsd', w, jnp.stack(outs, axis=0).astype(F32))
    return out.astype(q.dtype)


def even_mixer(x, w_in, lb, norm_w, w_out, cos, sin):
    proj = x @ w_in
    splits = np.cumsum([A_WIDTH] * 5 + [B_WIDTH] * 2).tolist()
    qa, fz_fw, fz_bw, va, ga, qb, kb, vb = jnp.split(proj, splits, axis=-1)
    lbh = lb.astype(F32).reshape(A_HEADS, 1, A_HEAD_DIM)

    def gates(z):
        zf = heads(z, A_HEADS, A_HEAD_DIM).astype(F32)
        f = lbh + (1.0 - lbh) * jax.nn.sigmoid(zf)
        log_f = jnp.log(jnp.maximum(f, MIN_FORGET))
        return log_f, (1.0 - lbh) * jax.nn.sigmoid(-zf)

    qa_h = heads(qa, A_HEADS, A_HEAD_DIM).astype(F32)
    va_h = heads(va, A_HEADS, A_HEAD_DIM).astype(F32)
    logf_fw, k_fw = gates(fz_fw)
    logf_bw, k_bw = gates(fz_bw)
    flip = lambda t: jnp.flip(t, axis=2)
    o_fw = hgrn2_direction(qa_h, k_fw, va_h, logf_fw)
    o_bw = flip(hgrn2_direction(flip(qa_h), flip(k_bw), flip(va_h), flip(logf_bw)))
    oa = rms_norm((o_fw + o_bw).astype(x.dtype), norm_w) * jax.nn.silu(heads(ga, A_HEADS, A_HEAD_DIM))
    qb_h = apply_rope(heads(qb, B_HEADS, B_HEAD_DIM), cos, sin)
    kb_h = apply_rope(heads(kb, B_HEADS, B_HEAD_DIM), cos, sin)
    vb_h = heads(vb, B_HEADS, B_HEAD_DIM)
    ob = dilated_attention(qb_h, kb_h, vb_h)
    return jnp.concatenate([merge_heads(oa), merge_heads(ob)], axis=-1) @ w_out


def diff_attention(q, k, v, lam, lambda_init, subln_w):
    bsz, h2, s, hd = q.shape
    h = h2 // 2
    nb = s // C_Q_BLOCK
    qblocks = q.reshape(bsz, h2, nb, C_Q_BLOCK, hd).transpose(2, 0, 1, 3, 4)

    def one_block(qblk):
        scores = jnp.einsum('bhqd,bhkd->bhqk', qblk, k).astype(F32) * (hd ** -0.5)
        p = jax.nn.softmax(scores, axis=-1).reshape(bsz, h, 2, C_Q_BLOCK, s)
        a = (p[:, :, 0] - lam * p[:, :, 1]).astype(v.dtype)
        return jnp.einsum('bhqk,bhkd->bhqd', a, v)

    out = lax.map(one_block, qblocks)
    out = out.transpose(1, 2, 0, 3, 4).reshape(bsz, h, s, v.shape[-1])
    return rms_norm(out, subln_w) * (1.0 - lambda_init)


def odd_mixer(x, w_in, lam_params, subln_w, w_out, cos, sin, lambda_init):
    q, k, v = jnp.split(x @ w_in, 3, axis=-1)
    q = apply_rope(heads(q, 2 * C_HEADS, C_HEAD_DIM), cos, sin)
    k = apply_rope(heads(k, 2 * C_HEADS, C_HEAD_DIM), cos, sin)
    v = heads(v, C_HEADS, 2 * C_HEAD_DIM)
    lp = lam_params.astype(F32)
    lam = jnp.exp(jnp.sum(lp[0] * lp[1])) - jnp.exp(jnp.sum(lp[2] * lp[3])) + lambda_init
    o = diff_attention(q, k, v, lam, lambda_init, subln_w)
    return merge_heads(o) @ w_out


def swiglu(x, w1, w3, w2):
    return (jax.nn.silu(x @ w1) * (x @ w3)) @ w2


def moe_swiglu(x, router, w1, w3, w2):
    bsz, s, d = x.shape
    xt = x.reshape(bsz * s, d)
    logits = (xt @ router).astype(F32)
    top_v, top_i = lax.top_k(logits, TOP_K)
    gates = jax.nn.softmax(top_v, axis=-1)
    dense_gate = jnp.sum(jax.nn.one_hot(top_i, N_EXPERTS, dtype=F32) * gates[..., None], axis=1)
    y = jnp.zeros_like(xt)
    for e in range(N_EXPERTS):
        h = jax.nn.silu(xt @ w1[e]) * (xt @ w3[e])
        y = y + dense_gate[:, e:e + 1].astype(x.dtype) * (h @ w2[e])
    return y.reshape(bsz, s, d)


def setup_inputs(seed: int = 0) -> dict:
    key = jax.random.key(seed)
    keys = iter(jax.random.split(key, 32))
    D = D_MODEL
    beta = DEEPNORM_BETA

    def normal(shape, scale):
        return jax.random.normal(next(keys), shape, F32) * scale

    def gain(shape):
        return 1.0 + normal(shape, 0.02)

    even_cols = jnp.concatenate([jnp.ones((3 * A_WIDTH,), F32), jnp.full((A_WIDTH,), beta, F32),
                                 jnp.ones((A_WIDTH + 2 * B_WIDTH,), F32), jnp.full((B_WIDTH,), beta, F32)])
    odd_cols = jnp.concatenate([jnp.ones((2 * D,), F32), jnp.full((D,), beta, F32)])
    return {
        'x': normal((BATCH, SEQ, D), 1.0),
        'ev_w_in': normal((N_EVEN, D, EVEN_IN), D ** -0.5) * even_cols,
        'ev_lb_logits': normal((N_EVEN, A_WIDTH), 0.5),
        'ev_hgrn_norm': gain((N_EVEN, A_HEAD_DIM)),
        'ev_w_out': normal((N_EVEN, D, D), beta * D ** -0.5),
        'ev_ln1_g': gain((N_EVEN, D)),
        'ev_ln1_b': normal((N_EVEN, D), 0.02),
        'ev_w1': normal((N_EVEN, D, D_FF), beta * D ** -0.5),
        'ev_w3': normal((N_EVEN, D, D_FF), beta * D ** -0.5),
        'ev_w2': normal((N_EVEN, D_FF, D), beta * D_FF ** -0.5),
        'ev_ln2_g': gain((N_EVEN, D)),
        'ev_ln2_b': normal((N_EVEN, D), 0.02),
        'od_w_in': normal((N_ODD, D, ODD_IN), D ** -0.5) * odd_cols,
        'od_lambda': normal((N_ODD, 4, C_HEAD_DIM), 0.1),
        'od_subln': gain((N_ODD, 2 * C_HEAD_DIM)),
        'od_w_out': normal((N_ODD, D, D), beta * D ** -0.5),
        'od_ln1_g': gain((N_ODD, D)),
        'od_ln1_b': normal((N_ODD, D), 0.02),
        'od_router': normal((N_ODD, D, N_EXPERTS), D ** -0.5),
        'od_w1': normal((N_ODD, N_EXPERTS, D, D_FF), beta * D ** -0.5),
        'od_w3': normal((N_ODD, N_EXPERTS, D, D_FF), beta * D ** -0.5),
        'od_w2': normal((N_ODD, N_EXPERTS, D_FF, D), beta * D_FF ** -0.5),
        'od_ln2_g': gain((N_ODD, D)),
        'od_ln2_b': normal((N_ODD, D), 0.02),
    }


def reference(x, ev_w_in, ev_lb_logits, ev_hgrn_norm, ev_w_out, ev_ln1_g, ev_ln1_b, ev_w1, ev_w3, ev_w2,
              ev_ln2_g, ev_ln2_b, od_w_in, od_lambda, od_subln, od_w_out, od_ln1_g, od_ln1_b, od_router,
              od_w1, od_w3, od_w2, od_ln2_g, od_ln2_b):
    s = x.shape[1]
    cos_b, sin_b = rope_tables(s, B_HEAD_DIM)
    cos_c, sin_c = rope_tables(s, C_HEAD_DIM)
    lb_soft = jax.nn.softmax(ev_lb_logits.astype(F32), axis=0)
    lower_bounds = jnp.cumsum(lb_soft, axis=0) - lb_soft[0]
    for layer in range(DEPTH):
        j = layer // 2
        if layer % 2 == 0:
            h = even_mixer(x, ev_w_in[j], lower_bounds[j], ev_hgrn_norm[j], ev_w_out[j], cos_b, sin_b)
            x = layer_norm(DEEPNORM_ALPHA * x + h, ev_ln1_g[j], ev_ln1_b[j])
            h = swiglu(x, ev_w1[j], ev_w3[j], ev_w2[j])
            x = layer_norm(DEEPNORM_ALPHA * x + h, ev_ln2_g[j], ev_ln2_b[j])
        else:
            lambda_init = 0.8 - 0.6 * math.exp(-0.3 * layer)
            h = odd_mixer(x, od_w_in[j], od_lambda[j], od_subln[j], od_w_out[j], cos_c, sin_c, lambda_init)
            x = layer_norm(DEEPNORM_ALPHA * x + h, od_ln1_g[j], od_ln1_b[j])
            h = moe_swiglu(x, od_router[j], od_w1[j], od_w3[j], od_w2[j])
            x = layer_norm(DEEPNORM_ALPHA * x + h, od_ln2_g[j], od_ln2_b[j])
    return x
```

```python
import functools
import math

import numpy as np
import jax
import jax.numpy as jnp
from jax import lax
from jax.experimental import pallas as pl
from jax.experimental.pallas import tpu as pltpu

F32 = jnp.float32
BF16 = jnp.bfloat16

A_HEAD_DIM = 128
B_HEAD_DIM = 64
B_CONFIGS = ((128, 1), (512, 4), (2048, 16))
BAND_RADIUS = 64
C_HEAD_DIM = 64
N_EXPERTS = 8
ROPE_THETA = 10000.0
LN_EPS = 1e-5
RMS_EPS = 1e-5
MASK_VALUE = -1e30
MIN_FORGET = 1e-30

LANES = 128
SUBLANES = 8
VMEM_LIMIT_BYTES = 56 * 1024 * 1024

HGRN_CHUNK = 128
HGRN_LEVELS = 7


def _cparams(*sem):
    return pltpu.CompilerParams(dimension_semantics=sem, vmem_limit_bytes=VMEM_LIMIT_BYTES)


def _tile(n, prefs):
    for p in prefs:
        if n % p == 0:
            return p
    return n


def _dot(a, b):
    return jnp.dot(a, b, preferred_element_type=F32)


def _dot_nt(a, b):
    return lax.dot_general(a, b, (((1,), (1,)), ((), ())), preferred_element_type=F32)


def _dot_tn(a, b):
    return lax.dot_general(a, b, (((0,), (0,)), ((), ())), preferred_element_type=F32)


def _sigmoid(x):
    return 1.0 / (1.0 + jnp.exp(-x))


def _layer_norm(y, g, b):
    mu = jnp.mean(y, axis=-1, keepdims=True)
    d = y - mu
    var = jnp.mean(d * d, axis=-1, keepdims=True)
    return d * lax.rsqrt(var + LN_EPS) * g + b


def _proj_kernel(x_ref, w_ref, o_ref):
    o_ref[...] = _dot(x_ref[...], w_ref[...]).astype(o_ref.dtype)


def _proj(x, w, out_dtype):
    m, k = x.shape
    n = w.shape[1]
    tm = _tile(m, (1024, 512, 256))
    tn = _tile(n, (512, 256, 128))
    return pl.pallas_call(
        _proj_kernel,
        grid=(m // tm, n // tn),
        in_specs=[pl.BlockSpec((tm, k), lambda i, j: (i, 0)), pl.BlockSpec((k, tn), lambda i, j: (0, j))],
        out_specs=pl.BlockSpec((tm, tn), lambda i, j: (i, j)),
        out_shape=jax.ShapeDtypeStruct((m, n), out_dtype),
        compiler_params=_cparams("parallel", "arbitrary"),
        name="proj",
    )(x, w)


def _proj_rope_kernel(x_ref, w_ref, cos_ref, sin_ref, scale_ref, o_ref):
    acc = _dot(x_ref[...], w_ref[...])
    tn = acc.shape[1]
    lane = lax.broadcasted_iota(jnp.int32, acc.shape, 1)
    first_half = (lane % B_HEAD_DIM) < (B_HEAD_DIM // 2)
    partner = jnp.where(first_half, pltpu.roll(acc, tn - B_HEAD_DIM // 2, 1), pltpu.roll(acc, B_HEAD_DIM // 2, 1))
    o_ref[...] = ((acc * cos_ref[...] + partner * sin_ref[...]) * scale_ref[...]).astype(o_ref.dtype)


def _proj_rope(x, w, cos_t, sin_t, scale_row, seq):
    m, k = x.shape
    n = w.shape[1]
    tm = _tile(seq, (1024, 512, 256))
    tn = cos_t.shape[1]
    nsb = seq // tm
    return pl.pallas_call(
        _proj_rope_kernel,
        grid=(m // tm, n // tn),
        in_specs=[
            pl.BlockSpec((tm, k), lambda i, j: (i, 0)),
            pl.BlockSpec((k, tn), lambda i, j: (0, j)),
            pl.BlockSpec((tm, tn), lambda i, j: (i % nsb, 0)),
            pl.BlockSpec((tm, tn), lambda i, j: (i % nsb, 0)),
            pl.BlockSpec((1, tn), lambda i, j: (0, j)),
        ],
        out_specs=pl.BlockSpec((tm, tn), lambda i, j: (i, j)),
        out_shape=jax.ShapeDtypeStruct((m, n), BF16),
        compiler_params=_cparams("parallel", "arbitrary"),
        name="proj_rope",
    )(x, w, cos_t, sin_t, scale_row)


def _out_ln_kernel(*refs, n_in, alpha):
    xs = refs[:n_in]
    ws = refs[n_in:2 * n_in]
    resid_ref, g_ref, b_ref, of_ref, ob_ref = refs[2 * n_in:]
    acc = _dot(xs[0][...], ws[0][...])
    for x_ref, w_ref in zip(xs[1:], ws[1:]):
        acc = acc + _dot(x_ref[...], w_ref[...])
    z = _layer_norm(alpha * resid_ref[...] + acc, g_ref[...], b_ref[...])
    of_ref[...] = z
    ob_ref[...] = z.astype(BF16)


def _out_ln(xs, ws, resid, g, b, alpha):
    m, d = resid.shape
    tm = _tile(m, (512, 256))
    n_in = len(xs)
    in_specs = [pl.BlockSpec((tm, x.shape[1]), lambda i: (i, 0)) for x in xs]
    in_specs += [pl.BlockSpec(w.shape, lambda i: (0, 0)) for w in ws]
    in_specs += [pl.BlockSpec((tm, d), lambda i: (i, 0)), pl.BlockSpec((1, d), lambda i: (0, 0)),
                 pl.BlockSpec((1, d), lambda i: (0, 0))]
    return pl.pallas_call(
        functools.partial(_out_ln_kernel, n_in=n_in, alpha=alpha),
        grid=(m // tm,),
        in_specs=in_specs,
        out_specs=[pl.BlockSpec((tm, d), lambda i: (i, 0)), pl.BlockSpec((tm, d), lambda i: (i, 0))],
        out_shape=[jax.ShapeDtypeStruct((m, d), F32), jax.ShapeDtypeStruct((m, d), BF16)],
        compiler_params=_cparams("parallel"),
        name="out_ln",
    )(*xs, *ws, resid, g.reshape(1, d), b.reshape(1, d))


def _swiglu_partial(x, w1, w3, w2):
    h1 = _dot(x, w1)
    h3 = _dot(x, w3)
    h = (h1 * _sigmoid(h1)) * h3
    return _dot(h.astype(BF16), w2)


def _ffn_ln_kernel(x_ref, w1_ref, w3_ref, w2_ref, resid_ref, g_ref, b_ref, of_ref, ob_ref, acc_ref, *, alpha):
    j = pl.program_id(1)

    @pl.when(j == 0)
    def _():
        acc_ref[...] = jnp.zeros_like(acc_ref)

    acc_ref[...] += _swiglu_partial(x_ref[...], w1_ref[...], w3_ref[...], w2_ref[...])

    @pl.when(j == pl.num_programs(1) - 1)
    def _():
        z = _layer_norm(alpha * resid_ref[...] + acc_ref[...], g_ref[...], b_ref[...])
        of_ref[...] = z
        ob_ref[...] = z.astype(BF16)


def _ffn_ln(x_bf, resid, w1, w3, w2, g, b, alpha):
    m, d = resid.shape
    ff = w1.shape[1]
    tm = _tile(m, (1024, 512, 256))
    tf = _tile(ff, (256, 128))
    return pl.pallas_call(
        functools.partial(_ffn_ln_kernel, alpha=alpha),
        grid=(m // tm, ff // tf),
        in_specs=[
            pl.BlockSpec((tm, d), lambda i, j: (i, 0)),
            pl.BlockSpec((d, tf), lambda i, j: (0, j)),
            pl.BlockSpec((d, tf), lambda i, j: (0, j)),
            pl.BlockSpec((tf, d), lambda i, j: (j, 0)),
            pl.BlockSpec((tm, d), lambda i, j: (i, 0)),
            pl.BlockSpec((1, d), lambda i, j: (0, 0)),
            pl.BlockSpec((1, d), lambda i, j: (0, 0)),
        ],
        out_specs=[pl.BlockSpec((tm, d), lambda i, j: (i, 0)), pl.BlockSpec((tm, d), lambda i, j: (i, 0))],
        out_shape=[jax.ShapeDtypeStruct((m, d), F32), jax.ShapeDtypeStruct((m, d), BF16)],
        scratch_shapes=[pltpu.VMEM((tm, d), F32)],
        compiler_params=_cparams("parallel", "arbitrary"),
        name="ffn_ln",
    )(x_bf, w1, w3, w2, resid, g.reshape(1, d), b.reshape(1, d))


def _moe_ffn_kernel(te_ref, na_ref, x_ref, w1_ref, w3_ref, w2_ref, o_ref, acc_ref):
    i = pl.program_id(0)
    j = pl.program_id(1)
    active = i < na_ref[0]

    @pl.when(active & (j == 0))
    def _():
        acc_ref[...] = jnp.zeros_like(acc_ref)

    @pl.when(active)
    def _():
        acc_ref[...] += _swiglu_partial(x_ref[...], w1_ref[0], w3_ref[0], w2_ref[0])

    @pl.when(active & (j == pl.num_programs(1) - 1))
    def _():
        o_ref[...] = acc_ref[...].astype(o_ref.dtype)


def _moe_ffn(x_sorted, tile_expert, n_active, w1, w3, w2, tm):
    p, d = x_sorted.shape
    ff = w1.shape[2]
    tf = _tile(ff, (256, 128))
    nf = ff // tf

    def row(i, na):
        return jnp.minimum(i, na[0] - 1)

    def col(i, j, na):
        return jnp.where(i < na[0], j, nf - 1)

    grid_spec = pltpu.PrefetchScalarGridSpec(
        num_scalar_prefetch=2,
        grid=(p // tm, nf),
        in_specs=[
            pl.BlockSpec((tm, d), lambda i, j, te, na: (row(i, na), 0)),
            pl.BlockSpec((1, d, tf), lambda i, j, te, na: (te[row(i, na)], 0, col(i, j, na))),
            pl.BlockSpec((1, d, tf), lambda i, j, te, na: (te[row(i, na)], 0, col(i, j, na))),
            pl.BlockSpec((1, tf, d), lambda i, j, te, na: (te[row(i, na)], col(i, j, na), 0)),
        ],
        out_specs=pl.BlockSpec((tm, d), lambda i, j, te, na: (row(i, na), 0)),
        scratch_shapes=[pltpu.VMEM((tm, d), F32)],
    )
    return pl.pallas_call(
        _moe_ffn_kernel,
        grid_spec=grid_spec,
        out_shape=jax.ShapeDtypeStruct((p, d), BF16),
        compiler_params=_cparams("arbitrary", "arbitrary"),
        name="moe_ffn",
    )(tile_expert, n_active, x_sorted, w1, w3, w2)


def _router_kernel(x_ref, r_ref, gate_ref, idx_ref):
    logits = jnp.dot(x_ref[...], r_ref[...], precision=lax.Precision.HIGHEST, preferred_element_type=F32)
    lane = lax.broadcasted_iota(jnp.int32, logits.shape, 1)
    neg = jnp.float32(-jnp.inf)
    logits = jnp.where(lane < N_EXPERTS, logits, neg)
    v1 = jnp.max(logits, axis=-1, keepdims=True)
    i1 = jnp.min(jnp.where(logits == v1, lane, LANES), axis=-1, keepdims=True)
    rest = jnp.where(lane == i1, neg, logits)
    v2 = jnp.max(rest, axis=-1, keepdims=True)
    i2 = jnp.min(jnp.where(rest == v2, lane, LANES), axis=-1, keepdims=True)
    e = jnp.exp(v2 - v1)
    g1 = 1.0 / (1.0 + e)
    g2 = e / (1.0 + e)
    gate_ref[...] = jnp.where(lane == 0, g1, jnp.where(lane == 1, g2, 0.0))
    idx_ref[...] = jnp.where(lane == 0, i1, jnp.where(lane == 1, i2, 0))


def _router(x_f32, router_padded):
    m, d = x_f32.shape
    tm = _tile(m, (1024, 512, 256))
    return pl.pallas_call(
        _router_kernel,
        grid=(m // tm,),
        in_specs=[pl.BlockSpec((tm, d), lambda i: (i, 0)), pl.BlockSpec((d, LANES), lambda i: (0, 0))],
        out_specs=[pl.BlockSpec((tm, LANES), lambda i: (i, 0)), pl.BlockSpec((tm, LANES), lambda i: (i, 0))],
        out_shape=[jax.ShapeDtypeStruct((m, LANES), F32), jax.ShapeDtypeStruct((m, LANES), jnp.int32)],
        compiler_params=_cparams("parallel"),
        name="router",
    )(x_f32, router_padded)


def _moe_combine_kernel(resid_ref, y1_ref, y2_ref, gate_ref, g_ref, b_ref, of_ref, ob_ref, *, alpha):
    gates = gate_ref[...]
    y = gates[:, 0:1] * y1_ref[...].astype(F32) + gates[:, 1:2] * y2_ref[...].astype(F32)
    z = _layer_norm(alpha * resid_ref[...] + y, g_ref[...], b_ref[...])
    of_ref[...] = z
    ob_ref[...] = z.astype(BF16)


def _moe_combine(resid, y1, y2, gates, g, b, alpha):
    m, d = resid.shape
    tm = _tile(m, (512, 256))
    row = lambda i: (i, 0)
    fixed = lambda i: (0, 0)
    return pl.pallas_call(
        functools.partial(_moe_combine_kernel, alpha=alpha),
        grid=(m // tm,),
        in_specs=[pl.BlockSpec((tm, d), row), pl.BlockSpec((tm, d), row), pl.BlockSpec((tm, d), row),
                  pl.BlockSpec((tm, LANES), row), pl.BlockSpec((1, d), fixed), pl.BlockSpec((1, d), fixed)],
        out_specs=[pl.BlockSpec((tm, d), row), pl.BlockSpec((tm, d), row)],
        out_shape=[jax.ShapeDtypeStruct((m, d), F32), jax.ShapeDtypeStruct((m, d), BF16)],
        compiler_params=_cparams("parallel"),
        name="moe_combine",
    )(resid, y1, y2, gates, g.reshape(1, d), b.reshape(1, d))


def _hgrn_masks(c, levels):
    t = np.arange(c)[:, None]
    s = np.arange(c)[None, :]
    tri = np.stack([(s <= t), (s >= t)]).astype(np.float32)
    fwd, bwd = [], []
    for l in range(levels):
        same = (t >> (l + 1)) == (s >> (l + 1))
        t_up = ((t >> l) & 1) == 1
        s_up = ((s >> l) & 1) == 1
        fwd.append(same & t_up & ~s_up)
        bwd.append(same & ~t_up & s_up)
    fwd.append(t == s)
    bwd.append(t == s)
    return tri, np.stack([np.stack(fwd), np.stack(bwd)]).astype(np.float32)


def _segment_reference(x, level, forward):
    c, w = x.shape
    half = 1 << level
    seg = 2 * half
    idx = half - 1 if forward else half
    if seg >= SUBLANES:
        xr = x.reshape(c // seg, seg, w)
        return jnp.broadcast_to(xr[:, idx:idx + 1, :], xr.shape).reshape(c, w)
    x3 = x.reshape(c // SUBLANES, SUBLANES, w)
    sub = lax.broadcasted_iota(jnp.int32, x3.shape, 1)
    r3 = jnp.broadcast_to(x3[:, idx:idx + 1, :], x3.shape)
    for j in range(1, SUBLANES // seg):
        row = j * seg + idx
        r3 = jnp.where(sub >= j * seg, jnp.broadcast_to(x3[:, row:row + 1, :], x3.shape), r3)
    return r3.reshape(c, w)


def _hgrn_chunk(q, k, v, log_f, state_t, tri, masks, forward):
    c = q.shape[0]
    x = jnp.dot(tri, log_f, precision=lax.Precision.HIGHEST, preferred_element_type=F32)
    row = lax.broadcasted_iota(jnp.int32, q.shape, 0)
    scores = masks[HGRN_LEVELS] * _dot_nt(q.astype(BF16), k.astype(BF16))
    for level in range(HGRN_LEVELS):
        ref = _segment_reference(x, level, forward)
        upper = ((row >> level) & 1) == 1
        is_query = upper if forward else jnp.logical_not(upper)
        decay = jnp.exp(jnp.where(is_query, x - ref, ref - x))
        z = (jnp.where(is_query, q, k) * decay).astype(BF16)
        scores = scores + masks[level] * _dot_nt(z, z)
    x_end = x[c - 1:c, :] if forward else x[0:1, :]
    q_dec = (q * jnp.exp(x)).astype(BF16)
    o = _dot(scores.astype(BF16), v.astype(BF16)) + _dot_nt(q_dec, state_t.astype(BF16))
    k_dec = (k * jnp.exp(x_end - x)).astype(BF16)
    new_state_t = state_t * jnp.exp(x_end) + _dot_tn(v.astype(BF16), k_dec)
    return o, new_state_t


def _hgrn_kernel(q_ref, v_ref, g_ref, zf_ref, zb_ref, lb_ref, nw_ref, tri_ref, msk_ref, o_ref, acc_ref, *, n_chunks):
    c = HGRN_CHUNK
    lb = lb_ref[...]
    one_minus_lb = 1.0 - lb

    def gates(z):
        e = jnp.exp(-jnp.abs(z))
        inv = 1.0 / (1.0 + e)
        pos = z >= 0
        sig = jnp.where(pos, inv, e * inv)
        sig_neg = jnp.where(pos, e * inv, inv)
        f = lb + one_minus_lb * sig
        return jnp.log(jnp.maximum(f, MIN_FORGET)), one_minus_lb * sig_neg

    def load(ref, c0):
        return ref[0, pl.ds(c0, c), :].astype(F32)

    def fwd_body(i, state_t):
        c0 = pl.multiple_of(i * c, c)
        log_f, k = gates(load(zf_ref, c0))
        masks = [msk_ref[0, l] for l in range(HGRN_LEVELS + 1)]
        o, state_t = _hgrn_chunk(load(q_ref, c0), k, load(v_ref, c0), log_f, state_t, tri_ref[0], masks, True)
        acc_ref[pl.ds(c0, c), :] = o
        return state_t

    def bwd_body(i, state_t):
        c0 = pl.multiple_of((n_chunks - 1 - i) * c, c)
        log_f, k = gates(load(zb_ref, c0))
        masks = [msk_ref[1, l] for l in range(HGRN_LEVELS + 1)]
        o, state_t = _hgrn_chunk(load(q_ref, c0), k, load(v_ref, c0), log_f, state_t, tri_ref[1], masks, False)
        tot = acc_ref[pl.ds(c0, c), :] + o
        ms = jnp.mean(tot * tot, axis=-1, keepdims=True)
        g = load(g_ref, c0)
        out = tot * lax.rsqrt(ms + RMS_EPS) * nw_ref[...] * (g * _sigmoid(g))
        o_ref[0, pl.ds(c0, c), :] = out.astype(o_ref.dtype)
        return state_t

    zero = jnp.zeros((A_HEAD_DIM, A_HEAD_DIM), F32)
    lax.fori_loop(0, n_chunks, fwd_body, zero)
    lax.fori_loop(0, n_chunks, bwd_body, zero)


def _hgrn(qvg, zz, lb, norm_w, n_heads):
    bsz, s, _ = qvg.shape
    c = HGRN_CHUNK
    tri, masks = _hgrn_masks(c, HGRN_LEVELS)
    hd = A_HEAD_DIM
    col = lambda off: pl.BlockSpec((1, s, hd), lambda b, h: (b, 0, off + h))
    return pl.pallas_call(
        functools.partial(_hgrn_kernel, n_chunks=s // c),
        grid=(bsz, n_heads),
        in_specs=[
            col(0), col(n_heads), col(2 * n_heads),
            pl.BlockSpec((1, s, hd), lambda b, h: (b, 0, h)),
            pl.BlockSpec((1, s, hd), lambda b, h: (b, 0, n_heads + h)),
            pl.BlockSpec((1, hd), lambda b, h: (0, h)),
            pl.BlockSpec((1, hd), lambda b, h: (0, 0)),
            pl.BlockSpec(tri.shape, lambda b, h: (0, 0, 0)),
            pl.BlockSpec(masks.shape, lambda b, h: (0, 0, 0, 0)),
        ],
        out_specs=pl.BlockSpec((1, s, hd), lambda b, h: (b, 0, h)),
        out_shape=jax.ShapeDtypeStruct((bsz, s, n_heads * hd), BF16),
        scratch_shapes=[pltpu.VMEM((s, hd), F32)],
        compiler_params=_cparams("parallel", "parallel"),
        name="hgrn2",
    )(qvg, qvg, qvg, zz, zz, lb.reshape(1, -1), norm_w.reshape(1, hd), jnp.asarray(tri), jnp.asarray(masks))


def _band_kernel(q_ref, k_ref, v_ref, o_ref, lse_ref, *, tq, win, length):
    q0 = pl.program_id(2) * tq
    start = pl.multiple_of(jnp.clip(q0 - BAND_RADIUS, 0, length - win), 16)
    q = q_ref[0]
    kw = k_ref[0, pl.ds(start, win), :]
    vw = v_ref[0, pl.ds(start, win), :]
    lane = lax.broadcasted_iota(jnp.int32, q.shape, 1)
    head0 = lane < B_HEAD_DIM
    qpos = q0 + lax.broadcasted_iota(jnp.int32, (tq, win), 0)
    kpos = start + lax.broadcasted_iota(jnp.int32, (tq, win), 1)
    valid = jnp.abs(kpos - qpos) <= BAND_RADIUS

    def one_head(mask):
        s = _dot_nt(jnp.where(mask, q, jnp.zeros_like(q)), kw)
        s = jnp.where(valid, s, MASK_VALUE)
        m = jnp.max(s, axis=-1, keepdims=True)
        p = jnp.exp(s - m)
        l = jnp.sum(p, axis=-1, keepdims=True)
        return _dot(p.astype(BF16), vw) / l, m + jnp.log(l)

    oa, la = one_head(head0)
    ob, lb = one_head(jnp.logical_not(head0))
    o_ref[0] = jnp.where(head0, oa, ob)
    lse_ref[0] = jnp.where(head0, la, lb)


def _band_attention(q, k, v):
    bd, length, width = q.shape
    tq = min(128, length)
    win = min(length, tq + 2 * BAND_RADIUS)
    kv_spec = pl.BlockSpec((1, length, LANES), lambda b, h, i: (b, 0, h))
    q_spec = pl.BlockSpec((1, tq, LANES), lambda b, h, i: (b, i, h))
    return pl.pallas_call(
        functools.partial(_band_kernel, tq=tq, win=win, length=length),
        grid=(bd, width // LANES, length // tq),
        in_specs=[q_spec, kv_spec, kv_spec],
        out_specs=[q_spec, q_spec],
        out_shape=[jax.ShapeDtypeStruct(q.shape, F32), jax.ShapeDtypeStruct(q.shape, F32)],
        compiler_params=_cparams("parallel", "parallel", "arbitrary"),
        name="band_attn",
    )(q, k, v)


def _mix_kernel(*refs):
    n = (len(refs) - 1) // 2
    outs = [r[...] for r in refs[:n]]
    lses = [r[...] for r in refs[n:2 * n]]
    top = lses[0]
    for l in lses[1:]:
        top = jnp.maximum(top, l)
    ws = [jnp.exp(l - top) for l in lses]
    num = ws[0] * outs[0]
    den = ws[0]
    for w, o in zip(ws[1:], outs[1:]):
        num = num + w * o
        den = den + w
    refs[-1][...] = (num / den).astype(refs[-1].dtype)


def _mix_dilations(outs, lses):
    m, w = outs[0].shape
    tm = _tile(m, (512, 256))
    spec = pl.BlockSpec((tm, w), lambda i: (i, 0))
    return pl.pallas_call(
        _mix_kernel,
        grid=(m // tm,),
        in_specs=[spec] * (2 * len(outs)),
        out_specs=spec,
        out_shape=jax.ShapeDtypeStruct((m, w), BF16),
        compiler_params=_cparams("parallel"),
        name="mix_dilations",
    )(*outs, *lses)


def _dilated_attention(q, k, v):
    bsz, s, w = q.shape
    outs, lses = [], []
    for _, dil in B_CONFIGS:
        length = s // dil

        def split(t):
            return t.reshape(bsz, length, dil, w).transpose(0, 2, 1, 3).reshape(bsz * dil, length, w)

        def merge(t):
            return t.reshape(bsz, dil, length, w).transpose(0, 2, 1, 3).reshape(bsz * s, w)

        if dil == 1:
            o, l = _band_attention(q, k, v)
            outs.append(o.reshape(bsz * s, w))
            lses.append(l.reshape(bsz * s, w))
        else:
            o, l = _band_attention(split(q), split(k), split(v))
            outs.append(merge(o))
            lses.append(merge(l))
    return _mix_dilations(outs, lses)


def _diff_kernel(q_ref, k_ref, v_ref, lam_ref, sub_ref, o_ref, *, tk, lambda_init):
    q = q_ref[0]
    tq = q.shape[0]
    s_len = k_ref.shape[1]
    lane = lax.broadcasted_iota(jnp.int32, q.shape, 1)
    zero = jnp.zeros_like(q)
    qs = (jnp.where(lane < C_HEAD_DIM, q, zero), jnp.where(lane >= C_HEAD_DIM, q, zero))
    ms = [jnp.full((tq, 1), -jnp.inf, F32) for _ in range(2)]
    ls = [jnp.zeros((tq, 1), F32) for _ in range(2)]
    accs = [jnp.zeros((tq, 2 * C_HEAD_DIM), F32) for _ in range(2)]
    for c in range(s_len // tk):
        kc = k_ref[0, c * tk:(c + 1) * tk, :]
        vc = v_ref[0, c * tk:(c + 1) * tk, :]
        for h in range(2):
            s = _dot_nt(qs[h], kc)
            m_new = jnp.maximum(ms[h], jnp.max(s, axis=-1, keepdims=True))
            alpha = jnp.exp(ms[h] - m_new)
            p = jnp.exp(s - m_new)
            ls[h] = alpha * ls[h] + jnp.sum(p, axis=-1, keepdims=True)
            accs[h] = alpha * accs[h] + _dot(p.astype(BF16), vc)
            ms[h] = m_new
    lp = lam_ref[...]
    lam = (jnp.exp(jnp.sum(lp[0:1] * lp[1:2], axis=-1, keepdims=True))
           - jnp.exp(jnp.sum(lp[2:3] * lp[3:4], axis=-1, keepdims=True)) + lambda_init)
    o = accs[0] / ls[0] - lam * (accs[1] / ls[1])
    ms_o = jnp.mean(o * o, axis=-1, keepdims=True)
    o_ref[0] = (o * lax.rsqrt(ms_o + RMS_EPS) * sub_ref[...] * (1.0 - lambda_init)).astype(o_ref.dtype)


def _diff_attention(qk, v, lam_params, subln_w, lambda_init):
    bsz, s, w = v.shape
    n_heads = w // LANES
    tq = _tile(s, (256, 128))
    tk = _tile(s, (512, 256, 128))
    return pl.pallas_call(
        functools.partial(_diff_kernel, tk=tk, lambda_init=lambda_init),
        grid=(bsz, n_heads, s // tq),
        in_specs=[
            pl.BlockSpec((1, tq, LANES), lambda b, h, i: (b, i, h)),
            pl.BlockSpec((1, s, LANES), lambda b, h, i: (b, 0, n_heads + h)),
            pl.BlockSpec((1, s, LANES), lambda b, h, i: (b, 0, h)),
            pl.BlockSpec(lam_params.shape, lambda b, h, i: (0, 0)),
            pl.BlockSpec((1, LANES), lambda b, h, i: (0, 0)),
        ],
        out_specs=pl.BlockSpec((1, tq, LANES), lambda b, h, i: (b, i, h)),
        out_shape=jax.ShapeDtypeStruct((bsz, s, w), BF16),
        compiler_params=_cparams("parallel", "parallel", "arbitrary"),
        name="diff_attn",
    )(qk, qk, v, lam_params, subln_w.reshape(1, LANES))


def _rope_tables(seq, width):
    half = B_HEAD_DIM // 2
    inv = ROPE_THETA ** (-jnp.arange(0, B_HEAD_DIM, 2, dtype=F32) / B_HEAD_DIM)
    ang = jnp.arange(seq, dtype=F32)[:, None] * inv[None, :]
    cos, sin = jnp.cos(ang), jnp.sin(ang)
    reps = width // B_HEAD_DIM
    assert half * 2 == B_HEAD_DIM
    return jnp.tile(jnp.concatenate([cos, cos], axis=1), (1, reps)), jnp.tile(jnp.concatenate([-sin, sin], axis=1), (1, reps))


def _even_layer(x_f, x_b, bsz, seq, w_in, lb, norm_w, w_out, ln1, w1, w3, w2, ln2, rope, alpha):
    d = x_f.shape[1]
    aw = d // 2
    n_heads_a = aw // A_HEAD_DIM
    w_in = w_in.astype(BF16)
    cols = lambda a, b: w_in[:, a * aw:b * aw]
    qvg = _proj(x_b, jnp.concatenate([cols(0, 1), cols(3, 5)], axis=1), BF16)
    zz = _proj(x_b, cols(1, 3), F32)
    cos_t, sin_t = rope
    scale_row = jnp.concatenate([jnp.full((1, aw), B_HEAD_DIM ** -0.5, F32), jnp.ones((1, aw), F32)], axis=1)
    qk = _proj_rope(x_b, cols(5, 7), cos_t, sin_t, scale_row, seq)
    vb = _proj(x_b, cols(7, 8), BF16)
    oa = _hgrn(qvg.reshape(bsz, seq, -1), zz.reshape(bsz, seq, -1), lb, norm_w, n_heads_a)
    qk3 = qk.reshape(bsz, seq, 2 * aw)
    ob = _dilated_attention(qk3[:, :, :aw], qk3[:, :, aw:], vb.reshape(bsz, seq, aw))
    w_out = w_out.astype(BF16)
    x_f, x_b = _out_ln([oa.reshape(bsz * seq, aw), ob], [w_out[:aw], w_out[aw:]], x_f, ln1[0], ln1[1], alpha)
    return _ffn_ln(x_b, x_f, w1.astype(BF16), w3.astype(BF16), w2.astype(BF16), ln2[0], ln2[1], alpha)


def _moe_dispatch(idx, n_tokens, tm):
    e_flat = idx[:, :2].reshape(-1)
    onehot = (e_flat[:, None] == jnp.arange(N_EXPERTS, dtype=jnp.int32)[None, :]).astype(jnp.int32)
    rank = jnp.sum(jnp.cumsum(onehot, axis=0) * onehot, axis=1) - 1
    counts = jnp.sum(onehot, axis=0)
    tiles = (counts + tm - 1) // tm
    tile_end = jnp.cumsum(tiles)
    group_start = (tile_end - tiles) * tm
    dest = group_start[e_flat] + rank
    n_tiles = (2 * n_tokens) // tm + N_EXPERTS
    src_tok = jnp.zeros((n_tiles * tm,), jnp.int32).at[dest].set(jnp.arange(2 * n_tokens, dtype=jnp.int32) // 2)
    tile_expert = jnp.minimum(jnp.searchsorted(tile_end, jnp.arange(n_tiles, dtype=jnp.int32), side="right"),
                              N_EXPERTS - 1).astype(jnp.int32)
    return src_tok, dest.reshape(n_tokens, 2), tile_expert, tile_end[-1:].astype(jnp.int32)


def _odd_layer(x_f, x_b, bsz, seq, w_in, lam_params, subln_w, w_out, ln1, router, w1, w3, w2, ln2, rope, alpha,
               lambda_init):
    d = x_f.shape[1]
    n_tok = bsz * seq
    w_in = w_in.astype(BF16)
    cos_t, sin_t = rope
    scale_row = jnp.concatenate([jnp.full((1, d), C_HEAD_DIM ** -0.5, F32), jnp.ones((1, d), F32)], axis=1)
    qk = _proj_rope(x_b, w_in[:, :2 * d], cos_t, sin_t, scale_row, seq)
    v = _proj(x_b, w_in[:, 2 * d:], BF16)
    o = _diff_attention(qk.reshape(bsz, seq, 2 * d), v.reshape(bsz, seq, d), lam_params.astype(F32), subln_w,
                        lambda_init)
    x_f, x_b = _out_ln([o.reshape(n_tok, d)], [w_out.astype(BF16)], x_f, ln1[0], ln1[1], alpha)
    router_padded = jnp.pad(router.astype(F32), ((0, 0), (0, LANES - N_EXPERTS)))
    gates, idx = _router(x_f, router_padded)
    tm = _tile(n_tok, (1024, 512, 256))
    src_tok, pos, tile_expert, n_active = _moe_dispatch(idx, n_tok, tm)
    y = _moe_ffn(jnp.take(x_b, src_tok, axis=0), tile_expert, n_active, w1.astype(BF16), w3.astype(BF16),
                 w2.astype(BF16), tm)
    y1 = jnp.take(y, pos[:, 0], axis=0)
    y2 = jnp.take(y, pos[:, 1], axis=0)
    return _moe_combine(x_f, y1, y2, gates, ln2[0], ln2[1], alpha)


def kernel(x, ev_w_in, ev_lb_logits, ev_hgrn_norm, ev_w_out, ev_ln1_g, ev_ln1_b, ev_w1, ev_w3, ev_w2, ev_ln2_g,
           ev_ln2_b, od_w_in, od_lambda, od_subln, od_w_out, od_ln1_g, od_ln1_b, od_router, od_w1, od_w3, od_w2,
           od_ln2_g, od_ln2_b):
    bsz, seq, d = x.shape
    depth = ev_w_in.shape[0] + od_w_in.shape[0]
    alpha = (2 * depth) ** 0.25
    rope = _rope_tables(seq, 512)
    lb_soft = jax.nn.softmax(ev_lb_logits.astype(F32), axis=0)
    lower_bounds = jnp.cumsum(lb_soft, axis=0) - lb_soft[0]
    x_f = x.reshape(bsz * seq, d).astype(F32)
    x_b = x_f.astype(BF16)
    for layer in range(depth):
        j = layer // 2
        if layer % 2 == 0:
            x_f, x_b = _even_layer(x_f, x_b, bsz, seq, ev_w_in[j], lower_bounds[j], ev_hgrn_norm[j], ev_w_out[j],
                                   (ev_ln1_g[j], ev_ln1_b[j]), ev_w1[j], ev_w3[j], ev_w2[j],
                                   (ev_ln2_g[j], ev_ln2_b[j]), rope, alpha)
        else:
            lambda_init = 0.8 - 0.6 * math.exp(-0.3 * layer)
            x_f, x_b = _odd_layer(x_f, x_b, bsz, seq, od_w_in[j], od_lambda[j], od_subln[j], od_w_out[j],
                                  (od_ln1_g[j], od_ln1_b[j]), od_router[j], od_w1[j], od_w3[j], od_w2[j],
                                  (od_ln2_g[j], od_ln2_b[j]), rope, alpha, lambda_init)
    return x_f.reshape(bsz, seq, d).astype(x.dtype)
```

```python
import functools
import math

import numpy as np
import jax
import jax.numpy as jnp
from jax import lax
from jax.experimental import pallas as pl
from jax.experimental.pallas import tpu as pltpu

F32 = jnp.float32
BF16 = jnp.bfloat16

A_HEAD_DIM = 128
B_HEAD_DIM = 64
B_CONFIGS = ((128, 1), (512, 4), (2048, 16))
BAND_RADIUS = 64
C_HEAD_DIM = 64
N_EXPERTS = 8
ROPE_THETA = 10000.0
LN_EPS = 1e-5
RMS_EPS = 1e-5
MASK_VALUE = -1e30
MIN_FORGET = 1e-30
LOG2_E = math.log2(math.e)

LANES = 128
SUBLANES = 8
VMEM_LIMIT_BYTES = 56 * 1024 * 1024

HGRN_CHUNK = 128
HGRN_LEVELS = 7


def _cparams(*sem):
    return pltpu.CompilerParams(dimension_semantics=sem, vmem_limit_bytes=VMEM_LIMIT_BYTES)


def _tile(n, prefs):
    for p in prefs:
        if n % p == 0:
            return p
    return n


def _dot(a, b):
    return jnp.dot(a, b, preferred_element_type=F32)


def _dot_nt(a, b):
    return lax.dot_general(a, b, (((1,), (1,)), ((), ())), preferred_element_type=F32)


def _dot_tn(a, b):
    return lax.dot_general(a, b, (((0,), (0,)), ((), ())), preferred_element_type=F32)


def _sigmoid(x):
    return 1.0 / (1.0 + jnp.exp(-x))


def _layer_norm(y, g, b):
    mu = jnp.mean(y, axis=-1, keepdims=True)
    d = y - mu
    var = jnp.mean(d * d, axis=-1, keepdims=True)
    return d * lax.rsqrt(var + LN_EPS) * g + b


def _proj_kernel(x_ref, w_ref, o_ref):
    o_ref[...] = _dot(x_ref[...], w_ref[...]).astype(o_ref.dtype)


def _proj(x, w, out_dtype):
    m, k = x.shape
    n = w.shape[1]
    tm = _tile(m, (1024, 512, 256))
    tn = _tile(n, (512, 256, 128))
    return pl.pallas_call(
        _proj_kernel,
        grid=(m // tm, n // tn),
        in_specs=[pl.BlockSpec((tm, k), lambda i, j: (i, 0)), pl.BlockSpec((k, tn), lambda i, j: (0, j))],
        out_specs=pl.BlockSpec((tm, tn), lambda i, j: (i, j)),
        out_shape=jax.ShapeDtypeStruct((m, n), out_dtype),
        compiler_params=_cparams("parallel", "arbitrary"),
        name="proj",
    )(x, w)


def _proj_rope_kernel(x_ref, w_ref, cos_ref, sin_ref, scale_ref, o_ref):
    acc = _dot(x_ref[...], w_ref[...])
    tn = acc.shape[1]
    lane = lax.broadcasted_iota(jnp.int32, acc.shape, 1)
    first_half = (lane % B_HEAD_DIM) < (B_HEAD_DIM // 2)
    partner = jnp.where(first_half, pltpu.roll(acc, tn - B_HEAD_DIM // 2, 1), pltpu.roll(acc, B_HEAD_DIM // 2, 1))
    o_ref[...] = ((acc * cos_ref[...] + partner * sin_ref[...]) * scale_ref[...]).astype(o_ref.dtype)


def _proj_rope(x, w, cos_t, sin_t, scale_row, seq):
    m, k = x.shape
    n = w.shape[1]
    tm = _tile(seq, (1024, 512, 256))
    tn = cos_t.shape[1]
    nsb = seq // tm
    return pl.pallas_call(
        _proj_rope_kernel,
        grid=(m // tm, n // tn),
        in_specs=[
            pl.BlockSpec((tm, k), lambda i, j: (i, 0)),
            pl.BlockSpec((k, tn), lambda i, j: (0, j)),
            pl.BlockSpec((tm, tn), lambda i, j: (i % nsb, 0)),
            pl.BlockSpec((tm, tn), lambda i, j: (i % nsb, 0)),
            pl.BlockSpec((1, tn), lambda i, j: (0, j)),
        ],
        out_specs=pl.BlockSpec((tm, tn), lambda i, j: (i, j)),
        out_shape=jax.ShapeDtypeStruct((m, n), BF16),
        compiler_params=_cparams("parallel", "arbitrary"),
        name="proj_rope",
    )(x, w, cos_t, sin_t, scale_row)


def _out_ln_kernel(*refs, n_in, alpha):
    xs = refs[:n_in]
    ws = refs[n_in:2 * n_in]
    resid_ref, g_ref, b_ref, of_ref, ob_ref = refs[2 * n_in:]
    acc = _dot(xs[0][...], ws[0][...])
    for x_ref, w_ref in zip(xs[1:], ws[1:]):
        acc = acc + _dot(x_ref[...], w_ref[...])
    z = _layer_norm(alpha * resid_ref[...] + acc, g_ref[...], b_ref[...])
    of_ref[...] = z
    ob_ref[...] = z.astype(BF16)


def _out_ln(xs, ws, resid, g, b, alpha):
    m, d = resid.shape
    tm = _tile(m, (512, 256))
    n_in = len(xs)
    in_specs = [pl.BlockSpec((tm, x.shape[1]), lambda i: (i, 0)) for x in xs]
    in_specs += [pl.BlockSpec(w.shape, lambda i: (0, 0)) for w in ws]
    in_specs += [pl.BlockSpec((tm, d), lambda i: (i, 0)), pl.BlockSpec((1, d), lambda i: (0, 0)),
                 pl.BlockSpec((1, d), lambda i: (0, 0))]
    return pl.pallas_call(
        functools.partial(_out_ln_kernel, n_in=n_in, alpha=alpha),
        grid=(m // tm,),
        in_specs=in_specs,
        out_specs=[pl.BlockSpec((tm, d), lambda i: (i, 0)), pl.BlockSpec((tm, d), lambda i: (i, 0))],
        out_shape=[jax.ShapeDtypeStruct((m, d), F32), jax.ShapeDtypeStruct((m, d), BF16)],
        compiler_params=_cparams("parallel"),
        name="out_ln",
    )(*xs, *ws, resid, g.reshape(1, d), b.reshape(1, d))


def _swiglu_partial(x, w1, w3, w2):
    h1 = _dot(x, w1.astype(BF16))
    h3 = _dot(x, w3.astype(BF16))
    h = (h1 * _sigmoid(h1)) * h3
    return _dot(h.astype(BF16), w2.astype(BF16))


def _ffn_ln_kernel(x_ref, w1_ref, w3_ref, w2_ref, resid_ref, g_ref, b_ref, of_ref, ob_ref, acc_ref, *, alpha):
    j = pl.program_id(1)

    @pl.when(j == 0)
    def _():
        acc_ref[...] = jnp.zeros_like(acc_ref)

    acc_ref[...] += _swiglu_partial(x_ref[...], w1_ref[...], w3_ref[...], w2_ref[...])

    @pl.when(j == pl.num_programs(1) - 1)
    def _():
        z = _layer_norm(alpha * resid_ref[...] + acc_ref[...], g_ref[...], b_ref[...])
        of_ref[...] = z
        ob_ref[...] = z.astype(BF16)


def _ffn_ln(x_bf, resid, w1, w3, w2, g, b, alpha):
    m, d = resid.shape
    ff = w1.shape[1]
    tm = _tile(m, (1024, 512, 256))
    tf = _tile(ff, (256, 128))
    return pl.pallas_call(
        functools.partial(_ffn_ln_kernel, alpha=alpha),
        grid=(m // tm, ff // tf),
        in_specs=[
            pl.BlockSpec((tm, d), lambda i, j: (i, 0)),
            pl.BlockSpec((d, tf), lambda i, j: (0, j)),
            pl.BlockSpec((d, tf), lambda i, j: (0, j)),
            pl.BlockSpec((tf, d), lambda i, j: (j, 0)),
            pl.BlockSpec((tm, d), lambda i, j: (i, 0)),
            pl.BlockSpec((1, d), lambda i, j: (0, 0)),
            pl.BlockSpec((1, d), lambda i, j: (0, 0)),
        ],
        out_specs=[pl.BlockSpec((tm, d), lambda i, j: (i, 0)), pl.BlockSpec((tm, d), lambda i, j: (i, 0))],
        out_shape=[jax.ShapeDtypeStruct((m, d), F32), jax.ShapeDtypeStruct((m, d), BF16)],
        scratch_shapes=[pltpu.VMEM((tm, d), F32)],
        compiler_params=_cparams("parallel", "arbitrary"),
        name="ffn_ln",
    )(x_bf, w1, w3, w2, resid, g.reshape(1, d), b.reshape(1, d))


def _moe_ffn_kernel(te_ref, na_ref, x_ref, w1_ref, w3_ref, w2_ref, o_ref, acc_ref):
    i = pl.program_id(0)
    j = pl.program_id(1)
    active = i < na_ref[0]

    @pl.when(active & (j == 0))
    def _():
        acc_ref[...] = jnp.zeros_like(acc_ref)

    @pl.when(active)
    def _():
        acc_ref[...] += _swiglu_partial(x_ref[...], w1_ref[0], w3_ref[0], w2_ref[0])

    @pl.when(active & (j == pl.num_programs(1) - 1))
    def _():
        o_ref[...] = acc_ref[...].astype(o_ref.dtype)


def _moe_ffn(x_sorted, tile_expert, n_active, w1, w3, w2, tm):
    p, d = x_sorted.shape
    ff = w1.shape[2]
    tf = _tile(ff, (256, 128))
    nf = ff // tf

    def row(i, na):
        return jnp.minimum(i, na[0] - 1)

    def col(i, j, na):
        return jnp.where(i < na[0], j, nf - 1)

    grid_spec = pltpu.PrefetchScalarGridSpec(
        num_scalar_prefetch=2,
        grid=(p // tm, nf),
        in_specs=[
            pl.BlockSpec((tm, d), lambda i, j, te, na: (row(i, na), 0)),
            pl.BlockSpec((1, d, tf), lambda i, j, te, na: (te[row(i, na)], 0, col(i, j, na))),
            pl.BlockSpec((1, d, tf), lambda i, j, te, na: (te[row(i, na)], 0, col(i, j, na))),
            pl.BlockSpec((1, tf, d), lambda i, j, te, na: (te[row(i, na)], col(i, j, na), 0)),
        ],
        out_specs=pl.BlockSpec((tm, d), lambda i, j, te, na: (row(i, na), 0)),
        scratch_shapes=[pltpu.VMEM((tm, d), F32)],
    )
    return pl.pallas_call(
        _moe_ffn_kernel,
        grid_spec=grid_spec,
        out_shape=jax.ShapeDtypeStruct((p, d), BF16),
        compiler_params=_cparams("arbitrary", "arbitrary"),
        name="moe_ffn",
    )(tile_expert, n_active, x_sorted, w1, w3, w2)


def _router_kernel(x_ref, r_ref, gate_ref, idx_ref):
    logits = jnp.dot(x_ref[...], r_ref[...], precision=lax.Precision.HIGHEST, preferred_element_type=F32)
    lane = lax.broadcasted_iota(jnp.int32, logits.shape, 1)
    neg = jnp.float32(-jnp.inf)
    logits = jnp.where(lane < N_EXPERTS, logits, neg)
    v1 = jnp.max(logits, axis=-1, keepdims=True)
    i1 = jnp.min(jnp.where(logits == v1, lane, LANES), axis=-1, keepdims=True)
    rest = jnp.where(lane == i1, neg, logits)
    v2 = jnp.max(rest, axis=-1, keepdims=True)
    i2 = jnp.min(jnp.where(rest == v2, lane, LANES), axis=-1, keepdims=True)
    e = jnp.exp(v2 - v1)
    g1 = 1.0 / (1.0 + e)
    g2 = e / (1.0 + e)
    gate_ref[...] = jnp.where(lane == 0, g1, jnp.where(lane == 1, g2, 0.0))
    idx_ref[...] = jnp.where(lane == 0, i1, jnp.where(lane == 1, i2, 0))


def _router(x_f32, router_padded):
    m, d = x_f32.shape
    tm = _tile(m, (1024, 512, 256))
    return pl.pallas_call(
        _router_kernel,
        grid=(m // tm,),
        in_specs=[pl.BlockSpec((tm, d), lambda i: (i, 0)), pl.BlockSpec((d, LANES), lambda i: (0, 0))],
        out_specs=[pl.BlockSpec((tm, LANES), lambda i: (i, 0)), pl.BlockSpec((tm, LANES), lambda i: (i, 0))],
        out_shape=[jax.ShapeDtypeStruct((m, LANES), F32), jax.ShapeDtypeStruct((m, LANES), jnp.int32)],
        compiler_params=_cparams("parallel"),
        name="router",
    )(x_f32, router_padded)


def _moe_combine_kernel(resid_ref, y1_ref, y2_ref, gate_ref, g_ref, b_ref, of_ref, ob_ref, *, alpha):
    gates = gate_ref[...]
    y = gates[:, 0:1] * y1_ref[...].astype(F32) + gates[:, 1:2] * y2_ref[...].astype(F32)
    z = _layer_norm(alpha * resid_ref[...] + y, g_ref[...], b_ref[...])
    of_ref[...] = z
    ob_ref[...] = z.astype(BF16)


def _moe_combine(resid, y1, y2, gates, g, b, alpha):
    m, d = resid.shape
    tm = _tile(m, (512, 256))
    row = lambda i: (i, 0)
    fixed = lambda i: (0, 0)
    return pl.pallas_call(
        functools.partial(_moe_combine_kernel, alpha=alpha),
        grid=(m // tm,),
        in_specs=[pl.BlockSpec((tm, d), row), pl.BlockSpec((tm, d), row), pl.BlockSpec((tm, d), row),
                  pl.BlockSpec((tm, LANES), row), pl.BlockSpec((1, d), fixed), pl.BlockSpec((1, d), fixed)],
        out_specs=[pl.BlockSpec((tm, d), row), pl.BlockSpec((tm, d), row)],
        out_shape=[jax.ShapeDtypeStruct((m, d), F32), jax.ShapeDtypeStruct((m, d), BF16)],
        compiler_params=_cparams("parallel"),
        name="moe_combine",
    )(resid, y1, y2, gates, g.reshape(1, d), b.reshape(1, d))


def _hgrn_masks(c, levels):
    t = np.arange(c)[:, None]
    s = np.arange(c)[None, :]
    tri = np.stack([(s <= t), (s >= t)]).astype(np.float32)
    fwd, bwd = [], []
    for l in range(levels):
        same = (t >> (l + 1)) == (s >> (l + 1))
        t_up = ((t >> l) & 1) == 1
        s_up = ((s >> l) & 1) == 1
        fwd.append(same & t_up & ~s_up)
        bwd.append(same & ~t_up & s_up)
    fwd.append(t == s)
    bwd.append(t == s)
    return tri, np.stack([np.stack(fwd), np.stack(bwd)]).astype(np.float32)


def _segment_reference(x, level, forward):
    c, w = x.shape
    half = 1 << level
    seg = 2 * half
    idx = half - 1 if forward else half
    if seg >= SUBLANES:
        xr = x.reshape(c // seg, seg, w)
        return jnp.broadcast_to(xr[:, idx:idx + 1, :], xr.shape).reshape(c, w)
    x3 = x.reshape(c // SUBLANES, SUBLANES, w)
    sub = lax.broadcasted_iota(jnp.int32, x3.shape, 1)
    r3 = jnp.broadcast_to(x3[:, idx:idx + 1, :], x3.shape)
    for j in range(1, SUBLANES // seg):
        row = j * seg + idx
        r3 = jnp.where(sub >= j * seg, jnp.broadcast_to(x3[:, row:row + 1, :], x3.shape), r3)
    return r3.reshape(c, w)


def _hgrn_chunk(q, k, v, log_f, state_t, tri, masks, forward):
    c = q.shape[0]
    x = jnp.dot(tri, log_f, precision=lax.Precision.HIGHEST, preferred_element_type=F32)
    row = lax.broadcasted_iota(jnp.int32, q.shape, 0)
    scores = masks[HGRN_LEVELS] * _dot_nt(q.astype(BF16), k.astype(BF16))
    for level in range(HGRN_LEVELS):
        ref = _segment_reference(x, level, forward)
        upper = ((row >> level) & 1) == 1
        is_query = upper if forward else jnp.logical_not(upper)
        decay = jnp.exp(jnp.where(is_query, x - ref, ref - x))
        z = (jnp.where(is_query, q, k) * decay).astype(BF16)
        scores = scores + masks[level] * _dot_nt(z, z)
    x_end = x[c - 1:c, :] if forward else x[0:1, :]
    q_dec = (q * jnp.exp(x)).astype(BF16)
    o = _dot(scores.astype(BF16), v.astype(BF16)) + _dot_nt(q_dec, state_t.astype(BF16))
    k_dec = (k * jnp.exp(x_end - x)).astype(BF16)
    new_state_t = state_t * jnp.exp(x_end) + _dot_tn(v.astype(BF16), k_dec)
    return o, new_state_t


def _hgrn_kernel(q_ref, v_ref, g_ref, zf_ref, zb_ref, lb_ref, nw_ref, tri_ref, msk_ref, o_ref, acc_ref, *, n_chunks):
    c = HGRN_CHUNK
    lb = lb_ref[...]
    one_minus_lb = 1.0 - lb

    def gates(z):
        e = jnp.exp(-jnp.abs(z))
        inv = 1.0 / (1.0 + e)
        pos = z >= 0
        sig = jnp.where(pos, inv, e * inv)
        sig_neg = jnp.where(pos, e * inv, inv)
        f = lb + one_minus_lb * sig
        return jnp.log(jnp.maximum(f, MIN_FORGET)), one_minus_lb * sig_neg

    def load(ref, c0):
        return ref[0, pl.ds(c0, c), :].astype(F32)

    def emit(c0, tot):
        ms = jnp.mean(tot * tot, axis=-1, keepdims=True)
        g = load(g_ref, c0)
        out = tot * lax.rsqrt(ms + RMS_EPS) * nw_ref[...] * (g * _sigmoid(g))
        o_ref[0, pl.ds(c0, c), :] = out.astype(o_ref.dtype)

    def step(i, states, finalize):
        st_f, st_b = states
        cf = pl.multiple_of(i * c, c)
        cb = pl.multiple_of((n_chunks - 1 - i) * c, c)
        log_f, k_f = gates(load(zf_ref, cf))
        masks_f = [msk_ref[0, l] for l in range(HGRN_LEVELS + 1)]
        o_f, st_f = _hgrn_chunk(load(q_ref, cf), k_f, load(v_ref, cf), log_f, st_f, tri_ref[0], masks_f, True)
        log_b, k_b = gates(load(zb_ref, cb))
        masks_b = [msk_ref[1, l] for l in range(HGRN_LEVELS + 1)]
        o_b, st_b = _hgrn_chunk(load(q_ref, cb), k_b, load(v_ref, cb), log_b, st_b, tri_ref[1], masks_b, False)
        if finalize:
            emit(cf, acc_ref[pl.ds(cf, c), :] + o_f)
            emit(cb, acc_ref[pl.ds(cb, c), :] + o_b)
        else:
            acc_ref[pl.ds(cf, c), :] = o_f
            acc_ref[pl.ds(cb, c), :] = o_b
        return st_f, st_b

    zero = jnp.zeros((A_HEAD_DIM, A_HEAD_DIM), F32)
    half = n_chunks // 2
    states = lax.fori_loop(0, half, functools.partial(step, finalize=False), (zero, zero))
    lax.fori_loop(half, n_chunks, functools.partial(step, finalize=True), states)


def _hgrn(qvg, zz, lb, norm_w, n_heads):
    bsz, s, _ = qvg.shape
    c = HGRN_CHUNK
    assert s % (2 * c) == 0
    tri, masks = _hgrn_masks(c, HGRN_LEVELS)
    hd = A_HEAD_DIM
    col = lambda off: pl.BlockSpec((1, s, hd), lambda b, h: (b, 0, off + h))
    return pl.pallas_call(
        functools.partial(_hgrn_kernel, n_chunks=s // c),
        grid=(bsz, n_heads),
        in_specs=[
            col(0), col(n_heads), col(2 * n_heads),
            pl.BlockSpec((1, s, hd), lambda b, h: (b, 0, h)),
            pl.BlockSpec((1, s, hd), lambda b, h: (b, 0, n_heads + h)),
            pl.BlockSpec((1, hd), lambda b, h: (0, h)),
            pl.BlockSpec((1, hd), lambda b, h: (0, 0)),
            pl.BlockSpec(tri.shape, lambda b, h: (0, 0, 0)),
            pl.BlockSpec(masks.shape, lambda b, h: (0, 0, 0, 0)),
        ],
        out_specs=pl.BlockSpec((1, s, hd), lambda b, h: (b, 0, h)),
        out_shape=jax.ShapeDtypeStruct((bsz, s, n_heads * hd), BF16),
        scratch_shapes=[pltpu.VMEM((s, hd), F32)],
        compiler_params=_cparams("parallel", "parallel"),
        name="hgrn2",
    )(qvg, qvg, qvg, zz, zz, lb.reshape(1, -1), norm_w.reshape(1, hd), jnp.asarray(tri), jnp.asarray(masks))


def _dilated_kernel(q_ref, k_ref, v_ref, o_ref, qf_ref, kf_ref, vf_ref, oc_ref, lc_ref, *, seq):
    qf_ref[...] = q_ref[0].astype(F32)
    kf_ref[...] = k_ref[0].astype(F32)
    vf_ref[...] = v_ref[0].astype(F32)
    n_cfg = len(B_CONFIGS)
    for ci, (_, dil) in enumerate(B_CONFIGS):
        length = seq // dil
        tq = min(LANES, length)
        win = min(length, tq + 2 * BAND_RADIUS)
        head0 = lax.broadcasted_iota(jnp.int32, (tq, LANES), 1) < B_HEAD_DIM
        rel = lax.broadcasted_iota(jnp.int32, (tq, win), 1) - lax.broadcasted_iota(jnp.int32, (tq, win), 0)

        def rows(first, size, dil=dil):
            return pl.ds(first, size) if dil == 1 else pl.ds(first, size, stride=dil)

        def block(t, carry, ci=ci, dil=dil, length=length, tq=tq, win=win, head0=head0, rel=rel, rows=rows):
            res = t % dil
            q0 = (t // dil) * tq
            start = jnp.clip(q0 - BAND_RADIUS, 0, length - win)
            valid = jnp.abs(rel + (start - q0)) <= BAND_RADIUS
            q_rows = rows(q0 * dil + res, tq)
            k_rows = rows(start * dil + res, win)
            q = qf_ref[q_rows, :].astype(BF16)
            kw = kf_ref[k_rows, :].astype(BF16)
            vw = vf_ref[k_rows, :].astype(BF16)

            def one_head(mask):
                s = _dot_nt(jnp.where(mask, q, jnp.zeros_like(q)), kw)
                s = jnp.where(valid, s, MASK_VALUE)
                m = jnp.max(s, axis=-1, keepdims=True)
                p = jnp.exp2(s - m)
                l = jnp.sum(p, axis=-1, keepdims=True)
                return _dot(p.astype(BF16), vw) / l, m + jnp.log(l) * LOG2_E

            oa, la = one_head(head0)
            ob, lb = one_head(jnp.logical_not(head0))
            oc_ref[ci, q_rows, :] = jnp.where(head0, oa, ob)
            lc_ref[ci, q_rows, :] = jnp.where(head0, la, lb)
            return carry

        lax.fori_loop(0, dil * (length // tq), block, 0, unroll=2)

    tmix = min(256, seq)

    def mix(i, carry):
        r0 = pl.multiple_of(i * tmix, tmix)
        lses = [lc_ref[c, pl.ds(r0, tmix), :] for c in range(n_cfg)]
        top = lses[0]
        for l in lses[1:]:
            top = jnp.maximum(top, l)
        num = jnp.zeros((tmix, LANES), F32)
        den = jnp.zeros((tmix, LANES), F32)
        for c in range(n_cfg):
            w = jnp.exp2(lses[c] - top)
            num = num + w * oc_ref[c, pl.ds(r0, tmix), :]
            den = den + w
        o_ref[0, pl.ds(r0, tmix), :] = (num / den).astype(o_ref.dtype)
        return carry

    lax.fori_loop(0, seq // tmix, mix, 0)


def _dilated_attention(qk, v):
    bsz, s, w = v.shape
    n_pairs = w // LANES
    n_cfg = len(B_CONFIGS)
    return pl.pallas_call(
        functools.partial(_dilated_kernel, seq=s),
        grid=(bsz, n_pairs),
        in_specs=[
            pl.BlockSpec((1, s, LANES), lambda b, h: (b, 0, h)),
            pl.BlockSpec((1, s, LANES), lambda b, h: (b, 0, n_pairs + h)),
            pl.BlockSpec((1, s, LANES), lambda b, h: (b, 0, h)),
        ],
        out_specs=pl.BlockSpec((1, s, LANES), lambda b, h: (b, 0, h)),
        out_shape=jax.ShapeDtypeStruct((bsz, s, w), BF16),
        scratch_shapes=[pltpu.VMEM((s, LANES), F32)] * 3 + [pltpu.VMEM((n_cfg, s, LANES), F32)] * 2,
        compiler_params=_cparams("parallel", "parallel"),
        name="dilated_attn",
    )(qk, qk, v)


def _diff_kernel(q_ref, k_ref, v_ref, lam_ref, sub_ref, o_ref, s_ref, p_ref, *, tk, lambda_init):
    q = q_ref[0]
    tq = q.shape[0]
    s_len = k_ref.shape[1]
    lane = lax.broadcasted_iota(jnp.int32, q.shape, 1)
    zero = jnp.zeros_like(q)
    qs = (jnp.where(lane < C_HEAD_DIM, q, zero), jnp.where(lane >= C_HEAD_DIM, q, zero))
    chunks = [(c * tk, (c + 1) * tk) for c in range(s_len // tk)]
    tiles = [(t * LANES, (t + 1) * LANES) for t in range(tk // LANES)]
    row_max = []
    for h in range(2):
        m_run = jnp.full((tq, LANES), -jnp.inf, F32)
        for lo, hi in chunks:
            s = _dot_nt(qs[h], k_ref[0, lo:hi, :])
            s_ref[h, :, lo:hi] = s
            for a, b in tiles:
                m_run = jnp.maximum(m_run, s[:, a:b])
        row_max.append(jnp.max(m_run, axis=-1, keepdims=True))
    outs = []
    for h in range(2):
        l_run = jnp.zeros((tq, LANES), F32)
        for lo, hi in chunks:
            p = jnp.exp2(s_ref[h, :, lo:hi] - row_max[h])
            for a, b in tiles:
                l_run = l_run + p[:, a:b]
            p_ref[h, :, lo:hi] = p.astype(BF16)
        outs.append(_dot(p_ref[h], v_ref[0]) / jnp.sum(l_run, axis=-1, keepdims=True))
    lp = lam_ref[...]
    lam = (jnp.exp(jnp.sum(lp[0:1] * lp[1:2], axis=-1, keepdims=True))
           - jnp.exp(jnp.sum(lp[2:3] * lp[3:4], axis=-1, keepdims=True)) + lambda_init)
    o = outs[0] - lam * outs[1]
    ms_o = jnp.mean(o * o, axis=-1, keepdims=True)
    o_ref[0] = (o * lax.rsqrt(ms_o + RMS_EPS) * sub_ref[...] * (1.0 - lambda_init)).astype(o_ref.dtype)


def _diff_attention(qk, v, lam_params, subln_w, lambda_init):
    bsz, s, w = v.shape
    n_heads = w // LANES
    tq = _tile(s, (512, 256, 128))
    tk = _tile(s, (1024, 512, 256, 128))
    return pl.pallas_call(
        functools.partial(_diff_kernel, tk=tk, lambda_init=lambda_init),
        grid=(bsz, n_heads, s // tq),
        in_specs=[
            pl.BlockSpec((1, tq, LANES), lambda b, h, i: (b, i, h)),
            pl.BlockSpec((1, s, LANES), lambda b, h, i: (b, 0, n_heads + h)),
            pl.BlockSpec((1, s, LANES), lambda b, h, i: (b, 0, h)),
            pl.BlockSpec(lam_params.shape, lambda b, h, i: (0, 0)),
            pl.BlockSpec((1, LANES), lambda b, h, i: (0, 0)),
        ],
        out_specs=pl.BlockSpec((1, tq, LANES), lambda b, h, i: (b, i, h)),
        out_shape=jax.ShapeDtypeStruct((bsz, s, w), BF16),
        scratch_shapes=[pltpu.VMEM((2, tq, s), F32), pltpu.VMEM((2, tq, s), BF16)],
        compiler_params=_cparams("parallel", "parallel", "arbitrary"),
        name="diff_attn",
    )(qk, qk, v, lam_params, subln_w.reshape(1, LANES))


def _rope_tables(seq, width):
    half = B_HEAD_DIM // 2
    inv = ROPE_THETA ** (-jnp.arange(0, B_HEAD_DIM, 2, dtype=F32) / B_HEAD_DIM)
    ang = jnp.arange(seq, dtype=F32)[:, None] * inv[None, :]
    cos, sin = jnp.cos(ang), jnp.sin(ang)
    reps = width // B_HEAD_DIM
    assert half * 2 == B_HEAD_DIM
    return jnp.tile(jnp.concatenate([cos, cos], axis=1), (1, reps)), jnp.tile(jnp.concatenate([-sin, sin], axis=1), (1, reps))


def _even_layer(x_f, x_b, bsz, seq, w_in, lb, norm_w, w_out, ln1, w1, w3, w2, ln2, rope, alpha):
    d = x_f.shape[1]
    aw = d // 2
    n_heads_a = aw // A_HEAD_DIM
    w_in = w_in.astype(BF16)
    cols = lambda a, b: w_in[:, a * aw:b * aw]
    qvg = _proj(x_b, jnp.concatenate([cols(0, 1), cols(3, 5)], axis=1), BF16)
    zz = _proj(x_b, cols(1, 3), F32)
    cos_t, sin_t = rope
    scale_row = jnp.concatenate([jnp.full((1, aw), LOG2_E * B_HEAD_DIM ** -0.5, F32), jnp.ones((1, aw), F32)], axis=1)
    qk = _proj_rope(x_b, cols(5, 7), cos_t, sin_t, scale_row, seq)
    vb = _proj(x_b, cols(7, 8), BF16)
    oa = _hgrn(qvg.reshape(bsz, seq, -1), zz.reshape(bsz, seq, -1), lb, norm_w, n_heads_a)
    ob = _dilated_attention(qk.reshape(bsz, seq, 2 * aw), vb.reshape(bsz, seq, aw))
    w_out = w_out.astype(BF16)
    x_f, x_b = _out_ln([oa.reshape(bsz * seq, aw), ob.reshape(bsz * seq, aw)], [w_out[:aw], w_out[aw:]], x_f,
                       ln1[0], ln1[1], alpha)
    return _ffn_ln(x_b, x_f, w1, w3, w2, ln2[0], ln2[1], alpha)


def _moe_dispatch(idx, n_tokens, tm):
    e_flat = idx[:, :2].reshape(-1)
    onehot = (e_flat[:, None] == jnp.arange(N_EXPERTS, dtype=jnp.int32)[None, :]).astype(jnp.int32)
    rank = jnp.sum(jnp.cumsum(onehot, axis=0) * onehot, axis=1) - 1
    counts = jnp.sum(onehot, axis=0)
    tiles = (counts + tm - 1) // tm
    tile_end = jnp.cumsum(tiles)
    group_start = (tile_end - tiles) * tm
    dest = group_start[e_flat] + rank
    n_tiles = (2 * n_tokens) // tm + N_EXPERTS
    src_tok = jnp.zeros((n_tiles * tm,), jnp.int32).at[dest].set(jnp.arange(2 * n_tokens, dtype=jnp.int32) // 2)
    tile_ids = jnp.arange(n_tiles, dtype=jnp.int32)
    tile_expert = jnp.minimum(jnp.sum((tile_ids[:, None] >= tile_end[None, :]).astype(jnp.int32), axis=1),
                              N_EXPERTS - 1)
    return src_tok, dest.reshape(n_tokens, 2), tile_expert, tile_end[-1:].astype(jnp.int32)


def _odd_layer(x_f, x_b, bsz, seq, w_in, lam_params, subln_w, w_out, ln1, router, w1, w3, w2, ln2, rope, alpha,
               lambda_init):
    d = x_f.shape[1]
    n_tok = bsz * seq
    w_in = w_in.astype(BF16)
    cos_t, sin_t = rope
    scale_row = jnp.concatenate([jnp.full((1, d), LOG2_E * C_HEAD_DIM ** -0.5, F32), jnp.ones((1, d), F32)], axis=1)
    qk = _proj_rope(x_b, w_in[:, :2 * d], cos_t, sin_t, scale_row, seq)
    v = _proj(x_b, w_in[:, 2 * d:], BF16)
    o = _diff_attention(qk.reshape(bsz, seq, 2 * d), v.reshape(bsz, seq, d), lam_params.astype(F32), subln_w,
                        lambda_init)
    x_f, x_b = _out_ln([o.reshape(n_tok, d)], [w_out.astype(BF16)], x_f, ln1[0], ln1[1], alpha)
    router_padded = jnp.pad(router.astype(F32), ((0, 0), (0, LANES - N_EXPERTS)))
    gates, idx = _router(x_f, router_padded)
    tm = _tile(n_tok, (1024, 512, 256))
    src_tok, pos, tile_expert, n_active = _moe_dispatch(idx, n_tok, tm)
    y = _moe_ffn(jnp.take(x_b, src_tok, axis=0), tile_expert, n_active, w1, w3, w2, tm)
    y1 = jnp.take(y, pos[:, 0], axis=0)
    y2 = jnp.take(y, pos[:, 1], axis=0)
    return _moe_combine(x_f, y1, y2, gates, ln2[0], ln2[1], alpha)


def kernel(x, ev_w_in, ev_lb_logits, ev_hgrn_norm, ev_w_out, ev_ln1_g, ev_ln1_b, ev_w1, ev_w3, ev_w2, ev_ln2_g,
           ev_ln2_b, od_w_in, od_lambda, od_subln, od_w_out, od_ln1_g, od_ln1_b, od_router, od_w1, od_w3, od_w2,
           od_ln2_g, od_ln2_b):
    bsz, seq, d = x.shape
    depth = ev_w_in.shape[0] + od_w_in.shape[0]
    alpha = (2 * depth) ** 0.25
    rope = _rope_tables(seq, 512)
    lb_soft = jax.nn.softmax(ev_lb_logits.astype(F32), axis=0)
    lower_bounds = jnp.cumsum(lb_soft, axis=0) - lb_soft[0]
    x_f = x.reshape(bsz * seq, d).astype(F32)
    x_b = x_f.astype(BF16)
    for layer in range(depth):
        j = layer // 2
        if layer % 2 == 0:
            x_f, x_b = _even_layer(x_f, x_b, bsz, seq, ev_w_in[j], lower_bounds[j], ev_hgrn_norm[j], ev_w_out[j],
                                   (ev_ln1_g[j], ev_ln1_b[j]), ev_w1[j], ev_w3[j], ev_w2[j],
                                   (ev_ln2_g[j], ev_ln2_b[j]), rope, alpha)
        else:
            lambda_init = 0.8 - 0.6 * math.exp(-0.3 * layer)
            x_f, x_b = _odd_layer(x_f, x_b, bsz, seq, od_w_in[j], od_lambda[j], od_subln[j], od_w_out[j],
                                  (od_ln1_g[j], od_ln1_b[j]), od_router[j], od_w1[j], od_w3[j], od_w2[j],
                                  (od_ln2_g[j], od_ln2_b[j]), rope, alpha, lambda_init)
    return x_f.reshape(bsz, seq, d).astype(x.dtype)
```

```python
import functools
import math

import numpy as np
import jax
import jax.numpy as jnp
from jax import lax
from jax.experimental import pallas as pl
from jax.experimental.pallas import tpu as pltpu

F32 = jnp.float32
BF16 = jnp.bfloat16

A_HEAD_DIM = 128
B_HEAD_DIM = 64
B_CONFIGS = ((128, 1), (512, 4), (2048, 16))
BAND_RADIUS = 64
C_HEAD_DIM = 64
N_EXPERTS = 8
ROPE_THETA = 10000.0
LN_EPS = 1e-5
RMS_EPS = 1e-5
MASK_VALUE = -1e30
MIN_FORGET = 1e-30
LOG2_E = math.log2(math.e)

LANES = 128
SUBLANES = 8
VMEM_LIMIT_BYTES = 56 * 1024 * 1024

HGRN_CHUNK = 128
HGRN_LEVELS = 7
HGRN_HEADS_PER_STEP = 2


def _cparams(*sem):
    return pltpu.CompilerParams(dimension_semantics=sem, vmem_limit_bytes=VMEM_LIMIT_BYTES)


def _tile(n, prefs):
    for p in prefs:
        if n % p == 0:
            return p
    return n


def _dot(a, b):
    return jnp.dot(a, b, preferred_element_type=F32)


def _dot_nt(a, b):
    return lax.dot_general(a, b, (((1,), (1,)), ((), ())), preferred_element_type=F32)


def _dot_tn(a, b):
    return lax.dot_general(a, b, (((0,), (0,)), ((), ())), preferred_element_type=F32)


def _sigmoid(x):
    return 1.0 / (1.0 + jnp.exp(-x))


def _layer_norm(y, g, b):
    mu = jnp.mean(y, axis=-1, keepdims=True)
    d = y - mu
    var = jnp.mean(d * d, axis=-1, keepdims=True)
    return d * lax.rsqrt(var + LN_EPS) * g + b


def _proj_kernel(x_ref, w_ref, o_ref):
    o_ref[...] = _dot(x_ref[...], w_ref[...]).astype(o_ref.dtype)


def _proj(x, w, out_dtype):
    m, k = x.shape
    n = w.shape[1]
    tm = _tile(m, (1024, 512, 256))
    tn = n if n <= 1536 else _tile(n, (1024, 512, 256, 128))
    return pl.pallas_call(
        _proj_kernel,
        grid=(m // tm, n // tn),
        in_specs=[pl.BlockSpec((tm, k), lambda i, j: (i, 0)), pl.BlockSpec((k, tn), lambda i, j: (0, j))],
        out_specs=pl.BlockSpec((tm, tn), lambda i, j: (i, j)),
        out_shape=jax.ShapeDtypeStruct((m, n), out_dtype),
        compiler_params=_cparams("parallel", "arbitrary"),
        name="proj",
    )(x, w)


def _proj_rope_kernel(x_ref, w_ref, cos_ref, sin_ref, scale_ref, o_ref):
    acc = _dot(x_ref[...], w_ref[...])
    tn = acc.shape[1]
    lane = lax.broadcasted_iota(jnp.int32, acc.shape, 1)
    first_half = (lane % B_HEAD_DIM) < (B_HEAD_DIM // 2)
    partner = jnp.where(first_half, pltpu.roll(acc, tn - B_HEAD_DIM // 2, 1), pltpu.roll(acc, B_HEAD_DIM // 2, 1))
    o_ref[...] = ((acc * cos_ref[...] + partner * sin_ref[...]) * scale_ref[...]).astype(o_ref.dtype)


def _proj_rope(x, w, cos_t, sin_t, scale_row, seq):
    m, k = x.shape
    n = w.shape[1]
    tm = _tile(seq, (1024, 512, 256))
    tn = cos_t.shape[1]
    nsb = seq // tm
    return pl.pallas_call(
        _proj_rope_kernel,
        grid=(m // tm, n // tn),
        in_specs=[
            pl.BlockSpec((tm, k), lambda i, j: (i, 0)),
            pl.BlockSpec((k, tn), lambda i, j: (0, j)),
            pl.BlockSpec((tm, tn), lambda i, j: (i % nsb, 0)),
            pl.BlockSpec((tm, tn), lambda i, j: (i % nsb, 0)),
            pl.BlockSpec((1, tn), lambda i, j: (0, j)),
        ],
        out_specs=pl.BlockSpec((tm, tn), lambda i, j: (i, j)),
        out_shape=jax.ShapeDtypeStruct((m, n), BF16),
        compiler_params=_cparams("parallel", "arbitrary"),
        name="proj_rope",
    )(x, w, cos_t, sin_t, scale_row)


def _out_ln_kernel(*refs, n_in, alpha):
    xs = refs[:n_in]
    ws = refs[n_in:2 * n_in]
    resid_ref, g_ref, b_ref, of_ref, ob_ref = refs[2 * n_in:]
    acc = _dot(xs[0][...], ws[0][...])
    for x_ref, w_ref in zip(xs[1:], ws[1:]):
        acc = acc + _dot(x_ref[...], w_ref[...])
    z = _layer_norm(alpha * resid_ref[...] + acc, g_ref[...], b_ref[...])
    of_ref[...] = z
    ob_ref[...] = z.astype(BF16)


def _out_ln(xs, ws, resid, g, b, alpha):
    m, d = resid.shape
    tm = _tile(m, (512, 256))
    n_in = len(xs)
    in_specs = [pl.BlockSpec((tm, x.shape[1]), lambda i: (i, 0)) for x in xs]
    in_specs += [pl.BlockSpec(w.shape, lambda i: (0, 0)) for w in ws]
    in_specs += [pl.BlockSpec((tm, d), lambda i: (i, 0)), pl.BlockSpec((1, d), lambda i: (0, 0)),
                 pl.BlockSpec((1, d), lambda i: (0, 0))]
    return pl.pallas_call(
        functools.partial(_out_ln_kernel, n_in=n_in, alpha=alpha),
        grid=(m // tm,),
        in_specs=in_specs,
        out_specs=[pl.BlockSpec((tm, d), lambda i: (i, 0)), pl.BlockSpec((tm, d), lambda i: (i, 0))],
        out_shape=[jax.ShapeDtypeStruct((m, d), F32), jax.ShapeDtypeStruct((m, d), BF16)],
        compiler_params=_cparams("parallel"),
        name="out_ln",
    )(*xs, *ws, resid, g.reshape(1, d), b.reshape(1, d))


def _swiglu_accumulate(x_ref, w1, w3, w2, acc_ref):
    w1 = w1.astype(BF16)
    w3 = w3.astype(BF16)
    w2 = w2.astype(BF16)
    half = x_ref.shape[0] // 2
    rows = (pl.ds(0, half), pl.ds(half, half))
    pre = [(_dot(x_ref[r, :], w1), _dot(x_ref[r, :], w3)) for r in rows]
    for r, (h1, h3) in zip(rows, pre):
        h = (h1 * _sigmoid(h1)) * h3
        acc_ref[r, :] += _dot(h.astype(BF16), w2)


def _ffn_ln_kernel(x_ref, w1_ref, w3_ref, w2_ref, resid_ref, g_ref, b_ref, of_ref, ob_ref, acc_ref, *, alpha):
    j = pl.program_id(1)

    @pl.when(j == 0)
    def _():
        acc_ref[...] = jnp.zeros_like(acc_ref)

    _swiglu_accumulate(x_ref, w1_ref[0], w3_ref[0], w2_ref[0], acc_ref)

    @pl.when(j == pl.num_programs(1) - 1)
    def _():
        z = _layer_norm(alpha * resid_ref[...] + acc_ref[...], g_ref[...], b_ref[...])
        of_ref[...] = z
        ob_ref[...] = z.astype(BF16)


def _ffn_ln(x_bf, resid, w1, w3, w2, layer, g, b, alpha):
    m, d = resid.shape
    ff = w1.shape[2]
    tm = _tile(m, (1024, 512, 256))
    tf = _tile(ff, (256, 128))
    return pl.pallas_call(
        functools.partial(_ffn_ln_kernel, alpha=alpha),
        grid=(m // tm, ff // tf),
        in_specs=[
            pl.BlockSpec((tm, d), lambda i, j: (i, 0)),
            pl.BlockSpec((1, d, tf), lambda i, j: (layer, 0, j)),
            pl.BlockSpec((1, d, tf), lambda i, j: (layer, 0, j)),
            pl.BlockSpec((1, tf, d), lambda i, j: (layer, j, 0)),
            pl.BlockSpec((tm, d), lambda i, j: (i, 0)),
            pl.BlockSpec((1, d), lambda i, j: (0, 0)),
            pl.BlockSpec((1, d), lambda i, j: (0, 0)),
        ],
        out_specs=[pl.BlockSpec((tm, d), lambda i, j: (i, 0)), pl.BlockSpec((tm, d), lambda i, j: (i, 0))],
        out_shape=[jax.ShapeDtypeStruct((m, d), F32), jax.ShapeDtypeStruct((m, d), BF16)],
        scratch_shapes=[pltpu.VMEM((tm, d), F32)],
        compiler_params=_cparams("parallel", "arbitrary"),
        name="ffn_ln",
    )(x_bf, w1, w3, w2, resid, g.reshape(1, d), b.reshape(1, d))


def _moe_ffn_kernel(te_ref, na_ref, x_ref, w1_ref, w3_ref, w2_ref, o_ref, acc_ref):
    i = pl.program_id(0)
    j = pl.program_id(1)
    active = i < na_ref[0]

    @pl.when(active & (j == 0))
    def _():
        acc_ref[...] = jnp.zeros_like(acc_ref)

    @pl.when(active)
    def _():
        _swiglu_accumulate(x_ref, w1_ref[0, 0], w3_ref[0, 0], w2_ref[0, 0], acc_ref)

    @pl.when(active & (j == pl.num_programs(1) - 1))
    def _():
        o_ref[...] = acc_ref[...].astype(o_ref.dtype)


def _moe_ffn(x_sorted, tile_expert, n_active, w1, w3, w2, layer, tm):
    p, d = x_sorted.shape
    ff = w1.shape[3]
    tf = _tile(ff, (256, 128))
    nf = ff // tf

    def row(i, na):
        return jnp.minimum(i, na[0] - 1)

    def col(i, j, na):
        return jnp.where(i < na[0], j, nf - 1)

    grid_spec = pltpu.PrefetchScalarGridSpec(
        num_scalar_prefetch=2,
        grid=(p // tm, nf),
        in_specs=[
            pl.BlockSpec((tm, d), lambda i, j, te, na: (row(i, na), 0)),
            pl.BlockSpec((1, 1, d, tf), lambda i, j, te, na: (layer, te[row(i, na)], 0, col(i, j, na))),
            pl.BlockSpec((1, 1, d, tf), lambda i, j, te, na: (layer, te[row(i, na)], 0, col(i, j, na))),
            pl.BlockSpec((1, 1, tf, d), lambda i, j, te, na: (layer, te[row(i, na)], col(i, j, na), 0)),
        ],
        out_specs=pl.BlockSpec((tm, d), lambda i, j, te, na: (row(i, na), 0)),
        scratch_shapes=[pltpu.VMEM((tm, d), F32)],
    )
    return pl.pallas_call(
        _moe_ffn_kernel,
        grid_spec=grid_spec,
        out_shape=jax.ShapeDtypeStruct((p, d), BF16),
        compiler_params=_cparams("arbitrary", "arbitrary"),
        name="moe_ffn",
    )(tile_expert, n_active, x_sorted, w1, w3, w2)


def _router_kernel(x_ref, r_ref, gate_ref, idx_ref):
    logits = jnp.dot(x_ref[...], r_ref[...], precision=lax.Precision.HIGHEST, preferred_element_type=F32)
    lane = lax.broadcasted_iota(jnp.int32, logits.shape, 1)
    neg = jnp.float32(-jnp.inf)
    logits = jnp.where(lane < N_EXPERTS, logits, neg)
    v1 = jnp.max(logits, axis=-1, keepdims=True)
    i1 = jnp.min(jnp.where(logits == v1, lane, LANES), axis=-1, keepdims=True)
    rest = jnp.where(lane == i1, neg, logits)
    v2 = jnp.max(rest, axis=-1, keepdims=True)
    i2 = jnp.min(jnp.where(rest == v2, lane, LANES), axis=-1, keepdims=True)
    e = jnp.exp(v2 - v1)
    g1 = 1.0 / (1.0 + e)
    g2 = e / (1.0 + e)
    gate_ref[...] = jnp.where(lane == 0, g1, jnp.where(lane == 1, g2, 0.0))
    idx_ref[...] = jnp.where(lane == 0, i1, jnp.where(lane == 1, i2, 0))


def _router(x_f32, router_padded):
    m, d = x_f32.shape
    tm = _tile(m, (1024, 512, 256))
    return pl.pallas_call(
        _router_kernel,
        grid=(m // tm,),
        in_specs=[pl.BlockSpec((tm, d), lambda i: (i, 0)), pl.BlockSpec((d, LANES), lambda i: (0, 0))],
        out_specs=[pl.BlockSpec((tm, LANES), lambda i: (i, 0)), pl.BlockSpec((tm, LANES), lambda i: (i, 0))],
        out_shape=[jax.ShapeDtypeStruct((m, LANES), F32), jax.ShapeDtypeStruct((m, LANES), jnp.int32)],
        compiler_params=_cparams("parallel"),
        name="router",
    )(x_f32, router_padded)


def _moe_combine_kernel(resid_ref, y1_ref, y2_ref, gate_ref, g_ref, b_ref, of_ref, ob_ref, *, alpha):
    gates = gate_ref[...]
    y = gates[:, 0:1] * y1_ref[...].astype(F32) + gates[:, 1:2] * y2_ref[...].astype(F32)
    z = _layer_norm(alpha * resid_ref[...] + y, g_ref[...], b_ref[...])
    of_ref[...] = z
    ob_ref[...] = z.astype(BF16)


def _moe_combine(resid, y1, y2, gates, g, b, alpha):
    m, d = resid.shape
    tm = _tile(m, (512, 256))
    row = lambda i: (i, 0)
    fixed = lambda i: (0, 0)
    return pl.pallas_call(
        functools.partial(_moe_combine_kernel, alpha=alpha),
        grid=(m // tm,),
        in_specs=[pl.BlockSpec((tm, d), row), pl.BlockSpec((tm, d), row), pl.BlockSpec((tm, d), row),
                  pl.BlockSpec((tm, LANES), row), pl.BlockSpec((1, d), fixed), pl.BlockSpec((1, d), fixed)],
        out_specs=[pl.BlockSpec((tm, d), row), pl.BlockSpec((tm, d), row)],
        out_shape=[jax.ShapeDtypeStruct((m, d), F32), jax.ShapeDtypeStruct((m, d), BF16)],
        compiler_params=_cparams("parallel"),
        name="moe_combine",
    )(resid, y1, y2, gates, g.reshape(1, d), b.reshape(1, d))


def _hgrn_masks(c, levels):
    t = np.arange(c)[:, None]
    s = np.arange(c)[None, :]
    tri = np.stack([(s <= t), (s >= t)]).astype(np.float32)
    fwd, bwd = [], []
    for l in range(levels):
        same = (t >> (l + 1)) == (s >> (l + 1))
        t_up = ((t >> l) & 1) == 1
        s_up = ((s >> l) & 1) == 1
        fwd.append(same & t_up & ~s_up)
        bwd.append(same & ~t_up & s_up)
    fwd.append(t == s)
    bwd.append(t == s)
    up = np.stack([np.broadcast_to(((t >> l) & 1) == 1, (c, A_HEAD_DIM)) for l in range(levels)])
    sign = np.stack([np.where(up, 1.0, -1.0), np.where(up, -1.0, 1.0)]).astype(np.float32)
    return tri, np.stack([np.stack(fwd), np.stack(bwd)]).astype(np.float32), sign


def _segment_reference(x, level, forward):
    c, w = x.shape
    half = 1 << level
    seg = 2 * half
    idx = half - 1 if forward else half
    if seg >= SUBLANES:
        xr = x.reshape(c // seg, seg, w)
        return jnp.broadcast_to(xr[:, idx:idx + 1, :], xr.shape).reshape(c, w)
    x3 = x.reshape(c // SUBLANES, SUBLANES, w)
    sub = lax.broadcasted_iota(jnp.int32, x3.shape, 1)
    r3 = jnp.broadcast_to(x3[:, idx:idx + 1, :], x3.shape)
    for j in range(1, SUBLANES // seg):
        row = j * seg + idx
        r3 = jnp.where(sub >= j * seg, jnp.broadcast_to(x3[:, row:row + 1, :], x3.shape), r3)
    return r3.reshape(c, w)


def _hgrn_chunk(q, k, v, log_f, state_t, tri, masks, signs, forward):
    c = q.shape[0]
    x = jnp.dot(tri, log_f, precision=lax.Precision.HIGHEST, preferred_element_type=F32)
    scores = masks[HGRN_LEVELS] * _dot_nt(q.astype(BF16), k.astype(BF16))
    for level in range(HGRN_LEVELS):
        ref = _segment_reference(x, level, forward)
        sign = signs[level]
        decay = jnp.exp(sign * (x - ref))
        z = (jnp.where(sign > 0, q, k) * decay).astype(BF16)
        scores = scores + masks[level] * _dot_nt(z, z)
    x_end = x[c - 1:c, :] if forward else x[0:1, :]
    q_dec = (q * jnp.exp(x)).astype(BF16)
    o = _dot(scores.astype(BF16), v.astype(BF16)) + _dot_nt(q_dec, state_t.astype(BF16))
    k_dec = (k * jnp.exp(x_end - x)).astype(BF16)
    new_state_t = state_t * jnp.exp(x_end) + _dot_tn(v.astype(BF16), k_dec)
    return o, new_state_t


def _hgrn_kernel(q_ref, v_ref, g_ref, zf_ref, zb_ref, lb_ref, nw_ref, tri_ref, msk_ref, sgn_ref, o_ref, acc_ref, *,
                 n_chunks):
    c = HGRN_CHUNK
    hd = A_HEAD_DIM

    def gates(z, lb):
        e = jnp.exp(-jnp.abs(z))
        inv = 1.0 / (1.0 + e)
        pos = z >= 0
        sig = jnp.where(pos, inv, e * inv)
        sig_neg = jnp.where(pos, e * inv, inv)
        f = lb + (1.0 - lb) * sig
        return jnp.log(jnp.maximum(f, MIN_FORGET)), (1.0 - lb) * sig_neg

    def load(ref, c0, h):
        return ref[0, pl.ds(c0, c), h * hd:(h + 1) * hd].astype(F32)

    def emit(c0, h, tot):
        ms = jnp.mean(tot * tot, axis=-1, keepdims=True)
        g = load(g_ref, c0, h)
        out = tot * lax.rsqrt(ms + RMS_EPS) * nw_ref[...] * (g * _sigmoid(g))
        o_ref[0, pl.ds(c0, c), h * hd:(h + 1) * hd] = out.astype(o_ref.dtype)

    def one_direction(h, c0, z_ref, state_t, direction):
        lb = lb_ref[:, h * hd:(h + 1) * hd]
        log_f, k = gates(load(z_ref, c0, h), lb)
        masks = [msk_ref[direction, l] for l in range(HGRN_LEVELS + 1)]
        signs = [sgn_ref[direction, l] for l in range(HGRN_LEVELS)]
        return _hgrn_chunk(load(q_ref, c0, h), k, load(v_ref, c0, h), log_f, state_t, tri_ref[direction], masks,
                           signs, direction == 0)

    def step(i, states, finalize):
        cf = pl.multiple_of(i * c, c)
        cb = pl.multiple_of((n_chunks - 1 - i) * c, c)
        new_states = []
        for h in range(HGRN_HEADS_PER_STEP):
            o_f, st_f = one_direction(h, cf, zf_ref, states[2 * h], 0)
            o_b, st_b = one_direction(h, cb, zb_ref, states[2 * h + 1], 1)
            cols = slice(h * hd, (h + 1) * hd)
            if finalize:
                emit(cf, h, acc_ref[pl.ds(cf, c), cols] + o_f)
                emit(cb, h, acc_ref[pl.ds(cb, c), cols] + o_b)
            else:
                acc_ref[pl.ds(cf, c), cols] = o_f
                acc_ref[pl.ds(cb, c), cols] = o_b
            new_states += [st_f, st_b]
        return tuple(new_states)

    zero = jnp.zeros((hd, hd), F32)
    half = n_chunks // 2
    states = lax.fori_loop(0, half, functools.partial(step, finalize=False), (zero,) * (2 * HGRN_HEADS_PER_STEP))
    lax.fori_loop(half, n_chunks, functools.partial(step, finalize=True), states)


def _hgrn(qvg, zz, lb, norm_w, n_heads):
    bsz, s, _ = qvg.shape
    c = HGRN_CHUNK
    assert s % (2 * c) == 0 and n_heads % HGRN_HEADS_PER_STEP == 0
    tri, masks, signs = _hgrn_masks(c, HGRN_LEVELS)
    hd = A_HEAD_DIM
    wide = HGRN_HEADS_PER_STEP * hd
    n_groups = n_heads // HGRN_HEADS_PER_STEP
    col = lambda off: pl.BlockSpec((1, s, wide), lambda b, h: (b, 0, off + h))
    const = lambda a: pl.BlockSpec(a.shape, lambda b, h: (0,) * a.ndim)
    return pl.pallas_call(
        functools.partial(_hgrn_kernel, n_chunks=s // c),
        grid=(bsz, n_groups),
        in_specs=[
            col(0), col(n_groups), col(2 * n_groups), col(0), col(n_groups),
            pl.BlockSpec((1, wide), lambda b, h: (0, h)),
            pl.BlockSpec((1, hd), lambda b, h: (0, 0)),
            const(tri), const(masks), const(signs),
        ],
        out_specs=pl.BlockSpec((1, s, wide), lambda b, h: (b, 0, h)),
        out_shape=jax.ShapeDtypeStruct((bsz, s, n_heads * hd), BF16),
        scratch_shapes=[pltpu.VMEM((s, wide), F32)],
        compiler_params=_cparams("parallel", "parallel"),
        name="hgrn2",
    )(qvg, qvg, qvg, zz, zz, lb.reshape(1, -1), norm_w.reshape(1, hd), jnp.asarray(tri), jnp.asarray(masks),
      jnp.asarray(signs))


def _dilated_kernel(q_ref, k_ref, v_ref, o_ref, qf_ref, kf_ref, vf_ref, oc_ref, lc_ref, *, seq):
    qf_ref[...] = q_ref[0].astype(F32)
    kf_ref[...] = k_ref[0].astype(F32)
    vf_ref[...] = v_ref[0].astype(F32)
    n_cfg = len(B_CONFIGS)
    for ci, (_, dil) in enumerate(B_CONFIGS):
        length = seq // dil
        tq = min(LANES, length)
        win = min(length, tq + 2 * BAND_RADIUS)
        head0 = lax.broadcasted_iota(jnp.int32, (tq, LANES), 1) < B_HEAD_DIM
        rel = lax.broadcasted_iota(jnp.int32, (tq, win), 1) - lax.broadcasted_iota(jnp.int32, (tq, win), 0)

        def rows(first, size, dil=dil):
            return pl.ds(first, size) if dil == 1 else pl.ds(first, size, stride=dil)

        def block(t, carry, ci=ci, dil=dil, length=length, tq=tq, win=win, head0=head0, rel=rel, rows=rows):
            res = t % dil
            q0 = (t // dil) * tq
            start = jnp.clip(q0 - BAND_RADIUS, 0, length - win)
            valid = jnp.abs(rel + (start - q0)) <= BAND_RADIUS
            q_rows = rows(q0 * dil + res, tq)
            k_rows = rows(start * dil + res, win)
            q = qf_ref[q_rows, :].astype(BF16)
            kw = kf_ref[k_rows, :].astype(BF16)
            vw = vf_ref[k_rows, :].astype(BF16)

            def one_head(mask):
                s = _dot_nt(jnp.where(mask, q, jnp.zeros_like(q)), kw)
                s = jnp.where(valid, s, MASK_VALUE)
                m = jnp.max(s, axis=-1, keepdims=True)
                p = jnp.exp2(s - m)
                l = jnp.sum(p, axis=-1, keepdims=True)
                return _dot(p.astype(BF16), vw) / l, m + jnp.log(l) * LOG2_E

            oa, la = one_head(head0)
            ob, lb = one_head(jnp.logical_not(head0))
            oc_ref[ci, q_rows, :] = jnp.where(head0, oa, ob)
            lc_ref[ci, q_rows, :] = jnp.where(head0, la, lb)
            return carry

        lax.fori_loop(0, dil * (length // tq), block, 0, unroll=2)

    tmix = min(256, seq)

    def mix(i, carry):
        r0 = pl.multiple_of(i * tmix, tmix)
        lses = [lc_ref[c, pl.ds(r0, tmix), :] for c in range(n_cfg)]
        top = lses[0]
        for l in lses[1:]:
            top = jnp.maximum(top, l)
        num = jnp.zeros((tmix, LANES), F32)
        den = jnp.zeros((tmix, LANES), F32)
        for c in range(n_cfg):
            w = jnp.exp2(lses[c] - top)
            num = num + w * oc_ref[c, pl.ds(r0, tmix), :]
            den = den + w
        o_ref[0, pl.ds(r0, tmix), :] = (num / den).astype(o_ref.dtype)
        return carry

    lax.fori_loop(0, seq // tmix, mix, 0)


def _dilated_attention(qk, v):
    bsz, s, w = v.shape
    n_pairs = w // LANES
    n_cfg = len(B_CONFIGS)
    return pl.pallas_call(
        functools.partial(_dilated_kernel, seq=s),
        grid=(bsz, n_pairs),
        in_specs=[
            pl.BlockSpec((1, s, LANES), lambda b, h: (b, 0, h)),
            pl.BlockSpec((1, s, LANES), lambda b, h: (b, 0, n_pairs + h)),
            pl.BlockSpec((1, s, LANES), lambda b, h: (b, 0, h)),
        ],
        out_specs=pl.BlockSpec((1, s, LANES), lambda b, h: (b, 0, h)),
        out_shape=jax.ShapeDtypeStruct((bsz, s, w), BF16),
        scratch_shapes=[pltpu.VMEM((s, LANES), F32)] * 3 + [pltpu.VMEM((n_cfg, s, LANES), F32)] * 2,
        compiler_params=_cparams("parallel", "parallel"),
        name="dilated_attn",
    )(qk, qk, v)


def _diff_kernel(q_ref, k_ref, v_ref, lam_ref, sub_ref, o_ref, s0_ref, s1_ref, p0_ref, p1_ref, *, tk, lambda_init):
    q = q_ref[0]
    tq = q.shape[0]
    s_len = k_ref.shape[1]
    lane = lax.broadcasted_iota(jnp.int32, q.shape, 1)
    zero = jnp.zeros_like(q)
    qs = (jnp.where(lane < C_HEAD_DIM, q, zero), jnp.where(lane >= C_HEAD_DIM, q, zero))
    s_refs = (s0_ref, s1_ref)
    p_refs = (p0_ref, p1_ref)
    chunks = [(c * tk, (c + 1) * tk) for c in range(s_len // tk)]
    tiles = [(t * LANES, (t + 1) * LANES) for t in range(tk // LANES)]

    def scores(h, lo, hi, m_run):
        s = _dot_nt(qs[h], k_ref[0, lo:hi, :])
        s_refs[h][:, lo:hi] = s
        for a, b in tiles:
            m_run = jnp.maximum(m_run, s[:, a:b])
        return m_run

    def probs(h, lo, hi, m, l_run):
        p = jnp.exp2(s_refs[h][:, lo:hi] - m)
        for a, b in tiles:
            l_run = l_run + p[:, a:b]
        p_refs[h][:, lo:hi] = p.astype(BF16)
        return l_run

    def weighted(h, lo, hi):
        return _dot(p_refs[h][:, lo:hi], v_ref[0, lo:hi, :])

    neg = jnp.full((tq, LANES), -jnp.inf, F32)
    zeros = jnp.zeros((tq, LANES), F32)
    m0 = neg
    for lo, hi in chunks:
        m0 = scores(0, lo, hi, m0)
    m0 = jnp.max(m0, axis=-1, keepdims=True)
    m1, l0 = neg, zeros
    for lo, hi in chunks:
        m1 = scores(1, lo, hi, m1)
        l0 = probs(0, lo, hi, m0, l0)
    m1 = jnp.max(m1, axis=-1, keepdims=True)
    l1, acc0 = zeros, zeros
    for lo, hi in chunks:
        l1 = probs(1, lo, hi, m1, l1)
        acc0 = acc0 + weighted(0, lo, hi)
    acc1 = zeros
    for lo, hi in chunks:
        acc1 = acc1 + weighted(1, lo, hi)
    outs = (acc0 / jnp.sum(l0, axis=-1, keepdims=True), acc1 / jnp.sum(l1, axis=-1, keepdims=True))
    lp = lam_ref[...]
    lam = (jnp.exp(jnp.sum(lp[0:1] * lp[1:2], axis=-1, keepdims=True))
           - jnp.exp(jnp.sum(lp[2:3] * lp[3:4], axis=-1, keepdims=True)) + lambda_init)
    o = outs[0] - lam * outs[1]
    ms_o = jnp.mean(o * o, axis=-1, keepdims=True)
    o_ref[0] = (o * lax.rsqrt(ms_o + RMS_EPS) * sub_ref[...] * (1.0 - lambda_init)).astype(o_ref.dtype)


def _diff_attention(qk, v, lam_params, subln_w, lambda_init):
    bsz, s, w = v.shape
    n_heads = w // LANES
    tq = _tile(s, (512, 256, 128))
    tk = _tile(s, (1024, 512, 256, 128))
    return pl.pallas_call(
        functools.partial(_diff_kernel, tk=tk, lambda_init=lambda_init),
        grid=(bsz, n_heads, s // tq),
        in_specs=[
            pl.BlockSpec((1, tq, LANES), lambda b, h, i: (b, i, h)),
            pl.BlockSpec((1, s, LANES), lambda b, h, i: (b, 0, n_heads + h)),
            pl.BlockSpec((1, s, LANES), lambda b, h, i: (b, 0, h)),
            pl.BlockSpec(lam_params.shape, lambda b, h, i: (0, 0)),
            pl.BlockSpec((1, LANES), lambda b, h, i: (0, 0)),
        ],
        out_specs=pl.BlockSpec((1, tq, LANES), lambda b, h, i: (b, i, h)),
        out_shape=jax.ShapeDtypeStruct((bsz, s, w), BF16),
        scratch_shapes=[pltpu.VMEM((tq, s), F32)] * 2 + [pltpu.VMEM((tq, s), BF16)] * 2,
        compiler_params=_cparams("parallel", "parallel", "arbitrary"),
        name="diff_attn",
    )(qk, qk, v, lam_params, subln_w.reshape(1, LANES))


def _rope_tables(seq, width):
    half = B_HEAD_DIM // 2
    inv = ROPE_THETA ** (-jnp.arange(0, B_HEAD_DIM, 2, dtype=F32) / B_HEAD_DIM)
    ang = jnp.arange(seq, dtype=F32)[:, None] * inv[None, :]
    cos, sin = jnp.cos(ang), jnp.sin(ang)
    reps = width // B_HEAD_DIM
    assert half * 2 == B_HEAD_DIM
    return jnp.tile(jnp.concatenate([cos, cos], axis=1), (1, reps)), jnp.tile(jnp.concatenate([-sin, sin], axis=1), (1, reps))


def _even_layer(x_f, x_b, bsz, seq, w_in, lb, norm_w, w_out, ln1, w1, w3, w2, layer, ln2, rope, alpha):
    d = x_f.shape[1]
    aw = d // 2
    n_heads_a = aw // A_HEAD_DIM
    w_in = w_in.astype(BF16)
    cols = lambda a, b: w_in[:, a * aw:b * aw]
    qvg = _proj(x_b, jnp.concatenate([cols(0, 1), cols(3, 5)], axis=1), BF16)
    zz = _proj(x_b, cols(1, 3), F32)
    cos_t, sin_t = rope
    scale_row = jnp.concatenate([jnp.full((1, aw), LOG2_E * B_HEAD_DIM ** -0.5, F32), jnp.ones((1, aw), F32)], axis=1)
    qk = _proj_rope(x_b, cols(5, 7), cos_t, sin_t, scale_row, seq)
    vb = _proj(x_b, cols(7, 8), BF16)
    oa = _hgrn(qvg.reshape(bsz, seq, -1), zz.reshape(bsz, seq, -1), lb, norm_w, n_heads_a)
    ob = _dilated_attention(qk.reshape(bsz, seq, 2 * aw), vb.reshape(bsz, seq, aw))
    w_out = w_out.astype(BF16)
    x_f, x_b = _out_ln([oa.reshape(bsz * seq, aw), ob.reshape(bsz * seq, aw)], [w_out[:aw], w_out[aw:]], x_f,
                       ln1[0], ln1[1], alpha)
    return _ffn_ln(x_b, x_f, w1, w3, w2, layer, ln2[0], ln2[1], alpha)


def _moe_dispatch(idx, n_tokens, tm):
    e_flat = idx[:, :2].reshape(-1)
    onehot = (e_flat[:, None] == jnp.arange(N_EXPERTS, dtype=jnp.int32)[None, :]).astype(jnp.int32)
    rank = jnp.sum(jnp.cumsum(onehot, axis=0) * onehot, axis=1) - 1
    counts = jnp.sum(onehot, axis=0)
    tiles = (counts + tm - 1) // tm
    tile_end = jnp.cumsum(tiles)
    group_start = (tile_end - tiles) * tm
    dest = group_start[e_flat] + rank
    n_tiles = (2 * n_tokens) // tm + N_EXPERTS
    src_tok = jnp.zeros((n_tiles * tm,), jnp.int32).at[dest].set(jnp.arange(2 * n_tokens, dtype=jnp.int32) // 2)
    tile_ids = jnp.arange(n_tiles, dtype=jnp.int32)
    tile_expert = jnp.minimum(jnp.sum((tile_ids[:, None] >= tile_end[None, :]).astype(jnp.int32), axis=1),
                              N_EXPERTS - 1)
    return src_tok, dest.reshape(n_tokens, 2), tile_expert, tile_end[-1:].astype(jnp.int32)


def _odd_layer(x_f, x_b, bsz, seq, w_in, lam_params, subln_w, w_out, ln1, router, w1, w3, w2, layer, ln2, rope,
               alpha, lambda_init):
    d = x_f.shape[1]
    n_tok = bsz * seq
    w_in = w_in.astype(BF16)
    cos_t, sin_t = rope
    scale_row = jnp.concatenate([jnp.full((1, d), LOG2_E * C_HEAD_DIM ** -0.5, F32), jnp.ones((1, d), F32)], axis=1)
    qk = _proj_rope(x_b, w_in[:, :2 * d], cos_t, sin_t, scale_row, seq)
    v = _proj(x_b, w_in[:, 2 * d:], BF16)
    o = _diff_attention(qk.reshape(bsz, seq, 2 * d), v.reshape(bsz, seq, d), lam_params.astype(F32), subln_w,
                        lambda_init)
    x_f, x_b = _out_ln([o.reshape(n_tok, d)], [w_out.astype(BF16)], x_f, ln1[0], ln1[1], alpha)
    router_padded = jnp.pad(router.astype(F32), ((0, 0), (0, LANES - N_EXPERTS)))
    gates, idx = _router(x_f, router_padded)
    tm = _tile(n_tok, (1024, 512, 256))
    src_tok, pos, tile_expert, n_active = _moe_dispatch(idx, n_tok, tm)
    y = _moe_ffn(jnp.take(x_b, src_tok, axis=0), tile_expert, n_active, w1, w3, w2, layer, tm)
    y1 = jnp.take(y, pos[:, 0], axis=0)
    y2 = jnp.take(y, pos[:, 1], axis=0)
    return _moe_combine(x_f, y1, y2, gates, ln2[0], ln2[1], alpha)


def kernel(x, ev_w_in, ev_lb_logits, ev_hgrn_norm, ev_w_out, ev_ln1_g, ev_ln1_b, ev_w1, ev_w3, ev_w2, ev_ln2_g,
           ev_ln2_b, od_w_in, od_lambda, od_subln, od_w_out, od_ln1_g, od_ln1_b, od_router, od_w1, od_w3, od_w2,
           od_ln2_g, od_ln2_b):
    bsz, seq, d = x.shape
    depth = ev_w_in.shape[0] + od_w_in.shape[0]
    alpha = (2 * depth) ** 0.25
    rope = _rope_tables(seq, 512)
    lb_soft = jax.nn.softmax(ev_lb_logits.astype(F32), axis=0)
    lower_bounds = jnp.cumsum(lb_soft, axis=0) - lb_soft[0]
    x_f = x.reshape(bsz * seq, d).astype(F32)
    x_b = x_f.astype(BF16)
    for layer in range(depth):
        j = layer // 2
        if layer % 2 == 0:
            x_f, x_b = _even_layer(x_f, x_b, bsz, seq, ev_w_in[j], lower_bounds[j], ev_hgrn_norm[j], ev_w_out[j],
                                   (ev_ln1_g[j], ev_ln1_b[j]), ev_w1, ev_w3, ev_w2, j,
                                   (ev_ln2_g[j], ev_ln2_b[j]), rope, alpha)
        else:
            lambda_init = 0.8 - 0.6 * math.exp(-0.3 * layer)
            x_f, x_b = _odd_layer(x_f, x_b, bsz, seq, od_w_in[j], od_lambda[j], od_subln[j], od_w_out[j],
                                  (od_ln1_g[j], od_ln1_b[j]), od_router[j], od_w1, od_w3, od_w2, j,
                                  (od_ln2_g[j], od_ln2_b[j]), rope, alpha, lambda_init)
    return x_f.reshape(bsz, seq, d).astype(x.dtype)
```

```python
import functools
import math

import numpy as np
import jax
import jax.numpy as jnp
from jax import lax
from jax.experimental import pallas as pl
from jax.experimental.pallas import tpu as pltpu

F32 = jnp.float32
BF16 = jnp.bfloat16

A_HEAD_DIM = 128
B_HEAD_DIM = 64
B_CONFIGS = ((128, 1), (512, 4), (2048, 16))
BAND_RADIUS = 64
C_HEAD_DIM = 64
N_EXPERTS = 8
ROPE_THETA = 10000.0
LN_EPS = 1e-5
RMS_EPS = 1e-5
MASK_VALUE = -1e30
MIN_FORGET = 1e-30
LOG2_E = math.log2(math.e)

LANES = 128
SUBLANES = 8
VMEM_LIMIT_BYTES = 56 * 1024 * 1024

HGRN_CHUNK = 128
HGRN_LEVELS = 7
HGRN_HEADS_PER_STEP = 2


def _cparams(*sem):
    return pltpu.CompilerParams(dimension_semantics=sem, vmem_limit_bytes=VMEM_LIMIT_BYTES)


def _tile(n, prefs):
    for p in prefs:
        if n % p == 0:
            return p
    return n


def _dot(a, b):
    return jnp.dot(a, b, preferred_element_type=F32)


def _dot_nt(a, b):
    return lax.dot_general(a, b, (((1,), (1,)), ((), ())), preferred_element_type=F32)


def _dot_tn(a, b):
    return lax.dot_general(a, b, (((0,), (0,)), ((), ())), preferred_element_type=F32)


def _sigmoid(x):
    return 1.0 / (1.0 + jnp.exp(-x))


def _layer_norm(y, g, b):
    mu = jnp.mean(y, axis=-1, keepdims=True)
    d = y - mu
    var = jnp.mean(d * d, axis=-1, keepdims=True)
    return d * lax.rsqrt(var + LN_EPS) * g + b


def _proj_kernel(x_ref, w_ref, o_ref):
    o_ref[...] = _dot(x_ref[...], w_ref[...]).astype(o_ref.dtype)


def _proj(x, w, out_dtype):
    m, k = x.shape
    n = w.shape[1]
    tm = _tile(m, (1024, 512, 256))
    tn = n if n <= 1536 else _tile(n, (1024, 512, 256, 128))
    return pl.pallas_call(
        _proj_kernel,
        grid=(m // tm, n // tn),
        in_specs=[pl.BlockSpec((tm, k), lambda i, j: (i, 0)), pl.BlockSpec((k, tn), lambda i, j: (0, j))],
        out_specs=pl.BlockSpec((tm, tn), lambda i, j: (i, j)),
        out_shape=jax.ShapeDtypeStruct((m, n), out_dtype),
        compiler_params=_cparams("parallel", "arbitrary"),
        name="proj",
    )(x, w)


def _proj_rope_kernel(x_ref, w_ref, cos_ref, sin_ref, scale_ref, o_ref):
    acc = _dot(x_ref[...], w_ref[...])
    tn = acc.shape[1]
    lane = lax.broadcasted_iota(jnp.int32, acc.shape, 1)
    first_half = (lane % B_HEAD_DIM) < (B_HEAD_DIM // 2)
    partner = jnp.where(first_half, pltpu.roll(acc, tn - B_HEAD_DIM // 2, 1), pltpu.roll(acc, B_HEAD_DIM // 2, 1))
    o_ref[...] = ((acc * cos_ref[...] + partner * sin_ref[...]) * scale_ref[...]).astype(o_ref.dtype)


def _proj_rope(x, w, cos_t, sin_t, scale_row, seq):
    m, k = x.shape
    n = w.shape[1]
    tm = _tile(seq, (1024, 512, 256))
    tn = cos_t.shape[1]
    nsb = seq // tm
    return pl.pallas_call(
        _proj_rope_kernel,
        grid=(m // tm, n // tn),
        in_specs=[
            pl.BlockSpec((tm, k), lambda i, j: (i, 0)),
            pl.BlockSpec((k, tn), lambda i, j: (0, j)),
            pl.BlockSpec((tm, tn), lambda i, j: (i % nsb, 0)),
            pl.BlockSpec((tm, tn), lambda i, j: (i % nsb, 0)),
            pl.BlockSpec((1, tn), lambda i, j: (0, j)),
        ],
        out_specs=pl.BlockSpec((tm, tn), lambda i, j: (i, j)),
        out_shape=jax.ShapeDtypeStruct((m, n), BF16),
        compiler_params=_cparams("parallel", "arbitrary"),
        name="proj_rope",
    )(x, w, cos_t, sin_t, scale_row)


def _out_ln_kernel(*refs, n_in, alpha):
    xs = refs[:n_in]
    ws = refs[n_in:2 * n_in]
    resid_ref, g_ref, b_ref, of_ref, ob_ref = refs[2 * n_in:]
    acc = _dot(xs[0][...], ws[0][...])
    for x_ref, w_ref in zip(xs[1:], ws[1:]):
        acc = acc + _dot(x_ref[...], w_ref[...])
    z = _layer_norm(alpha * resid_ref[...] + acc, g_ref[...], b_ref[...])
    of_ref[...] = z
    ob_ref[...] = z.astype(BF16)


def _out_ln(xs, ws, resid, g, b, alpha):
    m, d = resid.shape
    tm = _tile(m, (512, 256))
    n_in = len(xs)
    in_specs = [pl.BlockSpec((tm, x.shape[1]), lambda i: (i, 0)) for x in xs]
    in_specs += [pl.BlockSpec(w.shape, lambda i: (0, 0)) for w in ws]
    in_specs += [pl.BlockSpec((tm, d), lambda i: (i, 0)), pl.BlockSpec((1, d), lambda i: (0, 0)),
                 pl.BlockSpec((1, d), lambda i: (0, 0))]
    return pl.pallas_call(
        functools.partial(_out_ln_kernel, n_in=n_in, alpha=alpha),
        grid=(m // tm,),
        in_specs=in_specs,
        out_specs=[pl.BlockSpec((tm, d), lambda i: (i, 0)), pl.BlockSpec((tm, d), lambda i: (i, 0))],
        out_shape=[jax.ShapeDtypeStruct((m, d), F32), jax.ShapeDtypeStruct((m, d), BF16)],
        compiler_params=_cparams("parallel"),
        name="out_ln",
    )(*xs, *ws, resid, g.reshape(1, d), b.reshape(1, d))


def _swiglu_accumulate(x_ref, w1, w3, w2, acc_ref):
    w1 = w1.astype(BF16)
    w3 = w3.astype(BF16)
    w2 = w2.astype(BF16)
    half = x_ref.shape[0] // 2
    rows = (pl.ds(0, half), pl.ds(half, half))
    pre = [(_dot(x_ref[r, :], w1), _dot(x_ref[r, :], w3)) for r in rows]
    for r, (h1, h3) in zip(rows, pre):
        h = (h1 * _sigmoid(h1)) * h3
        acc_ref[r, :] += _dot(h.astype(BF16), w2)


def _ffn_ln_kernel(x_ref, w1_ref, w3_ref, w2_ref, resid_ref, g_ref, b_ref, of_ref, ob_ref, acc_ref, *, alpha):
    j = pl.program_id(1)

    @pl.when(j == 0)
    def _():
        acc_ref[...] = jnp.zeros_like(acc_ref)

    _swiglu_accumulate(x_ref, w1_ref[0], w3_ref[0], w2_ref[0], acc_ref)

    @pl.when(j == pl.num_programs(1) - 1)
    def _():
        z = _layer_norm(alpha * resid_ref[...] + acc_ref[...], g_ref[...], b_ref[...])
        of_ref[...] = z
        ob_ref[...] = z.astype(BF16)


def _ffn_ln(x_bf, resid, w1, w3, w2, layer, g, b, alpha):
    m, d = resid.shape
    ff = w1.shape[2]
    tm = _tile(m, (1024, 512, 256))
    tf = _tile(ff, (256, 128))
    return pl.pallas_call(
        functools.partial(_ffn_ln_kernel, alpha=alpha),
        grid=(m // tm, ff // tf),
        in_specs=[
            pl.BlockSpec((tm, d), lambda i, j: (i, 0)),
            pl.BlockSpec((1, d, tf), lambda i, j: (layer, 0, j)),
            pl.BlockSpec((1, d, tf), lambda i, j: (layer, 0, j)),
            pl.BlockSpec((1, tf, d), lambda i, j: (layer, j, 0)),
            pl.BlockSpec((tm, d), lambda i, j: (i, 0)),
            pl.BlockSpec((1, d), lambda i, j: (0, 0)),
            pl.BlockSpec((1, d), lambda i, j: (0, 0)),
        ],
        out_specs=[pl.BlockSpec((tm, d), lambda i, j: (i, 0)), pl.BlockSpec((tm, d), lambda i, j: (i, 0))],
        out_shape=[jax.ShapeDtypeStruct((m, d), F32), jax.ShapeDtypeStruct((m, d), BF16)],
        scratch_shapes=[pltpu.VMEM((tm, d), F32)],
        compiler_params=_cparams("parallel", "arbitrary"),
        name="ffn_ln",
    )(x_bf, w1, w3, w2, resid, g.reshape(1, d), b.reshape(1, d))


def _moe_ffn_kernel(te_ref, na_ref, x_ref, w1_ref, w3_ref, w2_ref, o_ref, acc_ref):
    i = pl.program_id(0)
    j = pl.program_id(1)
    active = i < na_ref[0]

    @pl.when(active & (j == 0))
    def _():
        acc_ref[...] = jnp.zeros_like(acc_ref)

    @pl.when(active)
    def _():
        _swiglu_accumulate(x_ref, w1_ref[0, 0], w3_ref[0, 0], w2_ref[0, 0], acc_ref)

    @pl.when(active & (j == pl.num_programs(1) - 1))
    def _():
        o_ref[...] = acc_ref[...].astype(o_ref.dtype)


def _moe_ffn(x_sorted, tile_expert, n_active, w1, w3, w2, layer, tm):
    p, d = x_sorted.shape
    ff = w1.shape[3]
    tf = _tile(ff, (256, 128))
    nf = ff // tf

    def row(i, na):
        return jnp.minimum(i, na[0] - 1)

    def col(i, j, na):
        return jnp.where(i < na[0], j, nf - 1)

    grid_spec = pltpu.PrefetchScalarGridSpec(
        num_scalar_prefetch=2,
        grid=(p // tm, nf),
        in_specs=[
            pl.BlockSpec((tm, d), lambda i, j, te, na: (row(i, na), 0)),
            pl.BlockSpec((1, 1, d, tf), lambda i, j, te, na: (layer, te[row(i, na)], 0, col(i, j, na))),
            pl.BlockSpec((1, 1, d, tf), lambda i, j, te, na: (layer, te[row(i, na)], 0, col(i, j, na))),
            pl.BlockSpec((1, 1, tf, d), lambda i, j, te, na: (layer, te[row(i, na)], col(i, j, na), 0)),
        ],
        out_specs=pl.BlockSpec((tm, d), lambda i, j, te, na: (row(i, na), 0)),
        scratch_shapes=[pltpu.VMEM((tm, d), F32)],
    )
    return pl.pallas_call(
        _moe_ffn_kernel,
        grid_spec=grid_spec,
        out_shape=jax.ShapeDtypeStruct((p, d), BF16),
        compiler_params=_cparams("arbitrary", "arbitrary"),
        name="moe_ffn",
    )(tile_expert, n_active, x_sorted, w1, w3, w2)


def _router_kernel(x_ref, r_ref, gate_ref, idx_ref):
    logits = jnp.dot(x_ref[...], r_ref[...], precision=lax.Precision.HIGHEST, preferred_element_type=F32)
    lane = lax.broadcasted_iota(jnp.int32, logits.shape, 1)
    neg = jnp.float32(-jnp.inf)
    logits = jnp.where(lane < N_EXPERTS, logits, neg)
    v1 = jnp.max(logits, axis=-1, keepdims=True)
    i1 = jnp.min(jnp.where(logits == v1, lane, LANES), axis=-1, keepdims=True)
    rest = jnp.where(lane == i1, neg, logits)
    v2 = jnp.max(rest, axis=-1, keepdims=True)
    i2 = jnp.min(jnp.where(rest == v2, lane, LANES), axis=-1, keepdims=True)
    e = jnp.exp(v2 - v1)
    g1 = 1.0 / (1.0 + e)
    g2 = e / (1.0 + e)
    gate_ref[...] = jnp.where(lane == 0, g1, jnp.where(lane == 1, g2, 0.0))
    idx_ref[...] = jnp.where(lane == 0, i1, jnp.where(lane == 1, i2, 0))


def _router(x_f32, router_padded):
    m, d = x_f32.shape
    tm = _tile(m, (1024, 512, 256))
    return pl.pallas_call(
        _router_kernel,
        grid=(m // tm,),
        in_specs=[pl.BlockSpec((tm, d), lambda i: (i, 0)), pl.BlockSpec((d, LANES), lambda i: (0, 0))],
        out_specs=[pl.BlockSpec((tm, LANES), lambda i: (i, 0)), pl.BlockSpec((tm, LANES), lambda i: (i, 0))],
        out_shape=[jax.ShapeDtypeStruct((m, LANES), F32), jax.ShapeDtypeStruct((m, LANES), jnp.int32)],
        compiler_params=_cparams("parallel"),
        name="router",
    )(x_f32, router_padded)


def _moe_combine_kernel(resid_ref, y1_ref, y2_ref, gate_ref, g_ref, b_ref, of_ref, ob_ref, *, alpha):
    gates = gate_ref[...]
    y = gates[:, 0:1] * y1_ref[...].astype(F32) + gates[:, 1:2] * y2_ref[...].astype(F32)
    z = _layer_norm(alpha * resid_ref[...] + y, g_ref[...], b_ref[...])
    of_ref[...] = z
    ob_ref[...] = z.astype(BF16)


def _moe_combine(resid, y1, y2, gates, g, b, alpha):
    m, d = resid.shape
    tm = _tile(m, (512, 256))
    row = lambda i: (i, 0)
    fixed = lambda i: (0, 0)
    return pl.pallas_call(
        functools.partial(_moe_combine_kernel, alpha=alpha),
        grid=(m // tm,),
        in_specs=[pl.BlockSpec((tm, d), row), pl.BlockSpec((tm, d), row), pl.BlockSpec((tm, d), row),
                  pl.BlockSpec((tm, LANES), row), pl.BlockSpec((1, d), fixed), pl.BlockSpec((1, d), fixed)],
        out_specs=[pl.BlockSpec((tm, d), row), pl.BlockSpec((tm, d), row)],
        out_shape=[jax.ShapeDtypeStruct((m, d), F32), jax.ShapeDtypeStruct((m, d), BF16)],
        compiler_params=_cparams("parallel"),
        name="moe_combine",
    )(resid, y1, y2, gates, g.reshape(1, d), b.reshape(1, d))


def _hgrn_masks(c, levels):
    t = np.arange(c)[:, None]
    s = np.arange(c)[None, :]
    tri = np.stack([(s <= t), (s >= t)]).astype(np.float32)
    fwd, bwd = [], []
    for l in range(levels):
        same = (t >> (l + 1)) == (s >> (l + 1))
        t_up = ((t >> l) & 1) == 1
        s_up = ((s >> l) & 1) == 1
        fwd.append(same & t_up & ~s_up)
        bwd.append(same & ~t_up & s_up)
    fwd.append(t == s)
    bwd.append(t == s)
    up = np.stack([np.broadcast_to(((t >> l) & 1) == 1, (c, A_HEAD_DIM)) for l in range(levels)])
    sign = np.stack([np.where(up, 1.0, -1.0), np.where(up, -1.0, 1.0)]).astype(np.float32)
    return tri, np.stack([np.stack(fwd), np.stack(bwd)]).astype(np.float32), sign


def _segment_reference(x, level, forward):
    c, w = x.shape
    half = 1 << level
    seg = 2 * half
    idx = half - 1 if forward else half
    if seg >= SUBLANES:
        xr = x.reshape(c // seg, seg, w)
        return jnp.broadcast_to(xr[:, idx:idx + 1, :], xr.shape).reshape(c, w)
    x3 = x.reshape(c // SUBLANES, SUBLANES, w)
    sub = lax.broadcasted_iota(jnp.int32, x3.shape, 1)
    r3 = jnp.broadcast_to(x3[:, idx:idx + 1, :], x3.shape)
    for j in range(1, SUBLANES // seg):
        row = j * seg + idx
        r3 = jnp.where(sub >= j * seg, jnp.broadcast_to(x3[:, row:row + 1, :], x3.shape), r3)
    return r3.reshape(c, w)


def _hgrn_chunk(q, k, v, log_f, state_t, tri, masks, signs, forward):
    c = q.shape[0]
    x = jnp.dot(tri, log_f, precision=lax.Precision.HIGHEST, preferred_element_type=F32)
    scores = masks[HGRN_LEVELS] * _dot_nt(q.astype(BF16), k.astype(BF16))
    for level in range(HGRN_LEVELS):
        ref = _segment_reference(x, level, forward)
        sign = signs[level]
        decay = jnp.exp(sign * (x - ref))
        z = (jnp.where(sign > 0, q, k) * decay).astype(BF16)
        scores = scores + masks[level] * _dot_nt(z, z)
    x_end = x[c - 1:c, :] if forward else x[0:1, :]
    q_dec = (q * jnp.exp(x)).astype(BF16)
    o = _dot(scores.astype(BF16), v.astype(BF16)) + _dot_nt(q_dec, state_t.astype(BF16))
    k_dec = (k * jnp.exp(x_end - x)).astype(BF16)
    new_state_t = state_t * jnp.exp(x_end) + _dot_tn(v.astype(BF16), k_dec)
    return o, new_state_t


def _hgrn_kernel(q_ref, v_ref, g_ref, zf_ref, zb_ref, lb_ref, nw_ref, tri_ref, msk_ref, sgn_ref, o_ref, acc_ref, *,
                 n_chunks):
    c = HGRN_CHUNK
    hd = A_HEAD_DIM

    def gates(z, lb):
        e = jnp.exp(-jnp.abs(z))
        inv = 1.0 / (1.0 + e)
        pos = z >= 0
        sig = jnp.where(pos, inv, e * inv)
        sig_neg = jnp.where(pos, e * inv, inv)
        f = lb + (1.0 - lb) * sig
        return jnp.log(jnp.maximum(f, MIN_FORGET)), (1.0 - lb) * sig_neg

    def load(ref, c0, h):
        return ref[0, pl.ds(c0, c), h * hd:(h + 1) * hd].astype(F32)

    def emit(c0, h, tot):
        ms = jnp.mean(tot * tot, axis=-1, keepdims=True)
        g = load(g_ref, c0, h)
        out = tot * lax.rsqrt(ms + RMS_EPS) * nw_ref[...] * (g * _sigmoid(g))
        o_ref[0, pl.ds(c0, c), h * hd:(h + 1) * hd] = out.astype(o_ref.dtype)

    def one_direction(h, c0, z_ref, state_t, direction):
        lb = lb_ref[:, h * hd:(h + 1) * hd]
        log_f, k = gates(load(z_ref, c0, h), lb)
        masks = [msk_ref[direction, l] for l in range(HGRN_LEVELS + 1)]
        signs = [sgn_ref[direction, l] for l in range(HGRN_LEVELS)]
        return _hgrn_chunk(load(q_ref, c0, h), k, load(v_ref, c0, h), log_f, state_t, tri_ref[direction], masks,
                           signs, direction == 0)

    def step(i, states, finalize):
        cf = pl.multiple_of(i * c, c)
        cb = pl.multiple_of((n_chunks - 1 - i) * c, c)
        new_states = []
        for h in range(HGRN_HEADS_PER_STEP):
            o_f, st_f = one_direction(h, cf, zf_ref, states[2 * h], 0)
            o_b, st_b = one_direction(h, cb, zb_ref, states[2 * h + 1], 1)
            cols = slice(h * hd, (h + 1) * hd)
            if finalize:
                emit(cf, h, acc_ref[pl.ds(cf, c), cols] + o_f)
                emit(cb, h, acc_ref[pl.ds(cb, c), cols] + o_b)
            else:
                acc_ref[pl.ds(cf, c), cols] = o_f
                acc_ref[pl.ds(cb, c), cols] = o_b
            new_states += [st_f, st_b]
        return tuple(new_states)

    zero = jnp.zeros((hd, hd), F32)
    half = n_chunks // 2
    states = lax.fori_loop(0, half, functools.partial(step, finalize=False), (zero,) * (2 * HGRN_HEADS_PER_STEP))
    lax.fori_loop(half, n_chunks, functools.partial(step, finalize=True), states)


def _hgrn(qvg, zz, lb, norm_w, n_heads):
    bsz, s, _ = qvg.shape
    c = HGRN_CHUNK
    assert s % (2 * c) == 0 and n_heads % HGRN_HEADS_PER_STEP == 0
    tri, masks, signs = _hgrn_masks(c, HGRN_LEVELS)
    hd = A_HEAD_DIM
    wide = HGRN_HEADS_PER_STEP * hd
    n_groups = n_heads // HGRN_HEADS_PER_STEP
    col = lambda off: pl.BlockSpec((1, s, wide), lambda b, h: (b, 0, off + h))
    const = lambda a: pl.BlockSpec(a.shape, lambda b, h: (0,) * a.ndim)
    return pl.pallas_call(
        functools.partial(_hgrn_kernel, n_chunks=s // c),
        grid=(bsz, n_groups),
        in_specs=[
            col(0), col(n_groups), col(2 * n_groups), col(0), col(n_groups),
            pl.BlockSpec((1, wide), lambda b, h: (0, h)),
            pl.BlockSpec((1, hd), lambda b, h: (0, 0)),
            const(tri), const(masks), const(signs),
        ],
        out_specs=pl.BlockSpec((1, s, wide), lambda b, h: (b, 0, h)),
        out_shape=jax.ShapeDtypeStruct((bsz, s, n_heads * hd), BF16),
        scratch_shapes=[pltpu.VMEM((s, wide), F32)],
        compiler_params=_cparams("parallel", "parallel"),
        name="hgrn2",
    )(qvg, qvg, qvg, zz, zz, lb.reshape(1, -1), norm_w.reshape(1, hd), jnp.asarray(tri), jnp.asarray(masks),
      jnp.asarray(signs))


def _dilated_kernel(q_ref, k_ref, v_ref, o_ref, qf_ref, kf_ref, vf_ref, oc_ref, lc_ref, *, seq):
    qf_ref[...] = q_ref[0].astype(F32)
    kf_ref[...] = k_ref[0].astype(F32)
    vf_ref[...] = v_ref[0].astype(F32)
    n_cfg = len(B_CONFIGS)
    for ci, (_, dil) in enumerate(B_CONFIGS):
        length = seq // dil
        tq = min(LANES, length)
        win = min(length, tq + 2 * BAND_RADIUS)
        head0 = lax.broadcasted_iota(jnp.int32, (tq, LANES), 1) < B_HEAD_DIM
        rel = lax.broadcasted_iota(jnp.int32, (tq, win), 1) - lax.broadcasted_iota(jnp.int32, (tq, win), 0)

        def rows(first, size, dil=dil):
            return pl.ds(first, size) if dil == 1 else pl.ds(first, size, stride=dil)

        def block(t, carry, ci=ci, dil=dil, length=length, tq=tq, win=win, head0=head0, rel=rel, rows=rows):
            res = t % dil
            q0 = (t // dil) * tq
            start = jnp.clip(q0 - BAND_RADIUS, 0, length - win)
            valid = jnp.abs(rel + (start - q0)) <= BAND_RADIUS
            q_rows = rows(q0 * dil + res, tq)
            k_rows = rows(start * dil + res, win)
            q = qf_ref[q_rows, :].astype(BF16)
            kw = kf_ref[k_rows, :].astype(BF16)
            vw = vf_ref[k_rows, :].astype(BF16)

            def one_head(mask):
                s = _dot_nt(jnp.where(mask, q, jnp.zeros_like(q)), kw)
                s = jnp.where(valid, s, MASK_VALUE)
                m = jnp.max(s, axis=-1, keepdims=True)
                p = jnp.exp2(s - m)
                l = jnp.sum(p, axis=-1, keepdims=True)
                return _dot(p.astype(BF16), vw) / l, m + jnp.log(l) * LOG2_E

            oa, la = one_head(head0)
            ob, lb = one_head(jnp.logical_not(head0))
            oc_ref[ci, q_rows, :] = jnp.where(head0, oa, ob)
            lc_ref[ci, q_rows, :] = jnp.where(head0, la, lb)
            return carry

        lax.fori_loop(0, dil * (length // tq), block, 0, unroll=2)

    tmix = min(256, seq)

    def mix(i, carry):
        r0 = pl.multiple_of(i * tmix, tmix)
        lses = [lc_ref[c, pl.ds(r0, tmix), :] for c in range(n_cfg)]
        top = lses[0]
        for l in lses[1:]:
            top = jnp.maximum(top, l)
        num = jnp.zeros((tmix, LANES), F32)
        den = jnp.zeros((tmix, LANES), F32)
        for c in range(n_cfg):
            w = jnp.exp2(lses[c] - top)
            num = num + w * oc_ref[c, pl.ds(r0, tmix), :]
            den = den + w
        o_ref[0, pl.ds(r0, tmix), :] = (num / den).astype(o_ref.dtype)
        return carry

    lax.fori_loop(0, seq // tmix, mix, 0)


def _dilated_attention(qk, v):
    bsz, s, w = v.shape
    n_pairs = w // LANES
    n_cfg = len(B_CONFIGS)
    return pl.pallas_call(
        functools.partial(_dilated_kernel, seq=s),
        grid=(bsz, n_pairs),
        in_specs=[
            pl.BlockSpec((1, s, LANES), lambda b, h: (b, 0, h)),
            pl.BlockSpec((1, s, LANES), lambda b, h: (b, 0, n_pairs + h)),
            pl.BlockSpec((1, s, LANES), lambda b, h: (b, 0, h)),
        ],
        out_specs=pl.BlockSpec((1, s, LANES), lambda b, h: (b, 0, h)),
        out_shape=jax.ShapeDtypeStruct((bsz, s, w), BF16),
        scratch_shapes=[pltpu.VMEM((s, LANES), F32)] * 3 + [pltpu.VMEM((n_cfg, s, LANES), F32)] * 2,
        compiler_params=_cparams("parallel", "parallel"),
        name="dilated_attn",
    )(qk, qk, v)


def _diff_kernel(q_ref, k_ref, v_ref, lam_ref, sub_ref, o_ref, *, tk, lambda_init):
    q = q_ref[0]
    tq = q.shape[0]
    s_len = k_ref.shape[1]
    lane = lax.broadcasted_iota(jnp.int32, q.shape, 1)
    zero = jnp.zeros_like(q)
    qs = (jnp.where(lane < C_HEAD_DIM, q, zero), jnp.where(lane >= C_HEAD_DIM, q, zero))
    tiles = [(t * LANES, (t + 1) * LANES) for t in range(tk // LANES)]
    m = [jnp.full((tq, 1), -jnp.inf, F32)] * 2
    l = [jnp.zeros((tq, 1), F32)] * 2
    acc = [jnp.zeros((tq, LANES), F32)] * 2
    for c in range(s_len // tk):
        lo, hi = c * tk, (c + 1) * tk
        for h in range(2):
            s = _dot_nt(qs[h], k_ref[0, lo:hi, :])
            m_tile = s[:, 0:LANES]
            for a, b in tiles[1:]:
                m_tile = jnp.maximum(m_tile, s[:, a:b])
            m_new = jnp.maximum(m[h], jnp.max(m_tile, axis=-1, keepdims=True))
            alpha = jnp.exp2(m[h] - m_new)
            p = jnp.exp2(s - m_new)
            l_tile = p[:, 0:LANES]
            for a, b in tiles[1:]:
                l_tile = l_tile + p[:, a:b]
            l[h] = alpha * l[h] + jnp.sum(l_tile, axis=-1, keepdims=True)
            acc[h] = alpha * acc[h] + _dot(p.astype(BF16), v_ref[0, lo:hi, :])
            m[h] = m_new
    outs = (acc[0] / l[0], acc[1] / l[1])
    lp = lam_ref[...]
    lam = (jnp.exp(jnp.sum(lp[0:1] * lp[1:2], axis=-1, keepdims=True))
           - jnp.exp(jnp.sum(lp[2:3] * lp[3:4], axis=-1, keepdims=True)) + lambda_init)
    o = outs[0] - lam * outs[1]
    ms_o = jnp.mean(o * o, axis=-1, keepdims=True)
    o_ref[0] = (o * lax.rsqrt(ms_o + RMS_EPS) * sub_ref[...] * (1.0 - lambda_init)).astype(o_ref.dtype)


def _diff_attention(qk, v, lam_params, subln_w, lambda_init):
    bsz, s, w = v.shape
    n_heads = w // LANES
    tq = _tile(s, (512, 256, 128))
    tk = _tile(s, (2048, 1024, 512, 256, 128))
    return pl.pallas_call(
        functools.partial(_diff_kernel, tk=tk, lambda_init=lambda_init),
        grid=(bsz, n_heads, s // tq),
        in_specs=[
            pl.BlockSpec((1, tq, LANES), lambda b, h, i: (b, i, h)),
            pl.BlockSpec((1, s, LANES), lambda b, h, i: (b, 0, n_heads + h)),
            pl.BlockSpec((1, s, LANES), lambda b, h, i: (b, 0, h)),
            pl.BlockSpec(lam_params.shape, lambda b, h, i: (0, 0)),
            pl.BlockSpec((1, LANES), lambda b, h, i: (0, 0)),
        ],
        out_specs=pl.BlockSpec((1, tq, LANES), lambda b, h, i: (b, i, h)),
        out_shape=jax.ShapeDtypeStruct((bsz, s, w), BF16),
        compiler_params=_cparams("parallel", "parallel", "arbitrary"),
        name="diff_attn",
    )(qk, qk, v, lam_params, subln_w.reshape(1, LANES))


def _rope_tables(seq, width):
    half = B_HEAD_DIM // 2
    inv = ROPE_THETA ** (-jnp.arange(0, B_HEAD_DIM, 2, dtype=F32) / B_HEAD_DIM)
    ang = jnp.arange(seq, dtype=F32)[:, None] * inv[None, :]
    cos, sin = jnp.cos(ang), jnp.sin(ang)
    reps = width // B_HEAD_DIM
    assert half * 2 == B_HEAD_DIM
    return jnp.tile(jnp.concatenate([cos, cos], axis=1), (1, reps)), jnp.tile(jnp.concatenate([-sin, sin], axis=1), (1, reps))


def _even_layer(x_f, x_b, bsz, seq, w_in, lb, norm_w, w_out, ln1, w1, w3, w2, layer, ln2, rope, alpha):
    d = x_f.shape[1]
    aw = d // 2
    n_heads_a = aw // A_HEAD_DIM
    w_in = w_in.astype(BF16)
    cols = lambda a, b: w_in[:, a * aw:b * aw]
    qvg = _proj(x_b, jnp.concatenate([cols(0, 1), cols(3, 5)], axis=1), BF16)
    zz = _proj(x_b, cols(1, 3), F32)
    cos_t, sin_t = rope
    scale_row = jnp.concatenate([jnp.full((1, aw), LOG2_E * B_HEAD_DIM ** -0.5, F32), jnp.ones((1, aw), F32)], axis=1)
    qk = _proj_rope(x_b, cols(5, 7), cos_t, sin_t, scale_row, seq)
    vb = _proj(x_b, cols(7, 8), BF16)
    oa = _hgrn(qvg.reshape(bsz, seq, -1), zz.reshape(bsz, seq, -1), lb, norm_w, n_heads_a)
    ob = _dilated_attention(qk.reshape(bsz, seq, 2 * aw), vb.reshape(bsz, seq, aw))
    w_out = w_out.astype(BF16)
    x_f, x_b = _out_ln([oa.reshape(bsz * seq, aw), ob.reshape(bsz * seq, aw)], [w_out[:aw], w_out[aw:]], x_f,
                       ln1[0], ln1[1], alpha)
    return _ffn_ln(x_b, x_f, w1, w3, w2, layer, ln2[0], ln2[1], alpha)


def _moe_dispatch(idx, n_tokens, tm):
    e_flat = idx[:, :2].reshape(-1)
    onehot = (e_flat[:, None] == jnp.arange(N_EXPERTS, dtype=jnp.int32)[None, :]).astype(jnp.int32)
    rank = jnp.sum(jnp.cumsum(onehot, axis=0) * onehot, axis=1) - 1
    counts = jnp.sum(onehot, axis=0)
    tiles = (counts + tm - 1) // tm
    tile_end = jnp.cumsum(tiles)
    group_start = (tile_end - tiles) * tm
    dest = group_start[e_flat] + rank
    n_tiles = (2 * n_tokens) // tm + N_EXPERTS
    src_tok = jnp.zeros((n_tiles * tm,), jnp.int32).at[dest].set(jnp.arange(2 * n_tokens, dtype=jnp.int32) // 2)
    tile_ids = jnp.arange(n_tiles, dtype=jnp.int32)
    tile_expert = jnp.minimum(jnp.sum((tile_ids[:, None] >= tile_end[None, :]).astype(jnp.int32), axis=1),
                              N_EXPERTS - 1)
    return src_tok, dest.reshape(n_tokens, 2), tile_expert, tile_end[-1:].astype(jnp.int32)


def _odd_layer(x_f, x_b, bsz, seq, w_in, lam_params, subln_w, w_out, ln1, router, w1, w3, w2, layer, ln2, rope,
               alpha, lambda_init):
    d = x_f.shape[1]
    w_in = w_in.astype(BF16)
    w_out = w_out.astype(BF16)
    cos_t, sin_t = rope
    scale_row = jnp.concatenate([jnp.full((1, d), LOG2_E * C_HEAD_DIM ** -0.5, F32), jnp.ones((1, d), F32)], axis=1)
    router_padded = jnp.pad(router.astype(F32), ((0, 0), (0, LANES - N_EXPERTS)))
    n_parts = 2 if bsz % 2 == 0 else 1
    part_b = bsz // n_parts
    n_tok = part_b * seq
    outs = []
    for part in range(n_parts):
        xf = x_f[part * n_tok:(part + 1) * n_tok]
        xb = x_b[part * n_tok:(part + 1) * n_tok]
        qk = _proj_rope(xb, w_in[:, :2 * d], cos_t, sin_t, scale_row, seq)
        v = _proj(xb, w_in[:, 2 * d:], BF16)
        o = _diff_attention(qk.reshape(part_b, seq, 2 * d), v.reshape(part_b, seq, d), lam_params.astype(F32),
                            subln_w, lambda_init)
        xf, xb = _out_ln([o.reshape(n_tok, d)], [w_out], xf, ln1[0], ln1[1], alpha)
        gates, idx = _router(xf, router_padded)
        tm = _tile(n_tok, (1024, 512, 256))
        src_tok, pos, tile_expert, n_active = _moe_dispatch(idx, n_tok, tm)
        y = _moe_ffn(jnp.take(xb, src_tok, axis=0), tile_expert, n_active, w1, w3, w2, layer, tm)
        y1 = jnp.take(y, pos[:, 0], axis=0)
        y2 = jnp.take(y, pos[:, 1], axis=0)
        outs.append(_moe_combine(xf, y1, y2, gates, ln2[0], ln2[1], alpha))
    if n_parts == 1:
        return outs[0]
    return jnp.concatenate([o[0] for o in outs], axis=0), jnp.concatenate([o[1] for o in outs], axis=0)


def kernel(x, ev_w_in, ev_lb_logits, ev_hgrn_norm, ev_w_out, ev_ln1_g, ev_ln1_b, ev_w1, ev_w3, ev_w2, ev_ln2_g,
           ev_ln2_b, od_w_in, od_lambda, od_subln, od_w_out, od_ln1_g, od_ln1_b, od_router, od_w1, od_w3, od_w2,
           od_ln2_g, od_ln2_b):
    bsz, seq, d = x.shape
    depth = ev_w_in.shape[0] + od_w_in.shape[0]
    alpha = (2 * depth) ** 0.25
    rope = _rope_tables(seq, 512)
    lb_soft = jax.nn.softmax(ev_lb_logits.astype(F32), axis=0)
    lower_bounds = jnp.cumsum(lb_soft, axis=0) - lb_soft[0]
    x_f = x.reshape(bsz * seq, d).astype(F32)
    x_b = x_f.astype(BF16)
    for layer in range(depth):
        j = layer // 2
        if layer % 2 == 0:
            x_f, x_b = _even_layer(x_f, x_b, bsz, seq, ev_w_in[j], lower_bounds[j], ev_hgrn_norm[j], ev_w_out[j],
                                   (ev_ln1_g[j], ev_ln1_b[j]), ev_w1, ev_w3, ev_w2, j,
                                   (ev_ln2_g[j], ev_ln2_b[j]), rope, alpha)
        else:
            lambda_init = 0.8 - 0.6 * math.exp(-0.3 * layer)
            x_f, x_b = _odd_layer(x_f, x_b, bsz, seq, od_w_in[j], od_lambda[j], od_subln[j], od_w_out[j],
                                  (od_ln1_g[j], od_ln1_b[j]), od_router[j], od_w1, od_w3, od_w2, j,
                                  (od_ln2_g[j], od_ln2_b[j]), rope, alpha, lambda_init)
    return x_f.reshape(bsz, seq, d).astype(x.dtype)
```

```python
import functools
import math

import numpy as np
import jax
import jax.numpy as jnp
from jax import lax
from jax.experimental import pallas as pl
from jax.experimental.pallas import tpu as pltpu

F32 = jnp.float32
BF16 = jnp.bfloat16

A_HEAD_DIM = 128
B_HEAD_DIM = 64
B_CONFIGS = ((128, 1), (512, 4), (2048, 16))
BAND_RADIUS = 64
C_HEAD_DIM = 64
N_EXPERTS = 8
ROPE_THETA = 10000.0
LN_EPS = 1e-5
RMS_EPS = 1e-5
MASK_VALUE = -1e30
MIN_FORGET = 1e-30
LOG2_E = math.log2(math.e)

LANES = 128
SUBLANES = 8
VMEM_LIMIT_BYTES = 56 * 1024 * 1024

HGRN_CHUNK = 128
HGRN_LEVELS = 7
HGRN_HEADS_PER_STEP = 2


def _cparams(*sem):
    return pltpu.CompilerParams(dimension_semantics=sem, vmem_limit_bytes=VMEM_LIMIT_BYTES)


def _tile(n, prefs):
    for p in prefs:
        if n % p == 0:
            return p
    return n


def _dot(a, b):
    return jnp.dot(a, b, preferred_element_type=F32)


def _dot_nt(a, b):
    return lax.dot_general(a, b, (((1,), (1,)), ((), ())), preferred_element_type=F32)


def _dot_tn(a, b):
    return lax.dot_general(a, b, (((0,), (0,)), ((), ())), preferred_element_type=F32)


def _sigmoid(x):
    return 1.0 / (1.0 + jnp.exp(-x))


def _layer_norm(y, g, b):
    mu = jnp.mean(y, axis=-1, keepdims=True)
    d = y - mu
    var = jnp.mean(d * d, axis=-1, keepdims=True)
    return d * lax.rsqrt(var + LN_EPS) * g + b


def _proj_kernel(x_ref, w_ref, o_ref):
    o_ref[...] = _dot(x_ref[...], w_ref[...]).astype(o_ref.dtype)


def _proj(x, w, out_dtype):
    m, k = x.shape
    n = w.shape[1]
    tm = _tile(m, (1024, 512, 256))
    tn = n if n <= 1536 else _tile(n, (1024, 512, 256, 128))
    return pl.pallas_call(
        _proj_kernel,
        grid=(m // tm, n // tn),
        in_specs=[pl.BlockSpec((tm, k), lambda i, j: (i, 0)), pl.BlockSpec((k, tn), lambda i, j: (0, j))],
        out_specs=pl.BlockSpec((tm, tn), lambda i, j: (i, j)),
        out_shape=jax.ShapeDtypeStruct((m, n), out_dtype),
        compiler_params=_cparams("parallel", "arbitrary"),
        name="proj",
    )(x, w)


def _proj_rope_kernel(x_ref, w_ref, cos_ref, sin_ref, scale_ref, o_ref):
    acc = _dot(x_ref[...], w_ref[...])
    tn = acc.shape[1]
    lane = lax.broadcasted_iota(jnp.int32, acc.shape, 1)
    first_half = (lane % B_HEAD_DIM) < (B_HEAD_DIM // 2)
    partner = jnp.where(first_half, pltpu.roll(acc, tn - B_HEAD_DIM // 2, 1), pltpu.roll(acc, B_HEAD_DIM // 2, 1))
    o_ref[...] = ((acc * cos_ref[...] + partner * sin_ref[...]) * scale_ref[...]).astype(o_ref.dtype)


def _proj_rope(x, w, cos_t, sin_t, scale_row, seq):
    m, k = x.shape
    n = w.shape[1]
    tm = _tile(seq, (1024, 512, 256))
    tn = cos_t.shape[1]
    nsb = seq // tm
    return pl.pallas_call(
        _proj_rope_kernel,
        grid=(m // tm, n // tn),
        in_specs=[
            pl.BlockSpec((tm, k), lambda i, j: (i, 0)),
            pl.BlockSpec((k, tn), lambda i, j: (0, j)),
            pl.BlockSpec((tm, tn), lambda i, j: (i % nsb, 0)),
            pl.BlockSpec((tm, tn), lambda i, j: (i % nsb, 0)),
            pl.BlockSpec((1, tn), lambda i, j: (0, j)),
        ],
        out_specs=pl.BlockSpec((tm, tn), lambda i, j: (i, j)),
        out_shape=jax.ShapeDtypeStruct((m, n), BF16),
        compiler_params=_cparams("parallel", "arbitrary"),
        name="proj_rope",
    )(x, w, cos_t, sin_t, scale_row)


def _out_ln_kernel(*refs, n_in, alpha):
    xs = refs[:n_in]
    ws = refs[n_in:2 * n_in]
    resid_ref, g_ref, b_ref, of_ref, ob_ref = refs[2 * n_in:]
    acc = _dot(xs[0][...], ws[0][...])
    for x_ref, w_ref in zip(xs[1:], ws[1:]):
        acc = acc + _dot(x_ref[...], w_ref[...])
    z = _layer_norm(alpha * resid_ref[...] + acc, g_ref[...], b_ref[...])
    of_ref[...] = z
    ob_ref[...] = z.astype(BF16)


def _out_ln(xs, ws, resid, g, b, alpha):
    m, d = resid.shape
    tm = _tile(m, (512, 256))
    n_in = len(xs)
    in_specs = [pl.BlockSpec((tm, x.shape[1]), lambda i: (i, 0)) for x in xs]
    in_specs += [pl.BlockSpec(w.shape, lambda i: (0, 0)) for w in ws]
    in_specs += [pl.BlockSpec((tm, d), lambda i: (i, 0)), pl.BlockSpec((1, d), lambda i: (0, 0)),
                 pl.BlockSpec((1, d), lambda i: (0, 0))]
    return pl.pallas_call(
        functools.partial(_out_ln_kernel, n_in=n_in, alpha=alpha),
        grid=(m // tm,),
        in_specs=in_specs,
        out_specs=[pl.BlockSpec((tm, d), lambda i: (i, 0)), pl.BlockSpec((tm, d), lambda i: (i, 0))],
        out_shape=[jax.ShapeDtypeStruct((m, d), F32), jax.ShapeDtypeStruct((m, d), BF16)],
        compiler_params=_cparams("parallel"),
        name="out_ln",
    )(*xs, *ws, resid, g.reshape(1, d), b.reshape(1, d))


def _swiglu_accumulate(x_ref, w1, w3, w2, acc_ref):
    w1 = w1.astype(BF16)
    w3 = w3.astype(BF16)
    w2 = w2.astype(BF16)
    half = x_ref.shape[0] // 2
    rows = (pl.ds(0, half), pl.ds(half, half))
    pre = [(_dot(x_ref[r, :], w1), _dot(x_ref[r, :], w3)) for r in rows]
    for r, (h1, h3) in zip(rows, pre):
        h = (h1 * _sigmoid(h1)) * h3
        acc_ref[r, :] += _dot(h.astype(BF16), w2)


def _ffn_ln_kernel(x_ref, w1_ref, w3_ref, w2_ref, resid_ref, g_ref, b_ref, of_ref, ob_ref, acc_ref, *, alpha):
    j = pl.program_id(1)

    @pl.when(j == 0)
    def _():
        acc_ref[...] = jnp.zeros_like(acc_ref)

    _swiglu_accumulate(x_ref, w1_ref[0], w3_ref[0], w2_ref[0], acc_ref)

    @pl.when(j == pl.num_programs(1) - 1)
    def _():
        z = _layer_norm(alpha * resid_ref[...] + acc_ref[...], g_ref[...], b_ref[...])
        of_ref[...] = z
        ob_ref[...] = z.astype(BF16)


def _ffn_ln(x_bf, resid, w1, w3, w2, layer, g, b, alpha):
    m, d = resid.shape
    ff = w1.shape[2]
    tm = _tile(m, (1024, 512, 256))
    tf = _tile(ff, (256, 128))
    return pl.pallas_call(
        functools.partial(_ffn_ln_kernel, alpha=alpha),
        grid=(m // tm, ff // tf),
        in_specs=[
            pl.BlockSpec((tm, d), lambda i, j: (i, 0)),
            pl.BlockSpec((1, d, tf), lambda i, j: (layer, 0, j)),
            pl.BlockSpec((1, d, tf), lambda i, j: (layer, 0, j)),
            pl.BlockSpec((1, tf, d), lambda i, j: (layer, j, 0)),
            pl.BlockSpec((tm, d), lambda i, j: (i, 0)),
            pl.BlockSpec((1, d), lambda i, j: (0, 0)),
            pl.BlockSpec((1, d), lambda i, j: (0, 0)),
        ],
        out_specs=[pl.BlockSpec((tm, d), lambda i, j: (i, 0)), pl.BlockSpec((tm, d), lambda i, j: (i, 0))],
        out_shape=[jax.ShapeDtypeStruct((m, d), F32), jax.ShapeDtypeStruct((m, d), BF16)],
        scratch_shapes=[pltpu.VMEM((tm, d), F32)],
        compiler_params=_cparams("parallel", "arbitrary"),
        name="ffn_ln",
    )(x_bf, w1, w3, w2, resid, g.reshape(1, d), b.reshape(1, d))


def _moe_ffn_kernel(te_ref, na_ref, x_ref, w1_ref, w3_ref, w2_ref, o_ref, acc_ref):
    i = pl.program_id(0)
    j = pl.program_id(1)
    active = i < na_ref[0]

    @pl.when(active & (j == 0))
    def _():
        acc_ref[...] = jnp.zeros_like(acc_ref)

    @pl.when(active)
    def _():
        _swiglu_accumulate(x_ref, w1_ref[0, 0], w3_ref[0, 0], w2_ref[0, 0], acc_ref)

    @pl.when(active & (j == pl.num_programs(1) - 1))
    def _():
        o_ref[...] = acc_ref[...].astype(o_ref.dtype)


def _moe_ffn(x_sorted, tile_expert, n_active, w1, w3, w2, layer, tm):
    p, d = x_sorted.shape
    ff = w1.shape[3]
    tf = _tile(ff, (256, 128))
    nf = ff // tf

    def row(i, na):
        return jnp.minimum(i, na[0] - 1)

    def col(i, j, na):
        return jnp.where(i < na[0], j, nf - 1)

    grid_spec = pltpu.PrefetchScalarGridSpec(
        num_scalar_prefetch=2,
        grid=(p // tm, nf),
        in_specs=[
            pl.BlockSpec((tm, d), lambda i, j, te, na: (row(i, na), 0)),
            pl.BlockSpec((1, 1, d, tf), lambda i, j, te, na: (layer, te[row(i, na)], 0, col(i, j, na))),
            pl.BlockSpec((1, 1, d, tf), lambda i, j, te, na: (layer, te[row(i, na)], 0, col(i, j, na))),
            pl.BlockSpec((1, 1, tf, d), lambda i, j, te, na: (layer, te[row(i, na)], col(i, j, na), 0)),
        ],
        out_specs=pl.BlockSpec((tm, d), lambda i, j, te, na: (row(i, na), 0)),
        scratch_shapes=[pltpu.VMEM((tm, d), F32)],
    )
    return pl.pallas_call(
        _moe_ffn_kernel,
        grid_spec=grid_spec,
        out_shape=jax.ShapeDtypeStruct((p, d), BF16),
        compiler_params=_cparams("arbitrary", "arbitrary"),
        name="moe_ffn",
    )(tile_expert, n_active, x_sorted, w1, w3, w2)


def _router_kernel(x_ref, r_ref, gate_ref, idx_ref):
    logits = jnp.dot(x_ref[...], r_ref[...], precision=lax.Precision.HIGHEST, preferred_element_type=F32)
    lane = lax.broadcasted_iota(jnp.int32, logits.shape, 1)
    neg = jnp.float32(-jnp.inf)
    logits = jnp.where(lane < N_EXPERTS, logits, neg)
    v1 = jnp.max(logits, axis=-1, keepdims=True)
    i1 = jnp.min(jnp.where(logits == v1, lane, LANES), axis=-1, keepdims=True)
    rest = jnp.where(lane == i1, neg, logits)
    v2 = jnp.max(rest, axis=-1, keepdims=True)
    i2 = jnp.min(jnp.where(rest == v2, lane, LANES), axis=-1, keepdims=True)
    e = jnp.exp(v2 - v1)
    g1 = 1.0 / (1.0 + e)
    g2 = e / (1.0 + e)
    gate_ref[...] = jnp.where(lane == 0, g1, jnp.where(lane == 1, g2, 0.0))
    idx_ref[...] = jnp.where(lane == 0, i1, jnp.where(lane == 1, i2, 0))


def _router(x_f32, router_padded):
    m, d = x_f32.shape
    tm = _tile(m, (1024, 512, 256))
    return pl.pallas_call(
        _router_kernel,
        grid=(m // tm,),
        in_specs=[pl.BlockSpec((tm, d), lambda i: (i, 0)), pl.BlockSpec((d, LANES), lambda i: (0, 0))],
        out_specs=[pl.BlockSpec((tm, LANES), lambda i: (i, 0)), pl.BlockSpec((tm, LANES), lambda i: (i, 0))],
        out_shape=[jax.ShapeDtypeStruct((m, LANES), F32), jax.ShapeDtypeStruct((m, LANES), jnp.int32)],
        compiler_params=_cparams("parallel"),
        name="router",
    )(x_f32, router_padded)


def _moe_combine_kernel(resid_ref, y1_ref, y2_ref, gate_ref, g_ref, b_ref, of_ref, ob_ref, *, alpha):
    gates = gate_ref[...]
    y = gates[:, 0:1] * y1_ref[...].astype(F32) + gates[:, 1:2] * y2_ref[...].astype(F32)
    z = _layer_norm(alpha * resid_ref[...] + y, g_ref[...], b_ref[...])
    of_ref[...] = z
    ob_ref[...] = z.astype(BF16)


def _moe_combine(resid, y1, y2, gates, g, b, alpha):
    m, d = resid.shape
    tm = _tile(m, (512, 256))
    row = lambda i: (i, 0)
    fixed = lambda i: (0, 0)
    return pl.pallas_call(
        functools.partial(_moe_combine_kernel, alpha=alpha),
        grid=(m // tm,),
        in_specs=[pl.BlockSpec((tm, d), row), pl.BlockSpec((tm, d), row), pl.BlockSpec((tm, d), row),
                  pl.BlockSpec((tm, LANES), row), pl.BlockSpec((1, d), fixed), pl.BlockSpec((1, d), fixed)],
        out_specs=[pl.BlockSpec((tm, d), row), pl.BlockSpec((tm, d), row)],
        out_shape=[jax.ShapeDtypeStruct((m, d), F32), jax.ShapeDtypeStruct((m, d), BF16)],
        compiler_params=_cparams("parallel"),
        name="moe_combine",
    )(resid, y1, y2, gates, g.reshape(1, d), b.reshape(1, d))


def _hgrn_masks(c, levels):
    t = np.arange(c)[:, None]
    s = np.arange(c)[None, :]
    tri = np.stack([(s <= t), (s >= t)]).astype(np.float32)
    fwd, bwd = [], []
    for l in range(levels):
        same = (t >> (l + 1)) == (s >> (l + 1))
        t_up = ((t >> l) & 1) == 1
        s_up = ((s >> l) & 1) == 1
        fwd.append(same & t_up & ~s_up)
        bwd.append(same & ~t_up & s_up)
    fwd.append(t == s)
    bwd.append(t == s)
    up = np.stack([np.broadcast_to(((t >> l) & 1) == 1, (c, A_HEAD_DIM)) for l in range(levels)])
    sign = np.stack([np.where(up, 1.0, -1.0), np.where(up, -1.0, 1.0)]).astype(np.float32)
    return tri, np.stack([np.stack(fwd), np.stack(bwd)]).astype(np.float32), sign


def _segment_reference(x, level, forward):
    c, w = x.shape
    half = 1 << level
    seg = 2 * half
    idx = half - 1 if forward else half
    if seg >= SUBLANES:
        xr = x.reshape(c // seg, seg, w)
        return jnp.broadcast_to(xr[:, idx:idx + 1, :], xr.shape).reshape(c, w)
    x3 = x.reshape(c // SUBLANES, SUBLANES, w)
    sub = lax.broadcasted_iota(jnp.int32, x3.shape, 1)
    r3 = jnp.broadcast_to(x3[:, idx:idx + 1, :], x3.shape)
    for j in range(1, SUBLANES // seg):
        row = j * seg + idx
        r3 = jnp.where(sub >= j * seg, jnp.broadcast_to(x3[:, row:row + 1, :], x3.shape), r3)
    return r3.reshape(c, w)


def _hgrn_chunk(q, k, v, log_f, state_t, tri, masks, signs, forward):
    c = q.shape[0]
    x = jnp.dot(tri, log_f, precision=lax.Precision.HIGHEST, preferred_element_type=F32)
    scores = masks[HGRN_LEVELS] * _dot_nt(q.astype(BF16), k.astype(BF16))
    for level in range(HGRN_LEVELS):
        ref = _segment_reference(x, level, forward)
        sign = signs[level]
        decay = jnp.exp(sign * (x - ref))
        z = (jnp.where(sign > 0, q, k) * decay).astype(BF16)
        scores = scores + masks[level] * _dot_nt(z, z)
    x_end = x[c - 1:c, :] if forward else x[0:1, :]
    q_dec = (q * jnp.exp(x)).astype(BF16)
    o = _dot(scores.astype(BF16), v.astype(BF16)) + _dot_nt(q_dec, state_t.astype(BF16))
    k_dec = (k * jnp.exp(x_end - x)).astype(BF16)
    new_state_t = state_t * jnp.exp(x_end) + _dot_tn(v.astype(BF16), k_dec)
    return o, new_state_t


def _hgrn_kernel(q_ref, v_ref, g_ref, zf_ref, zb_ref, lb_ref, nw_ref, tri_ref, msk_ref, sgn_ref, o_ref, acc_ref, *,
                 n_chunks):
    c = HGRN_CHUNK
    hd = A_HEAD_DIM

    def gates(z, lb):
        e = jnp.exp(-jnp.abs(z))
        inv = 1.0 / (1.0 + e)
        pos = z >= 0
        sig = jnp.where(pos, inv, e * inv)
        sig_neg = jnp.where(pos, e * inv, inv)
        f = lb + (1.0 - lb) * sig
        return jnp.log(jnp.maximum(f, MIN_FORGET)), (1.0 - lb) * sig_neg

    def load(ref, c0, h):
        return ref[0, pl.ds(c0, c), h * hd:(h + 1) * hd].astype(F32)

    def emit(c0, h, tot):
        ms = jnp.mean(tot * tot, axis=-1, keepdims=True)
        g = load(g_ref, c0, h)
        out = tot * lax.rsqrt(ms + RMS_EPS) * nw_ref[...] * (g * _sigmoid(g))
        o_ref[0, pl.ds(c0, c), h * hd:(h + 1) * hd] = out.astype(o_ref.dtype)

    def one_direction(h, c0, z_ref, state_t, direction):
        lb = lb_ref[:, h * hd:(h + 1) * hd]
        log_f, k = gates(load(z_ref, c0, h), lb)
        masks = [msk_ref[direction, l] for l in range(HGRN_LEVELS + 1)]
        signs = [sgn_ref[direction, l] for l in range(HGRN_LEVELS)]
        return _hgrn_chunk(load(q_ref, c0, h), k, load(v_ref, c0, h), log_f, state_t, tri_ref[direction], masks,
                           signs, direction == 0)

    def step(i, states, finalize):
        cf = pl.multiple_of(i * c, c)
        cb = pl.multiple_of((n_chunks - 1 - i) * c, c)
        new_states = []
        for h in range(HGRN_HEADS_PER_STEP):
            o_f, st_f = one_direction(h, cf, zf_ref, states[2 * h], 0)
            o_b, st_b = one_direction(h, cb, zb_ref, states[2 * h + 1], 1)
            cols = slice(h * hd, (h + 1) * hd)
            if finalize:
                emit(cf, h, acc_ref[pl.ds(cf, c), cols] + o_f)
                emit(cb, h, acc_ref[pl.ds(cb, c), cols] + o_b)
            else:
                acc_ref[pl.ds(cf, c), cols] = o_f
                acc_ref[pl.ds(cb, c), cols] = o_b
            new_states += [st_f, st_b]
        return tuple(new_states)

    zero = jnp.zeros((hd, hd), F32)
    half = n_chunks // 2
    states = lax.fori_loop(0, half, functools.partial(step, finalize=False), (zero,) * (2 * HGRN_HEADS_PER_STEP))
    lax.fori_loop(half, n_chunks, functools.partial(step, finalize=True), states)


def _hgrn(qvg, zz, lb, norm_w, n_heads):
    bsz, s, _ = qvg.shape
    c = HGRN_CHUNK
    assert s % (2 * c) == 0 and n_heads % HGRN_HEADS_PER_STEP == 0
    tri, masks, signs = _hgrn_masks(c, HGRN_LEVELS)
    hd = A_HEAD_DIM
    wide = HGRN_HEADS_PER_STEP * hd
    n_groups = n_heads // HGRN_HEADS_PER_STEP
    col = lambda off: pl.BlockSpec((1, s, wide), lambda b, h: (b, 0, off + h))
    const = lambda a: pl.BlockSpec(a.shape, lambda b, h: (0,) * a.ndim)
    return pl.pallas_call(
        functools.partial(_hgrn_kernel, n_chunks=s // c),
        grid=(bsz, n_groups),
        in_specs=[
            col(0), col(n_groups), col(2 * n_groups), col(0), col(n_groups),
            pl.BlockSpec((1, wide), lambda b, h: (0, h)),
            pl.BlockSpec((1, hd), lambda b, h: (0, 0)),
            const(tri), const(masks), const(signs),
        ],
        out_specs=pl.BlockSpec((1, s, wide), lambda b, h: (b, 0, h)),
        out_shape=jax.ShapeDtypeStruct((bsz, s, n_heads * hd), BF16),
        scratch_shapes=[pltpu.VMEM((s, wide), F32)],
        compiler_params=_cparams("parallel", "parallel"),
        name="hgrn2",
    )(qvg, qvg, qvg, zz, zz, lb.reshape(1, -1), norm_w.reshape(1, hd), jnp.asarray(tri), jnp.asarray(masks),
      jnp.asarray(signs))


def _dilated_kernel(q_ref, k_ref, v_ref, o_ref, qf_ref, kf_ref, vf_ref, oc_ref, lc_ref, *, seq):
    qf_ref[...] = q_ref[0].astype(F32)
    kf_ref[...] = k_ref[0].astype(F32)
    vf_ref[...] = v_ref[0].astype(F32)
    n_cfg = len(B_CONFIGS)
    for ci, (_, dil) in enumerate(B_CONFIGS):
        length = seq // dil
        tq = min(LANES, length)
        win = min(length, tq + 2 * BAND_RADIUS)
        head0 = lax.broadcasted_iota(jnp.int32, (tq, LANES), 1) < B_HEAD_DIM
        rel = lax.broadcasted_iota(jnp.int32, (tq, win), 1) - lax.broadcasted_iota(jnp.int32, (tq, win), 0)

        def rows(first, size, dil=dil):
            return pl.ds(first, size) if dil == 1 else pl.ds(first, size, stride=dil)

        def block(t, carry, ci=ci, dil=dil, length=length, tq=tq, win=win, head0=head0, rel=rel, rows=rows):
            res = t % dil
            q0 = (t // dil) * tq
            start = jnp.clip(q0 - BAND_RADIUS, 0, length - win)
            valid = jnp.abs(rel + (start - q0)) <= BAND_RADIUS
            q_rows = rows(q0 * dil + res, tq)
            k_rows = rows(start * dil + res, win)
            q = qf_ref[q_rows, :].astype(BF16)
            kw = kf_ref[k_rows, :].astype(BF16)
            vw = vf_ref[k_rows, :].astype(BF16)

            def one_head(mask):
                s = _dot_nt(jnp.where(mask, q, jnp.zeros_like(q)), kw)
                s = jnp.where(valid, s, MASK_VALUE)
                m = jnp.max(s, axis=-1, keepdims=True)
                p = jnp.exp2(s - m)
                l = jnp.sum(p, axis=-1, keepdims=True)
                return _dot(p.astype(BF16), vw) / l, m + jnp.log(l) * LOG2_E

            oa, la = one_head(head0)
            ob, lb = one_head(jnp.logical_not(head0))
            oc_ref[ci, q_rows, :] = jnp.where(head0, oa, ob)
            lc_ref[ci, q_rows, :] = jnp.where(head0, la, lb)
            return carry

        lax.fori_loop(0, dil * (length // tq), block, 0, unroll=8)

    tmix = min(256, seq)

    def mix(i, carry):
        r0 = pl.multiple_of(i * tmix, tmix)
        lses = [lc_ref[c, pl.ds(r0, tmix), :] for c in range(n_cfg)]
        top = lses[0]
        for l in lses[1:]:
            top = jnp.maximum(top, l)
        num = jnp.zeros((tmix, LANES), F32)
        den = jnp.zeros((tmix, LANES), F32)
        for c in range(n_cfg):
            w = jnp.exp2(lses[c] - top)
            num = num + w * oc_ref[c, pl.ds(r0, tmix), :]
            den = den + w
        o_ref[0, pl.ds(r0, tmix), :] = (num / den).astype(o_ref.dtype)
        return carry

    lax.fori_loop(0, seq // tmix, mix, 0)


def _dilated_attention(qk, v):
    bsz, s, w = v.shape
    n_pairs = w // LANES
    n_cfg = len(B_CONFIGS)
    return pl.pallas_call(
        functools.partial(_dilated_kernel, seq=s),
        grid=(bsz, n_pairs),
        in_specs=[
            pl.BlockSpec((1, s, LANES), lambda b, h: (b, 0, h)),
            pl.BlockSpec((1, s, LANES), lambda b, h: (b, 0, n_pairs + h)),
            pl.BlockSpec((1, s, LANES), lambda b, h: (b, 0, h)),
        ],
        out_specs=pl.BlockSpec((1, s, LANES), lambda b, h: (b, 0, h)),
        out_shape=jax.ShapeDtypeStruct((bsz, s, w), BF16),
        scratch_shapes=[pltpu.VMEM((s, LANES), F32)] * 3 + [pltpu.VMEM((n_cfg, s, LANES), F32)] * 2,
        compiler_params=_cparams("parallel", "parallel"),
        name="dilated_attn",
    )(qk, qk, v)


def _diff_kernel(q_ref, k_ref, v_ref, lam_ref, sub_ref, o_ref, *, tk, lambda_init):
    q = q_ref[0]
    tq = q.shape[0]
    s_len = k_ref.shape[1]
    lane = lax.broadcasted_iota(jnp.int32, q.shape, 1)
    zero = jnp.zeros_like(q)
    qs = (jnp.where(lane < C_HEAD_DIM, q, zero), jnp.where(lane >= C_HEAD_DIM, q, zero))
    tiles = [(t * LANES, (t + 1) * LANES) for t in range(tk // LANES)]
    m = [jnp.full((tq, 1), -jnp.inf, F32)] * 2
    l = [jnp.zeros((tq, 1), F32)] * 2
    acc = [jnp.zeros((tq, LANES), F32)] * 2
    for c in range(s_len // tk):
        lo, hi = c * tk, (c + 1) * tk
        for h in range(2):
            s = _dot_nt(qs[h], k_ref[0, lo:hi, :])
            m_tile = s[:, 0:LANES]
            for a, b in tiles[1:]:
                m_tile = jnp.maximum(m_tile, s[:, a:b])
            m_new = jnp.maximum(m[h], jnp.max(m_tile, axis=-1, keepdims=True))
            alpha = jnp.exp2(m[h] - m_new)
            p = jnp.exp2(s - m_new)
            l_tile = p[:, 0:LANES]
            for a, b in tiles[1:]:
                l_tile = l_tile + p[:, a:b]
            l[h] = alpha * l[h] + jnp.sum(l_tile, axis=-1, keepdims=True)
            acc[h] = alpha * acc[h] + _dot(p.astype(BF16), v_ref[0, lo:hi, :])
            m[h] = m_new
    outs = (acc[0] / l[0], acc[1] / l[1])
    lp = lam_ref[...]
    lam = (jnp.exp(jnp.sum(lp[0:1] * lp[1:2], axis=-1, keepdims=True))
           - jnp.exp(jnp.sum(lp[2:3] * lp[3:4], axis=-1, keepdims=True)) + lambda_init)
    o = outs[0] - lam * outs[1]
    ms_o = jnp.mean(o * o, axis=-1, keepdims=True)
    o_ref[0] = (o * lax.rsqrt(ms_o + RMS_EPS) * sub_ref[...] * (1.0 - lambda_init)).astype(o_ref.dtype)


def _diff_attention(qk, v, lam_params, subln_w, lambda_init):
    bsz, s, w = v.shape
    n_heads = w // LANES
    tq = _tile(s, (1024, 512, 256, 128))
    tk = _tile(s, (2048, 1024, 512, 256, 128))
    return pl.pallas_call(
        functools.partial(_diff_kernel, tk=tk, lambda_init=lambda_init),
        grid=(bsz, n_heads, s // tq),
        in_specs=[
            pl.BlockSpec((1, tq, LANES), lambda b, h, i: (b, i, h)),
            pl.BlockSpec((1, s, LANES), lambda b, h, i: (b, 0, n_heads + h)),
            pl.BlockSpec((1, s, LANES), lambda b, h, i: (b, 0, h)),
            pl.BlockSpec(lam_params.shape, lambda b, h, i: (0, 0)),
            pl.BlockSpec((1, LANES), lambda b, h, i: (0, 0)),
        ],
        out_specs=pl.BlockSpec((1, tq, LANES), lambda b, h, i: (b, i, h)),
        out_shape=jax.ShapeDtypeStruct((bsz, s, w), BF16),
        compiler_params=_cparams("parallel", "parallel", "arbitrary"),
        name="diff_attn",
    )(qk, qk, v, lam_params, subln_w.reshape(1, LANES))


def _rope_tables(seq, width):
    half = B_HEAD_DIM // 2
    inv = ROPE_THETA ** (-jnp.arange(0, B_HEAD_DIM, 2, dtype=F32) / B_HEAD_DIM)
    ang = jnp.arange(seq, dtype=F32)[:, None] * inv[None, :]
    cos, sin = jnp.cos(ang), jnp.sin(ang)
    reps = width // B_HEAD_DIM
    assert half * 2 == B_HEAD_DIM
    return jnp.tile(jnp.concatenate([cos, cos], axis=1), (1, reps)), jnp.tile(jnp.concatenate([-sin, sin], axis=1), (1, reps))


def _even_layer(x_f, x_b, bsz, seq, w_in, lb, norm_w, w_out, ln1, w1, w3, w2, layer, ln2, rope, alpha):
    d = x_f.shape[1]
    aw = d // 2
    n_heads_a = aw // A_HEAD_DIM
    w_in = w_in.astype(BF16)
    cols = lambda a, b: w_in[:, a * aw:b * aw]
    qvg = _proj(x_b, jnp.concatenate([cols(0, 1), cols(3, 5)], axis=1), BF16)
    zz = _proj(x_b, cols(1, 3), F32)
    cos_t, sin_t = rope
    scale_row = jnp.concatenate([jnp.full((1, aw), LOG2_E * B_HEAD_DIM ** -0.5, F32), jnp.ones((1, aw), F32)], axis=1)
    qk = _proj_rope(x_b, cols(5, 7), cos_t, sin_t, scale_row, seq)
    vb = _proj(x_b, cols(7, 8), BF16)
    oa = _hgrn(qvg.reshape(bsz, seq, -1), zz.reshape(bsz, seq, -1), lb, norm_w, n_heads_a)
    ob = _dilated_attention(qk.reshape(bsz, seq, 2 * aw), vb.reshape(bsz, seq, aw))
    w_out = w_out.astype(BF16)
    x_f, x_b = _out_ln([oa.reshape(bsz * seq, aw), ob.reshape(bsz * seq, aw)], [w_out[:aw], w_out[aw:]], x_f,
                       ln1[0], ln1[1], alpha)
    return _ffn_ln(x_b, x_f, w1, w3, w2, layer, ln2[0], ln2[1], alpha)


def _moe_dispatch(idx, n_tokens, tm):
    e_flat = idx[:, :2].reshape(-1)
    onehot = (e_flat[:, None] == jnp.arange(N_EXPERTS, dtype=jnp.int32)[None, :]).astype(jnp.int32)
    rank = jnp.sum(jnp.cumsum(onehot, axis=0) * onehot, axis=1) - 1
    counts = jnp.sum(onehot, axis=0)
    tiles = (counts + tm - 1) // tm
    tile_end = jnp.cumsum(tiles)
    group_start = (tile_end - tiles) * tm
    dest = group_start[e_flat] + rank
    n_tiles = (2 * n_tokens) // tm + N_EXPERTS
    src_tok = jnp.zeros((n_tiles * tm,), jnp.int32).at[dest].set(jnp.arange(2 * n_tokens, dtype=jnp.int32) // 2)
    tile_ids = jnp.arange(n_tiles, dtype=jnp.int32)
    tile_expert = jnp.minimum(jnp.sum((tile_ids[:, None] >= tile_end[None, :]).astype(jnp.int32), axis=1),
                              N_EXPERTS - 1)
    return src_tok, dest.reshape(n_tokens, 2), tile_expert, tile_end[-1:].astype(jnp.int32)


def _odd_layer(x_f, x_b, bsz, seq, w_in, lam_params, subln_w, w_out, ln1, router, w1, w3, w2, layer, ln2, rope,
               alpha, lambda_init):
    d = x_f.shape[1]
    n_tok = bsz * seq
    w_in = w_in.astype(BF16)
    cos_t, sin_t = rope
    scale_row = jnp.concatenate([jnp.full((1, d), LOG2_E * C_HEAD_DIM ** -0.5, F32), jnp.ones((1, d), F32)], axis=1)
    qk = _proj_rope(x_b, w_in[:, :2 * d], cos_t, sin_t, scale_row, seq)
    v = _proj(x_b, w_in[:, 2 * d:], BF16)
    o = _diff_attention(qk.reshape(bsz, seq, 2 * d), v.reshape(bsz, seq, d), lam_params.astype(F32), subln_w,
                        lambda_init)
    x_f, x_b = _out_ln([o.reshape(n_tok, d)], [w_out.astype(BF16)], x_f, ln1[0], ln1[1], alpha)
    router_padded = jnp.pad(router.astype(F32), ((0, 0), (0, LANES - N_EXPERTS)))
    gates, idx = _router(x_f, router_padded)
    tm = _tile(n_tok, (1024, 512, 256))
    src_tok, pos, tile_expert, n_active = _moe_dispatch(idx, n_tok, tm)
    y = _moe_ffn(jnp.take(x_b, src_tok, axis=0), tile_expert, n_active, w1, w3, w2, layer, tm)
    y1 = jnp.take(y, pos[:, 0], axis=0)
    y2 = jnp.take(y, pos[:, 1], axis=0)
    return _moe_combine(x_f, y1, y2, gates, ln2[0], ln2[1], alpha)


def kernel(x, ev_w_in, ev_lb_logits, ev_hgrn_norm, ev_w_out, ev_ln1_g, ev_ln1_b, ev_w1, ev_w3, ev_w2, ev_ln2_g,
           ev_ln2_b, od_w_in, od_lambda, od_subln, od_w_out, od_ln1_g, od_ln1_b, od_router, od_w1, od_w3, od_w2,
           od_ln2_g, od_ln2_b):
    bsz, seq, d = x.shape
    depth = ev_w_in.shape[0] + od_w_in.shape[0]
    alpha = (2 * depth) ** 0.25
    rope = _rope_tables(seq, 512)
    lb_soft = jax.nn.softmax(ev_lb_logits.astype(F32), axis=0)
    lower_bounds = jnp.cumsum(lb_soft, axis=0) - lb_soft[0]
    x_f = x.reshape(bsz * seq, d).astype(F32)
    x_b = x_f.astype(BF16)
    for layer in range(depth):
        j = layer // 2
        if layer % 2 == 0:
            x_f, x_b = _even_layer(x_f, x_b, bsz, seq, ev_w_in[j], lower_bounds[j], ev_hgrn_norm[j], ev_w_out[j],
                                   (ev_ln1_g[j], ev_ln1_b[j]), ev_w1, ev_w3, ev_w2, j,
                                   (ev_ln2_g[j], ev_ln2_b[j]), rope, alpha)
        else:
            lambda_init = 0.8 - 0.6 * math.exp(-0.3 * layer)
            x_f, x_b = _odd_layer(x_f, x_b, bsz, seq, od_w_in[j], od_lambda[j], od_subln[j], od_w_out[j],
                                  (od_ln1_g[j], od_ln1_b[j]), od_router[j], od_w1, od_w3, od_w2, j,
                                  (od_ln2_g[j], od_ln2_b[j]), rope, alpha, lambda_init)
    return x_f.reshape(bsz, seq, d).astype(x.dtype)
```

```python
import functools
import math

import numpy as np
import jax
import jax.numpy as jnp
from jax import lax
from jax.experimental import pallas as pl
from jax.experimental.pallas import tpu as pltpu

F32 = jnp.float32
BF16 = jnp.bfloat16

A_HEAD_DIM = 128
B_HEAD_DIM = 64
B_CONFIGS = ((128, 1), (512, 4), (2048, 16))
BAND_RADIUS = 64
C_HEAD_DIM = 64
N_EXPERTS = 8
ROPE_THETA = 10000.0
LN_EPS = 1e-5
RMS_EPS = 1e-5
MASK_VALUE = -1e30
MIN_FORGET = 1e-30
LOG2_E = math.log2(math.e)

LANES = 128
SUBLANES = 8
VMEM_LIMIT_BYTES = 56 * 1024 * 1024

ROW_CHAINS = 4

HGRN_CHUNK = 128
HGRN_LEVELS = 7
HGRN_HEADS_PER_STEP = 2


def _cparams(*sem):
    return pltpu.CompilerParams(dimension_semantics=sem, vmem_limit_bytes=VMEM_LIMIT_BYTES)


def _tile(n, prefs):
    for p in prefs:
        if n % p == 0:
            return p
    return n


def _dot(a, b):
    return jnp.dot(a, b, preferred_element_type=F32)


def _dot_nt(a, b):
    return lax.dot_general(a, b, (((1,), (1,)), ((), ())), preferred_element_type=F32)


def _dot_tn(a, b):
    return lax.dot_general(a, b, (((0,), (0,)), ((), ())), preferred_element_type=F32)


def _sigmoid(x):
    return 1.0 / (1.0 + jnp.exp(-x))


def _layer_norm(y, g, b):
    mu = jnp.mean(y, axis=-1, keepdims=True)
    d = y - mu
    var = jnp.mean(d * d, axis=-1, keepdims=True)
    return d * lax.rsqrt(var + LN_EPS) * g + b


def _proj_kernel(x_ref, w_ref, o_ref):
    o_ref[...] = _dot(x_ref[...], w_ref[...]).astype(o_ref.dtype)


def _proj(x, w, out_dtype):
    m, k = x.shape
    n = w.shape[1]
    tm = _tile(m, (1024, 512, 256))
    tn = n if n <= 1536 else _tile(n, (1024, 512, 256, 128))
    return pl.pallas_call(
        _proj_kernel,
        grid=(m // tm, n // tn),
        in_specs=[pl.BlockSpec((tm, k), lambda i, j: (i, 0)), pl.BlockSpec((k, tn), lambda i, j: (0, j))],
        out_specs=pl.BlockSpec((tm, tn), lambda i, j: (i, j)),
        out_shape=jax.ShapeDtypeStruct((m, n), out_dtype),
        compiler_params=_cparams("parallel", "arbitrary"),
        name="proj",
    )(x, w)


def _proj_rope_kernel(x_ref, w_ref, cos_ref, sin_ref, scale_ref, o_ref):
    tm, tn = o_ref.shape
    sub = tm // ROW_CHAINS
    w = w_ref[...]
    lane = lax.broadcasted_iota(jnp.int32, (sub, tn), 1)
    first_half = (lane % B_HEAD_DIM) < (B_HEAD_DIM // 2)
    for k in range(ROW_CHAINS):
        rows = pl.ds(k * sub, sub)
        acc = _dot(x_ref[rows, :], w)
        partner = jnp.where(first_half, pltpu.roll(acc, tn - B_HEAD_DIM // 2, 1), pltpu.roll(acc, B_HEAD_DIM // 2, 1))
        o_ref[rows, :] = ((acc * cos_ref[rows, :] + partner * sin_ref[rows, :]) * scale_ref[...]).astype(o_ref.dtype)


def _proj_rope(x, w, cos_t, sin_t, scale_row, seq):
    m, k = x.shape
    n = w.shape[1]
    tm = _tile(seq, (1024, 512, 256))
    tn = cos_t.shape[1]
    nsb = seq // tm
    return pl.pallas_call(
        _proj_rope_kernel,
        grid=(m // tm, n // tn),
        in_specs=[
            pl.BlockSpec((tm, k), lambda i, j: (i, 0)),
            pl.BlockSpec((k, tn), lambda i, j: (0, j)),
            pl.BlockSpec((tm, tn), lambda i, j: (i % nsb, 0)),
            pl.BlockSpec((tm, tn), lambda i, j: (i % nsb, 0)),
            pl.BlockSpec((1, tn), lambda i, j: (0, j)),
        ],
        out_specs=pl.BlockSpec((tm, tn), lambda i, j: (i, j)),
        out_shape=jax.ShapeDtypeStruct((m, n), BF16),
        compiler_params=_cparams("parallel", "arbitrary"),
        name="proj_rope",
    )(x, w, cos_t, sin_t, scale_row)


def _out_ln_kernel(*refs, n_in, alpha):
    xs = refs[:n_in]
    ws = refs[n_in:2 * n_in]
    resid_ref, g_ref, b_ref, of_ref, ob_ref = refs[2 * n_in:]
    sub = of_ref.shape[0] // ROW_CHAINS
    for k in range(ROW_CHAINS):
        rows = pl.ds(k * sub, sub)
        acc = _dot(xs[0][rows, :], ws[0][...])
        for x_ref, w_ref in zip(xs[1:], ws[1:]):
            acc = acc + _dot(x_ref[rows, :], w_ref[...])
        z = _layer_norm(alpha * resid_ref[rows, :] + acc, g_ref[...], b_ref[...])
        of_ref[rows, :] = z
        ob_ref[rows, :] = z.astype(BF16)


def _out_ln(xs, ws, resid, g, b, alpha):
    m, d = resid.shape
    tm = _tile(m, (1024, 512, 256))
    n_in = len(xs)
    in_specs = [pl.BlockSpec((tm, x.shape[1]), lambda i: (i, 0)) for x in xs]
    in_specs += [pl.BlockSpec(w.shape, lambda i: (0, 0)) for w in ws]
    in_specs += [pl.BlockSpec((tm, d), lambda i: (i, 0)), pl.BlockSpec((1, d), lambda i: (0, 0)),
                 pl.BlockSpec((1, d), lambda i: (0, 0))]
    return pl.pallas_call(
        functools.partial(_out_ln_kernel, n_in=n_in, alpha=alpha),
        grid=(m // tm,),
        in_specs=in_specs,
        out_specs=[pl.BlockSpec((tm, d), lambda i: (i, 0)), pl.BlockSpec((tm, d), lambda i: (i, 0))],
        out_shape=[jax.ShapeDtypeStruct((m, d), F32), jax.ShapeDtypeStruct((m, d), BF16)],
        compiler_params=_cparams("parallel"),
        name="out_ln",
    )(*xs, *ws, resid, g.reshape(1, d), b.reshape(1, d))


def _swiglu_accumulate(x_ref, w1, w3, w2, acc_ref):
    w1 = w1.astype(BF16)
    w3 = w3.astype(BF16)
    w2 = w2.astype(BF16)
    half = x_ref.shape[0] // 2
    rows = (pl.ds(0, half), pl.ds(half, half))
    pre = [(_dot(x_ref[r, :], w1), _dot(x_ref[r, :], w3)) for r in rows]
    for r, (h1, h3) in zip(rows, pre):
        h = (h1 * _sigmoid(h1)) * h3
        acc_ref[r, :] += _dot(h.astype(BF16), w2)


def _ffn_ln_kernel(x_ref, w1_ref, w3_ref, w2_ref, resid_ref, g_ref, b_ref, of_ref, ob_ref, acc_ref, *, alpha):
    j = pl.program_id(1)

    @pl.when(j == 0)
    def _():
        acc_ref[...] = jnp.zeros_like(acc_ref)

    _swiglu_accumulate(x_ref, w1_ref[0], w3_ref[0], w2_ref[0], acc_ref)

    @pl.when(j == pl.num_programs(1) - 1)
    def _():
        z = _layer_norm(alpha * resid_ref[...] + acc_ref[...], g_ref[...], b_ref[...])
        of_ref[...] = z
        ob_ref[...] = z.astype(BF16)


def _ffn_ln(x_bf, resid, w1, w3, w2, layer, g, b, alpha):
    m, d = resid.shape
    ff = w1.shape[2]
    tm = _tile(m, (1024, 512, 256))
    tf = _tile(ff, (256, 128))
    return pl.pallas_call(
        functools.partial(_ffn_ln_kernel, alpha=alpha),
        grid=(m // tm, ff // tf),
        in_specs=[
            pl.BlockSpec((tm, d), lambda i, j: (i, 0)),
            pl.BlockSpec((1, d, tf), lambda i, j: (layer, 0, j)),
            pl.BlockSpec((1, d, tf), lambda i, j: (layer, 0, j)),
            pl.BlockSpec((1, tf, d), lambda i, j: (layer, j, 0)),
            pl.BlockSpec((tm, d), lambda i, j: (i, 0)),
            pl.BlockSpec((1, d), lambda i, j: (0, 0)),
            pl.BlockSpec((1, d), lambda i, j: (0, 0)),
        ],
        out_specs=[pl.BlockSpec((tm, d), lambda i, j: (i, 0)), pl.BlockSpec((tm, d), lambda i, j: (i, 0))],
        out_shape=[jax.ShapeDtypeStruct((m, d), F32), jax.ShapeDtypeStruct((m, d), BF16)],
        scratch_shapes=[pltpu.VMEM((tm, d), F32)],
        compiler_params=_cparams("parallel", "arbitrary"),
        name="ffn_ln",
    )(x_bf, w1, w3, w2, resid, g.reshape(1, d), b.reshape(1, d))


def _moe_ffn_kernel(te_ref, na_ref, x_ref, w1_ref, w3_ref, w2_ref, o_ref, acc_ref):
    i = pl.program_id(0)
    j = pl.program_id(1)
    active = i < na_ref[0]

    @pl.when(active & (j == 0))
    def _():
        acc_ref[...] = jnp.zeros_like(acc_ref)

    @pl.when(active)
    def _():
        _swiglu_accumulate(x_ref, w1_ref[0, 0], w3_ref[0, 0], w2_ref[0, 0], acc_ref)

    @pl.when(active & (j == pl.num_programs(1) - 1))
    def _():
        o_ref[...] = acc_ref[...].astype(o_ref.dtype)


def _moe_ffn(x_sorted, tile_expert, n_active, w1, w3, w2, layer, tm):
    p, d = x_sorted.shape
    ff = w1.shape[3]
    tf = _tile(ff, (256, 128))
    nf = ff // tf

    def row(i, na):
        return jnp.maximum(jnp.minimum(i, na[0] - 1), 0)

    def col(i, j, na):
        return jnp.where(i < na[0], j, nf - 1)

    grid_spec = pltpu.PrefetchScalarGridSpec(
        num_scalar_prefetch=2,
        grid=(p // tm, nf),
        in_specs=[
            pl.BlockSpec((tm, d), lambda i, j, te, na: (row(i, na), 0)),
            pl.BlockSpec((1, 1, d, tf), lambda i, j, te, na: (layer, te[row(i, na)], 0, col(i, j, na))),
            pl.BlockSpec((1, 1, d, tf), lambda i, j, te, na: (layer, te[row(i, na)], 0, col(i, j, na))),
            pl.BlockSpec((1, 1, tf, d), lambda i, j, te, na: (layer, te[row(i, na)], col(i, j, na), 0)),
        ],
        out_specs=pl.BlockSpec((tm, d), lambda i, j, te, na: (row(i, na), 0)),
        scratch_shapes=[pltpu.VMEM((tm, d), F32)],
    )
    return pl.pallas_call(
        _moe_ffn_kernel,
        grid_spec=grid_spec,
        out_shape=jax.ShapeDtypeStruct((p, d), BF16),
        compiler_params=_cparams("arbitrary", "arbitrary"),
        name="moe_ffn",
    )(tile_expert, n_active, x_sorted, w1, w3, w2)


def _router_kernel(x_ref, r_ref, gate_ref, idx_ref):
    logits = jnp.dot(x_ref[...], r_ref[...], precision=lax.Precision.HIGHEST, preferred_element_type=F32)
    lane = lax.broadcasted_iota(jnp.int32, logits.shape, 1)
    neg = jnp.float32(-jnp.inf)
    logits = jnp.where(lane < N_EXPERTS, logits, neg)
    v1 = jnp.max(logits, axis=-1, keepdims=True)
    i1 = jnp.min(jnp.where(logits == v1, lane, LANES), axis=-1, keepdims=True)
    rest = jnp.where(lane == i1, neg, logits)
    v2 = jnp.max(rest, axis=-1, keepdims=True)
    i2 = jnp.min(jnp.where(rest == v2, lane, LANES), axis=-1, keepdims=True)
    e = jnp.exp(v2 - v1)
    g1 = 1.0 / (1.0 + e)
    g2 = e / (1.0 + e)
    gate_ref[...] = jnp.where(lane == 0, g1, jnp.where(lane == 1, g2, 0.0))
    idx_ref[...] = jnp.where(lane == 0, i1, jnp.where(lane == 1, i2, 0))


def _router(x_f32, router_padded):
    m, d = x_f32.shape
    tm = _tile(m, (1024, 512, 256))
    return pl.pallas_call(
        _router_kernel,
        grid=(m // tm,),
        in_specs=[pl.BlockSpec((tm, d), lambda i: (i, 0)), pl.BlockSpec((d, LANES), lambda i: (0, 0))],
        out_specs=[pl.BlockSpec((tm, LANES), lambda i: (i, 0)), pl.BlockSpec((tm, LANES), lambda i: (i, 0))],
        out_shape=[jax.ShapeDtypeStruct((m, LANES), F32), jax.ShapeDtypeStruct((m, LANES), jnp.int32)],
        compiler_params=_cparams("parallel"),
        name="router",
    )(x_f32, router_padded)


def _moe_combine_kernel(resid_ref, y1_ref, y2_ref, gate_ref, g_ref, b_ref, of_ref, ob_ref, *, alpha):
    gates = gate_ref[...]
    y = gates[:, 0:1] * y1_ref[...].astype(F32) + gates[:, 1:2] * y2_ref[...].astype(F32)
    z = _layer_norm(alpha * resid_ref[...] + y, g_ref[...], b_ref[...])
    of_ref[...] = z
    ob_ref[...] = z.astype(BF16)


def _moe_combine(resid, y1, y2, gates, g, b, alpha):
    m, d = resid.shape
    tm = _tile(m, (512, 256))
    row = lambda i: (i, 0)
    fixed = lambda i: (0, 0)
    return pl.pallas_call(
        functools.partial(_moe_combine_kernel, alpha=alpha),
        grid=(m // tm,),
        in_specs=[pl.BlockSpec((tm, d), row), pl.BlockSpec((tm, d), row), pl.BlockSpec((tm, d), row),
                  pl.BlockSpec((tm, LANES), row), pl.BlockSpec((1, d), fixed), pl.BlockSpec((1, d), fixed)],
        out_specs=[pl.BlockSpec((tm, d), row), pl.BlockSpec((tm, d), row)],
        out_shape=[jax.ShapeDtypeStruct((m, d), F32), jax.ShapeDtypeStruct((m, d), BF16)],
        compiler_params=_cparams("parallel"),
        name="moe_combine",
    )(resid, y1, y2, gates, g.reshape(1, d), b.reshape(1, d))


def _hgrn_masks(c, levels):
    t = np.arange(c)[:, None]
    s = np.arange(c)[None, :]
    tri = np.stack([(s <= t), (s >= t)]).astype(np.float32)
    fwd, bwd = [], []
    for l in range(levels):
        same = (t >> (l + 1)) == (s >> (l + 1))
        t_up = ((t >> l) & 1) == 1
        s_up = ((s >> l) & 1) == 1
        fwd.append(same & t_up & ~s_up)
        bwd.append(same & ~t_up & s_up)
    fwd.append(t == s)
    bwd.append(t == s)
    up = np.stack([np.broadcast_to(((t >> l) & 1) == 1, (c, A_HEAD_DIM)) for l in range(levels)])
    sign = np.stack([np.where(up, 1.0, -1.0), np.where(up, -1.0, 1.0)]).astype(np.float32)
    return tri, np.stack([np.stack(fwd), np.stack(bwd)]).astype(np.float32), sign


def _segment_reference(x, level, forward):
    c, w = x.shape
    half = 1 << level
    seg = 2 * half
    idx = half - 1 if forward else half
    if seg >= SUBLANES:
        xr = x.reshape(c // seg, seg, w)
        return jnp.broadcast_to(xr[:, idx:idx + 1, :], xr.shape).reshape(c, w)
    x3 = x.reshape(c // SUBLANES, SUBLANES, w)
    sub = lax.broadcasted_iota(jnp.int32, x3.shape, 1)
    r3 = jnp.broadcast_to(x3[:, idx:idx + 1, :], x3.shape)
    for j in range(1, SUBLANES // seg):
        row = j * seg + idx
        r3 = jnp.where(sub >= j * seg, jnp.broadcast_to(x3[:, row:row + 1, :], x3.shape), r3)
    return r3.reshape(c, w)


def _hgrn_chunk(q, k, v, log_f, state_t, tri, masks, signs, forward):
    c = q.shape[0]
    x = jnp.dot(tri, log_f, precision=lax.Precision.HIGHEST, preferred_element_type=F32)
    scores = masks[HGRN_LEVELS] * _dot_nt(q.astype(BF16), k.astype(BF16))
    for level in range(HGRN_LEVELS):
        ref = _segment_reference(x, level, forward)
        sign = signs[level]
        decay = jnp.exp(sign * (x - ref))
        z = (jnp.where(sign > 0, q, k) * decay).astype(BF16)
        scores = scores + masks[level] * _dot_nt(z, z)
    x_end = x[c - 1:c, :] if forward else x[0:1, :]
    q_dec = (q * jnp.exp(x)).astype(BF16)
    o = _dot(scores.astype(BF16), v.astype(BF16)) + _dot_nt(q_dec, state_t.astype(BF16))
    k_dec = (k * jnp.exp(x_end - x)).astype(BF16)
    new_state_t = state_t * jnp.exp(x_end) + _dot_tn(v.astype(BF16), k_dec)
    return o, new_state_t


def _hgrn_kernel(q_ref, v_ref, g_ref, zf_ref, zb_ref, lb_ref, nw_ref, tri_ref, msk_ref, sgn_ref, o_ref, acc_ref, *,
                 n_chunks):
    c = HGRN_CHUNK
    hd = A_HEAD_DIM

    def gates(z, lb):
        e = jnp.exp(-jnp.abs(z))
        inv = 1.0 / (1.0 + e)
        pos = z >= 0
        sig = jnp.where(pos, inv, e * inv)
        sig_neg = jnp.where(pos, e * inv, inv)
        f = lb + (1.0 - lb) * sig
        return jnp.log(jnp.maximum(f, MIN_FORGET)), (1.0 - lb) * sig_neg

    def load(ref, c0, h):
        return ref[0, pl.ds(c0, c), h * hd:(h + 1) * hd].astype(F32)

    def emit(c0, h, tot):
        ms = jnp.mean(tot * tot, axis=-1, keepdims=True)
        g = load(g_ref, c0, h)
        out = tot * lax.rsqrt(ms + RMS_EPS) * nw_ref[...] * (g * _sigmoid(g))
        o_ref[0, pl.ds(c0, c), h * hd:(h + 1) * hd] = out.astype(o_ref.dtype)

    def one_direction(h, c0, z_ref, state_t, direction):
        lb = lb_ref[:, h * hd:(h + 1) * hd]
        log_f, k = gates(load(z_ref, c0, h), lb)
        masks = [msk_ref[direction, l] for l in range(HGRN_LEVELS + 1)]
        signs = [sgn_ref[direction, l] for l in range(HGRN_LEVELS)]
        return _hgrn_chunk(load(q_ref, c0, h), k, load(v_ref, c0, h), log_f, state_t, tri_ref[direction], masks,
                           signs, direction == 0)

    def step(i, states, finalize):
        cf = pl.multiple_of(i * c, c)
        cb = pl.multiple_of((n_chunks - 1 - i) * c, c)
        new_states = []
        for h in range(HGRN_HEADS_PER_STEP):
            o_f, st_f = one_direction(h, cf, zf_ref, states[2 * h], 0)
            o_b, st_b = one_direction(h, cb, zb_ref, states[2 * h + 1], 1)
            cols = slice(h * hd, (h + 1) * hd)
            if finalize:
                emit(cf, h, acc_ref[pl.ds(cf, c), cols] + o_f)
                emit(cb, h, acc_ref[pl.ds(cb, c), cols] + o_b)
            else:
                acc_ref[pl.ds(cf, c), cols] = o_f
                acc_ref[pl.ds(cb, c), cols] = o_b
            new_states += [st_f, st_b]
        return tuple(new_states)

    zero = jnp.zeros((hd, hd), F32)
    half = n_chunks // 2
    states = lax.fori_loop(0, half, functools.partial(step, finalize=False), (zero,) * (2 * HGRN_HEADS_PER_STEP))
    lax.fori_loop(half, n_chunks, functools.partial(step, finalize=True), states)


def _hgrn(qvg, zz, lb, norm_w, n_heads):
    bsz, s, _ = qvg.shape
    c = HGRN_CHUNK
    assert s % (2 * c) == 0 and n_heads % HGRN_HEADS_PER_STEP == 0
    tri, masks, signs = _hgrn_masks(c, HGRN_LEVELS)
    hd = A_HEAD_DIM
    wide = HGRN_HEADS_PER_STEP * hd
    n_groups = n_heads // HGRN_HEADS_PER_STEP
    col = lambda off: pl.BlockSpec((1, s, wide), lambda b, h: (b, 0, off + h))
    const = lambda a: pl.BlockSpec(a.shape, lambda b, h: (0,) * a.ndim)
    return pl.pallas_call(
        functools.partial(_hgrn_kernel, n_chunks=s // c),
        grid=(bsz, n_groups),
        in_specs=[
            col(0), col(n_groups), col(2 * n_groups), col(0), col(n_groups),
            pl.BlockSpec((1, wide), lambda b, h: (0, h)),
            pl.BlockSpec((1, hd), lambda b, h: (0, 0)),
            const(tri), const(masks), const(signs),
        ],
        out_specs=pl.BlockSpec((1, s, wide), lambda b, h: (b, 0, h)),
        out_shape=jax.ShapeDtypeStruct((bsz, s, n_heads * hd), BF16),
        scratch_shapes=[pltpu.VMEM((s, wide), F32)],
        compiler_params=_cparams("parallel", "parallel"),
        name="hgrn2",
    )(qvg, qvg, qvg, zz, zz, lb.reshape(1, -1), norm_w.reshape(1, hd), jnp.asarray(tri), jnp.asarray(masks),
      jnp.asarray(signs))


def _dilated_kernel(q_ref, k_ref, v_ref, o_ref, qf_ref, kf_ref, vf_ref, oc_ref, lc_ref, *, seq):
    qf_ref[...] = q_ref[0].astype(F32)
    kf_ref[...] = k_ref[0].astype(F32)
    vf_ref[...] = v_ref[0].astype(F32)
    n_cfg = len(B_CONFIGS)
    for ci, (_, dil) in enumerate(B_CONFIGS):
        length = seq // dil
        tq = min(LANES, length)
        win = min(length, tq + 2 * BAND_RADIUS)
        head0 = lax.broadcasted_iota(jnp.int32, (tq, LANES), 1) < B_HEAD_DIM
        rel = lax.broadcasted_iota(jnp.int32, (tq, win), 1) - lax.broadcasted_iota(jnp.int32, (tq, win), 0)

        def rows(first, size, dil=dil):
            return pl.ds(first, size) if dil == 1 else pl.ds(first, size, stride=dil)

        def block(t, carry, ci=ci, dil=dil, length=length, tq=tq, win=win, head0=head0, rel=rel, rows=rows):
            res = t % dil
            q0 = (t // dil) * tq
            start = jnp.clip(q0 - BAND_RADIUS, 0, length - win)
            valid = jnp.abs(rel + (start - q0)) <= BAND_RADIUS
            q_rows = rows(q0 * dil + res, tq)
            k_rows = rows(start * dil + res, win)
            q = qf_ref[q_rows, :].astype(BF16)
            kw = kf_ref[k_rows, :].astype(BF16)
            vw = vf_ref[k_rows, :].astype(BF16)

            def one_head(mask):
                s = _dot_nt(jnp.where(mask, q, jnp.zeros_like(q)), kw)
                s = jnp.where(valid, s, MASK_VALUE)
                m = jnp.max(s, axis=-1, keepdims=True)
                p = jnp.exp2(s - m)
                l = jnp.sum(p, axis=-1, keepdims=True)
                return _dot(p.astype(BF16), vw) / l, m + jnp.log(l) * LOG2_E

            oa, la = one_head(head0)
            ob, lb = one_head(jnp.logical_not(head0))
            oc_ref[ci, q_rows, :] = jnp.where(head0, oa, ob)
            lc_ref[ci, q_rows, :] = jnp.where(head0, la, lb)
            return carry

        lax.fori_loop(0, dil * (length // tq), block, 0, unroll=8)

    tmix = min(256, seq)

    def mix(i, carry):
        r0 = pl.multiple_of(i * tmix, tmix)
        lses = [lc_ref[c, pl.ds(r0, tmix), :] for c in range(n_cfg)]
        top = lses[0]
        for l in lses[1:]:
            top = jnp.maximum(top, l)
        num = jnp.zeros((tmix, LANES), F32)
        den = jnp.zeros((tmix, LANES), F32)
        for c in range(n_cfg):
            w = jnp.exp2(lses[c] - top)
            num = num + w * oc_ref[c, pl.ds(r0, tmix), :]
            den = den + w
        o_ref[0, pl.ds(r0, tmix), :] = (num / den).astype(o_ref.dtype)
        return carry

    lax.fori_loop(0, seq // tmix, mix, 0)


def _dilated_attention(qk, v):
    bsz, s, w = v.shape
    n_pairs = w // LANES
    n_cfg = len(B_CONFIGS)
    return pl.pallas_call(
        functools.partial(_dilated_kernel, seq=s),
        grid=(bsz, n_pairs),
        in_specs=[
            pl.BlockSpec((1, s, LANES), lambda b, h: (b, 0, h)),
            pl.BlockSpec((1, s, LANES), lambda b, h: (b, 0, n_pairs + h)),
            pl.BlockSpec((1, s, LANES), lambda b, h: (b, 0, h)),
        ],
        out_specs=pl.BlockSpec((1, s, LANES), lambda b, h: (b, 0, h)),
        out_shape=jax.ShapeDtypeStruct((bsz, s, w), BF16),
        scratch_shapes=[pltpu.VMEM((s, LANES), F32)] * 3 + [pltpu.VMEM((n_cfg, s, LANES), F32)] * 2,
        compiler_params=_cparams("parallel", "parallel"),
        name="dilated_attn",
    )(qk, qk, v)


def _diff_kernel(q_ref, k_ref, v_ref, lam_ref, sub_ref, o_ref, *, tk, lambda_init):
    q = q_ref[0]
    tq = q.shape[0]
    s_len = k_ref.shape[1]
    lane = lax.broadcasted_iota(jnp.int32, q.shape, 1)
    zero = jnp.zeros_like(q)
    qs = (jnp.where(lane < C_HEAD_DIM, q, zero), jnp.where(lane >= C_HEAD_DIM, q, zero))
    tiles = [(t * LANES, (t + 1) * LANES) for t in range(tk // LANES)]
    m = [jnp.full((tq, 1), -jnp.inf, F32)] * 2
    l = [jnp.zeros((tq, 1), F32)] * 2
    acc = [jnp.zeros((tq, LANES), F32)] * 2
    for c in range(s_len // tk):
        lo, hi = c * tk, (c + 1) * tk
        for h in range(2):
            s = _dot_nt(qs[h], k_ref[0, lo:hi, :])
            m_tile = s[:, 0:LANES]
            for a, b in tiles[1:]:
                m_tile = jnp.maximum(m_tile, s[:, a:b])
            m_new = jnp.maximum(m[h], jnp.max(m_tile, axis=-1, keepdims=True))
            alpha = jnp.exp2(m[h] - m_new)
            p = jnp.exp2(s - m_new)
            l_tile = p[:, 0:LANES]
            for a, b in tiles[1:]:
                l_tile = l_tile + p[:, a:b]
            l[h] = alpha * l[h] + jnp.sum(l_tile, axis=-1, keepdims=True)
            acc[h] = alpha * acc[h] + _dot(p.astype(BF16), v_ref[0, lo:hi, :])
            m[h] = m_new
    outs = (acc[0] / l[0], acc[1] / l[1])
    lp = lam_ref[...]
    lam = (jnp.exp(jnp.sum(lp[0:1] * lp[1:2], axis=-1, keepdims=True))
           - jnp.exp(jnp.sum(lp[2:3] * lp[3:4], axis=-1, keepdims=True)) + lambda_init)
    o = outs[0] - lam * outs[1]
    ms_o = jnp.mean(o * o, axis=-1, keepdims=True)
    o_ref[0] = (o * lax.rsqrt(ms_o + RMS_EPS) * sub_ref[...] * (1.0 - lambda_init)).astype(o_ref.dtype)


def _diff_attention(qk, v, lam_params, subln_w, lambda_init):
    bsz, s, w = v.shape
    n_heads = w // LANES
    tq = _tile(s, (1024, 512, 256, 128))
    tk = _tile(s, (2048, 1024, 512, 256, 128))
    return pl.pallas_call(
        functools.partial(_diff_kernel, tk=tk, lambda_init=lambda_init),
        grid=(bsz, n_heads, s // tq),
        in_specs=[
            pl.BlockSpec((1, tq, LANES), lambda b, h, i: (b, i, h)),
            pl.BlockSpec((1, s, LANES), lambda b, h, i: (b, 0, n_heads + h)),
            pl.BlockSpec((1, s, LANES), lambda b, h, i: (b, 0, h)),
            pl.BlockSpec(lam_params.shape, lambda b, h, i: (0, 0)),
            pl.BlockSpec((1, LANES), lambda b, h, i: (0, 0)),
        ],
        out_specs=pl.BlockSpec((1, tq, LANES), lambda b, h, i: (b, i, h)),
        out_shape=jax.ShapeDtypeStruct((bsz, s, w), BF16),
        compiler_params=_cparams("parallel", "parallel", "arbitrary"),
        name="diff_attn",
    )(qk, qk, v, lam_params, subln_w.reshape(1, LANES))


def _rope_tables(seq, width):
    half = B_HEAD_DIM // 2
    inv = ROPE_THETA ** (-jnp.arange(0, B_HEAD_DIM, 2, dtype=F32) / B_HEAD_DIM)
    ang = jnp.arange(seq, dtype=F32)[:, None] * inv[None, :]
    cos, sin = jnp.cos(ang), jnp.sin(ang)
    reps = width // B_HEAD_DIM
    assert half * 2 == B_HEAD_DIM
    return jnp.tile(jnp.concatenate([cos, cos], axis=1), (1, reps)), jnp.tile(jnp.concatenate([-sin, sin], axis=1), (1, reps))


def _even_layer(x_f, x_b, bsz, seq, w_in, lb, norm_w, w_out, ln1, w1, w3, w2, layer, ln2, rope, alpha):
    d = x_f.shape[1]
    aw = d // 2
    n_heads_a = aw // A_HEAD_DIM
    w_in = w_in.astype(BF16)
    cols = lambda a, b: w_in[:, a * aw:b * aw]
    qvg = _proj(x_b, jnp.concatenate([cols(0, 1), cols(3, 5)], axis=1), BF16)
    zz = _proj(x_b, cols(1, 3), F32)
    cos_t, sin_t = rope
    scale_row = jnp.concatenate([jnp.full((1, aw), LOG2_E * B_HEAD_DIM ** -0.5, F32), jnp.ones((1, aw), F32)], axis=1)
    qk = _proj_rope(x_b, cols(5, 7), cos_t, sin_t, scale_row, seq)
    vb = _proj(x_b, cols(7, 8), BF16)
    oa = _hgrn(qvg.reshape(bsz, seq, -1), zz.reshape(bsz, seq, -1), lb, norm_w, n_heads_a)
    ob = _dilated_attention(qk.reshape(bsz, seq, 2 * aw), vb.reshape(bsz, seq, aw))
    w_out = w_out.astype(BF16)
    x_f, x_b = _out_ln([oa.reshape(bsz * seq, aw), ob.reshape(bsz * seq, aw)], [w_out[:aw], w_out[aw:]], x_f,
                       ln1[0], ln1[1], alpha)
    return _ffn_ln(x_b, x_f, w1, w3, w2, layer, ln2[0], ln2[1], alpha)


def _moe_dispatch(idx, n_tokens, tm):
    e_flat = idx[:, :2].reshape(-1)
    onehot = (e_flat[:, None] == jnp.arange(N_EXPERTS, dtype=jnp.int32)[None, :]).astype(jnp.int32)
    rank = jnp.sum(jnp.cumsum(onehot, axis=0) * onehot, axis=1) - 1
    counts = jnp.sum(onehot, axis=0)
    tiles = (counts + tm - 1) // tm
    tile_end = jnp.cumsum(tiles)
    group_start = (tile_end - tiles) * tm
    dest = group_start[e_flat] + rank
    n_tiles = (2 * n_tokens) // tm + N_EXPERTS
    src_tok = jnp.zeros((n_tiles * tm,), jnp.int32).at[dest].set(jnp.arange(2 * n_tokens, dtype=jnp.int32) // 2)
    tile_ids = jnp.arange(n_tiles, dtype=jnp.int32)
    tile_expert = jnp.minimum(jnp.sum((tile_ids[:, None] >= tile_end[None, :]).astype(jnp.int32), axis=1),
                              N_EXPERTS - 1)
    return src_tok, dest.reshape(n_tokens, 2), tile_expert, tile_end[-1:].astype(jnp.int32)


def _odd_layer(x_f, x_b, bsz, seq, w_in, lam_params, subln_w, w_out, ln1, router, w1, w3, w2, layer, ln2, rope,
               alpha, lambda_init):
    d = x_f.shape[1]
    n_tok = bsz * seq
    w_in = w_in.astype(BF16)
    cos_t, sin_t = rope
    scale_row = jnp.concatenate([jnp.full((1, d), LOG2_E * C_HEAD_DIM ** -0.5, F32), jnp.ones((1, d), F32)], axis=1)
    qk = _proj_rope(x_b, w_in[:, :2 * d], cos_t, sin_t, scale_row, seq)
    v = _proj(x_b, w_in[:, 2 * d:], BF16)
    o = _diff_attention(qk.reshape(bsz, seq, 2 * d), v.reshape(bsz, seq, d), lam_params.astype(F32), subln_w,
                        lambda_init)
    x_f, x_b = _out_ln([o.reshape(n_tok, d)], [w_out.astype(BF16)], x_f, ln1[0], ln1[1], alpha)
    router_padded = jnp.pad(router.astype(F32), ((0, 0), (0, LANES - N_EXPERTS)))
    gates, idx = _router(x_f, router_padded)
    tm = _tile(n_tok, (1024, 512, 256))
    src_tok, pos, tile_expert, n_active = _moe_dispatch(idx, n_tok, tm)
    n_tiles = src_tok.shape[0] // tm
    parts = []
    for lo, hi in ((0, n_tiles // 2), (n_tiles // 2, n_tiles)):
        x_rows = jnp.take(x_b, src_tok[lo * tm:hi * tm], axis=0)
        n_act = jnp.clip(n_active - lo, 0, hi - lo)
        parts.append(_moe_ffn(x_rows, tile_expert[lo:hi], n_act, w1, w3, w2, layer, tm))
    y = jnp.concatenate(parts, axis=0)
    y1 = jnp.take(y, pos[:, 0], axis=0)
    y2 = jnp.take(y, pos[:, 1], axis=0)
    return _moe_combine(x_f, y1, y2, gates, ln2[0], ln2[1], alpha)


def kernel(x, ev_w_in, ev_lb_logits, ev_hgrn_norm, ev_w_out, ev_ln1_g, ev_ln1_b, ev_w1, ev_w3, ev_w2, ev_ln2_g,
           ev_ln2_b, od_w_in, od_lambda, od_subln, od_w_out, od_ln1_g, od_ln1_b, od_router, od_w1, od_w3, od_w2,
           od_ln2_g, od_ln2_b):
    bsz, seq, d = x.shape
    depth = ev_w_in.shape[0] + od_w_in.shape[0]
    alpha = (2 * depth) ** 0.25
    rope = _rope_tables(seq, 512)
    lb_soft = jax.nn.softmax(ev_lb_logits.astype(F32), axis=0)
    lower_bounds = jnp.cumsum(lb_soft, axis=0) - lb_soft[0]
    x_f = x.reshape(bsz * seq, d).astype(F32)
    x_b = x_f.astype(BF16)
    for layer in range(depth):
        j = layer // 2
        if layer % 2 == 0:
            x_f, x_b = _even_layer(x_f, x_b, bsz, seq, ev_w_in[j], lower_bounds[j], ev_hgrn_norm[j], ev_w_out[j],
                                   (ev_ln1_g[j], ev_ln1_b[j]), ev_w1, ev_w3, ev_w2, j,
                                   (ev_ln2_g[j], ev_ln2_b[j]), rope, alpha)
        else:
            lambda_init = 0.8 - 0.6 * math.exp(-0.3 * layer)
            x_f, x_b = _odd_layer(x_f, x_b, bsz, seq, od_w_in[j], od_lambda[j], od_subln[j], od_w_out[j],
                                  (od_ln1_g[j], od_ln1_b[j]), od_router[j], od_w1, od_w3, od_w2, j,
                                  (od_ln2_g[j], od_ln2_b[j]), rope, alpha, lambda_init)
    return x_f.reshape(bsz, seq, d).astype(x.dtype)
```

```python
import functools
import math

import numpy as np
import jax
import jax.numpy as jnp
from jax import lax
from jax.experimental import pallas as pl
from jax.experimental.pallas import tpu as pltpu

F32 = jnp.float32
BF16 = jnp.bfloat16

A_HEAD_DIM = 128
B_HEAD_DIM = 64
B_CONFIGS = ((128, 1), (512, 4), (2048, 16))
BAND_RADIUS = 64
C_HEAD_DIM = 64
N_EXPERTS = 8
ROPE_THETA = 10000.0
LN_EPS = 1e-5
RMS_EPS = 1e-5
MASK_VALUE = -1e30
MIN_FORGET = 1e-30
LOG2_E = math.log2(math.e)

LANES = 128
SUBLANES = 8
VMEM_LIMIT_BYTES = 56 * 1024 * 1024

ROW_CHAINS = 4

HGRN_CHUNK = 128
HGRN_LEVELS = 7
HGRN_HEADS_PER_STEP = 2


def _cparams(*sem):
    return pltpu.CompilerParams(dimension_semantics=sem, vmem_limit_bytes=VMEM_LIMIT_BYTES)


def _tile(n, prefs):
    for p in prefs:
        if n % p == 0:
            return p
    return n


def _dot(a, b):
    return jnp.dot(a, b, preferred_element_type=F32)


def _dot_nt(a, b):
    return lax.dot_general(a, b, (((1,), (1,)), ((), ())), preferred_element_type=F32)


def _dot_tn(a, b):
    return lax.dot_general(a, b, (((0,), (0,)), ((), ())), preferred_element_type=F32)


def _sigmoid(x):
    return 1.0 / (1.0 + jnp.exp(-x))


def _layer_norm(y, g, b):
    mu = jnp.mean(y, axis=-1, keepdims=True)
    d = y - mu
    var = jnp.mean(d * d, axis=-1, keepdims=True)
    return d * lax.rsqrt(var + LN_EPS) * g + b


def _proj_kernel(x_ref, w_ref, o_ref):
    o_ref[...] = _dot(x_ref[...], w_ref[...]).astype(o_ref.dtype)


def _proj(x, w, out_dtype):
    m, k = x.shape
    n = w.shape[1]
    tm = _tile(m, (1024, 512, 256))
    tn = n if n <= 1536 else _tile(n, (1024, 512, 256, 128))
    return pl.pallas_call(
        _proj_kernel,
        grid=(m // tm, n // tn),
        in_specs=[pl.BlockSpec((tm, k), lambda i, j: (i, 0)), pl.BlockSpec((k, tn), lambda i, j: (0, j))],
        out_specs=pl.BlockSpec((tm, tn), lambda i, j: (i, j)),
        out_shape=jax.ShapeDtypeStruct((m, n), out_dtype),
        compiler_params=_cparams("parallel", "arbitrary"),
        name="proj",
    )(x, w)


def _proj_rope_kernel(x_ref, w_ref, cos_ref, sin_ref, scale_ref, o_ref):
    tm, tn = o_ref.shape
    sub = tm // ROW_CHAINS
    w = w_ref[...]
    lane = lax.broadcasted_iota(jnp.int32, (sub, tn), 1)
    first_half = (lane % B_HEAD_DIM) < (B_HEAD_DIM // 2)
    for k in range(ROW_CHAINS):
        rows = pl.ds(k * sub, sub)
        acc = _dot(x_ref[rows, :], w)
        partner = jnp.where(first_half, pltpu.roll(acc, tn - B_HEAD_DIM // 2, 1), pltpu.roll(acc, B_HEAD_DIM // 2, 1))
        o_ref[rows, :] = ((acc * cos_ref[rows, :] + partner * sin_ref[rows, :]) * scale_ref[...]).astype(o_ref.dtype)


def _proj_rope(x, w, cos_t, sin_t, scale_row, seq):
    m, k = x.shape
    n = w.shape[1]
    tm = _tile(seq, (1024, 512, 256))
    tn = cos_t.shape[1]
    nsb = seq // tm
    return pl.pallas_call(
        _proj_rope_kernel,
        grid=(m // tm, n // tn),
        in_specs=[
            pl.BlockSpec((tm, k), lambda i, j: (i, 0)),
            pl.BlockSpec((k, tn), lambda i, j: (0, j)),
            pl.BlockSpec((tm, tn), lambda i, j: (i % nsb, 0)),
            pl.BlockSpec((tm, tn), lambda i, j: (i % nsb, 0)),
            pl.BlockSpec((1, tn), lambda i, j: (0, j)),
        ],
        out_specs=pl.BlockSpec((tm, tn), lambda i, j: (i, j)),
        out_shape=jax.ShapeDtypeStruct((m, n), BF16),
        compiler_params=_cparams("parallel", "arbitrary"),
        name="proj_rope",
    )(x, w, cos_t, sin_t, scale_row)


def _out_ln_kernel(*refs, n_in, alpha):
    xs = refs[:n_in]
    ws = refs[n_in:2 * n_in]
    resid_ref, g_ref, b_ref, of_ref, ob_ref = refs[2 * n_in:]
    sub = of_ref.shape[0] // ROW_CHAINS
    for k in range(ROW_CHAINS):
        rows = pl.ds(k * sub, sub)
        acc = _dot(xs[0][rows, :], ws[0][...])
        for x_ref, w_ref in zip(xs[1:], ws[1:]):
            acc = acc + _dot(x_ref[rows, :], w_ref[...])
        z = _layer_norm(alpha * resid_ref[rows, :] + acc, g_ref[...], b_ref[...])
        of_ref[rows, :] = z
        ob_ref[rows, :] = z.astype(BF16)


def _out_ln(xs, ws, resid, g, b, alpha):
    m, d = resid.shape
    tm = _tile(m, (1024, 512, 256))
    n_in = len(xs)
    in_specs = [pl.BlockSpec((tm, x.shape[1]), lambda i: (i, 0)) for x in xs]
    in_specs += [pl.BlockSpec(w.shape, lambda i: (0, 0)) for w in ws]
    in_specs += [pl.BlockSpec((tm, d), lambda i: (i, 0)), pl.BlockSpec((1, d), lambda i: (0, 0)),
                 pl.BlockSpec((1, d), lambda i: (0, 0))]
    return pl.pallas_call(
        functools.partial(_out_ln_kernel, n_in=n_in, alpha=alpha),
        grid=(m // tm,),
        in_specs=in_specs,
        out_specs=[pl.BlockSpec((tm, d), lambda i: (i, 0)), pl.BlockSpec((tm, d), lambda i: (i, 0))],
        out_shape=[jax.ShapeDtypeStruct((m, d), F32), jax.ShapeDtypeStruct((m, d), BF16)],
        compiler_params=_cparams("parallel"),
        name="out_ln",
    )(*xs, *ws, resid, g.reshape(1, d), b.reshape(1, d))


def _swiglu_accumulate(x_ref, w1, w3, w2, acc_ref):
    w1 = w1.astype(BF16)
    w3 = w3.astype(BF16)
    w2 = w2.astype(BF16)
    half = x_ref.shape[0] // 2
    rows = (pl.ds(0, half), pl.ds(half, half))
    pre = [(_dot(x_ref[r, :], w1), _dot(x_ref[r, :], w3)) for r in rows]
    for r, (h1, h3) in zip(rows, pre):
        h = (h1 * _sigmoid(h1)) * h3
        acc_ref[r, :] += _dot(h.astype(BF16), w2)


def _ffn_ln_kernel(x_ref, w1_ref, w3_ref, w2_ref, resid_ref, g_ref, b_ref, of_ref, ob_ref, acc_ref, *, alpha):
    j = pl.program_id(1)

    @pl.when(j == 0)
    def _():
        acc_ref[...] = jnp.zeros_like(acc_ref)

    _swiglu_accumulate(x_ref, w1_ref[...], w3_ref[...], w2_ref[...], acc_ref)

    @pl.when(j == pl.num_programs(1) - 1)
    def _():
        z = _layer_norm(alpha * resid_ref[...] + acc_ref[...], g_ref[...], b_ref[...])
        of_ref[...] = z
        ob_ref[...] = z.astype(BF16)


def _ffn_ln(x_bf, resid, w1, w3, w2, g, b, alpha):
    m, d = resid.shape
    ff = w1.shape[1]
    tm = _tile(m, (512, 256))
    tf = _tile(ff, (1408, 256, 128))
    return pl.pallas_call(
        functools.partial(_ffn_ln_kernel, alpha=alpha),
        grid=(m // tm, ff // tf),
        in_specs=[
            pl.BlockSpec((tm, d), lambda i, j: (i, 0)),
            pl.BlockSpec((d, tf), lambda i, j: (0, j)),
            pl.BlockSpec((d, tf), lambda i, j: (0, j)),
            pl.BlockSpec((tf, d), lambda i, j: (j, 0)),
            pl.BlockSpec((tm, d), lambda i, j: (i, 0)),
            pl.BlockSpec((1, d), lambda i, j: (0, 0)),
            pl.BlockSpec((1, d), lambda i, j: (0, 0)),
        ],
        out_specs=[pl.BlockSpec((tm, d), lambda i, j: (i, 0)), pl.BlockSpec((tm, d), lambda i, j: (i, 0))],
        out_shape=[jax.ShapeDtypeStruct((m, d), F32), jax.ShapeDtypeStruct((m, d), BF16)],
        scratch_shapes=[pltpu.VMEM((tm, d), F32)],
        compiler_params=_cparams("parallel", "arbitrary"),
        name="ffn_ln",
    )(x_bf, w1, w3, w2, resid, g.reshape(1, d), b.reshape(1, d))


def _moe_ffn_kernel(te_ref, na_ref, x_ref, w1_ref, w3_ref, w2_ref, *rest):
    o_ref, acc_ref = rest[-2:]
    i = pl.program_id(0)
    j = pl.program_id(1)
    active = i < na_ref[0]

    @pl.when(active & (j == 0))
    def _():
        acc_ref[...] = jnp.zeros_like(acc_ref)

    @pl.when(active)
    def _():
        _swiglu_accumulate(x_ref, w1_ref[0, 0], w3_ref[0, 0], w2_ref[0, 0], acc_ref)

    @pl.when(active & (j == pl.num_programs(1) - 1))
    def _():
        o_ref[...] = acc_ref[...].astype(o_ref.dtype)


def _moe_ffn(x_sorted, tile_expert, n_active, w1, w3, w2, layer, tm, tile_offset, total_rows, y_prev=None):
    p, d = x_sorted.shape
    ff = w1.shape[3]
    tf = _tile(ff, (256, 128))
    nf = ff // tf

    def row(i, na):
        return jnp.maximum(jnp.minimum(i, na[0] - 1), 0)

    def col(i, j, na):
        return jnp.where(i < na[0], j, nf - 1)

    operands = [tile_expert, n_active, x_sorted, w1, w3, w2]
    in_specs = [
        pl.BlockSpec((tm, d), lambda i, j, te, na: (row(i, na), 0)),
        pl.BlockSpec((1, 1, d, tf), lambda i, j, te, na: (layer, te[row(i, na)], 0, col(i, j, na))),
        pl.BlockSpec((1, 1, d, tf), lambda i, j, te, na: (layer, te[row(i, na)], 0, col(i, j, na))),
        pl.BlockSpec((1, 1, tf, d), lambda i, j, te, na: (layer, te[row(i, na)], col(i, j, na), 0)),
    ]
    aliases = {}
    if y_prev is not None:
        in_specs.append(pl.BlockSpec(memory_space=pl.ANY))
        aliases = {len(operands): 0}
        operands.append(y_prev)
    grid_spec = pltpu.PrefetchScalarGridSpec(
        num_scalar_prefetch=2,
        grid=(p // tm, nf),
        in_specs=in_specs,
        out_specs=pl.BlockSpec((tm, d), lambda i, j, te, na: (tile_offset + row(i, na), 0)),
        scratch_shapes=[pltpu.VMEM((tm, d), F32)],
    )
    return pl.pallas_call(
        _moe_ffn_kernel,
        grid_spec=grid_spec,
        out_shape=jax.ShapeDtypeStruct((total_rows, d), BF16),
        input_output_aliases=aliases,
        compiler_params=_cparams("arbitrary", "arbitrary"),
        name="moe_ffn",
    )(*operands)


def _router_kernel(x_ref, r_ref, gate_ref, idx_ref):
    logits = jnp.dot(x_ref[...], r_ref[...], precision=lax.Precision.HIGHEST, preferred_element_type=F32)
    lane = lax.broadcasted_iota(jnp.int32, logits.shape, 1)
    neg = jnp.float32(-jnp.inf)
    logits = jnp.where(lane < N_EXPERTS, logits, neg)
    v1 = jnp.max(logits, axis=-1, keepdims=True)
    i1 = jnp.min(jnp.where(logits == v1, lane, LANES), axis=-1, keepdims=True)
    rest = jnp.where(lane == i1, neg, logits)
    v2 = jnp.max(rest, axis=-1, keepdims=True)
    i2 = jnp.min(jnp.where(rest == v2, lane, LANES), axis=-1, keepdims=True)
    e = jnp.exp(v2 - v1)
    g1 = 1.0 / (1.0 + e)
    g2 = e / (1.0 + e)
    gate_ref[...] = jnp.where(lane == 0, g1, jnp.where(lane == 1, g2, 0.0))
    idx_ref[...] = jnp.where(lane == 0, i1, jnp.where(lane == 1, i2, 0))


def _router(x_f32, router_padded):
    m, d = x_f32.shape
    tm = _tile(m, (1024, 512, 256))
    return pl.pallas_call(
        _router_kernel,
        grid=(m // tm,),
        in_specs=[pl.BlockSpec((tm, d), lambda i: (i, 0)), pl.BlockSpec((d, LANES), lambda i: (0, 0))],
        out_specs=[pl.BlockSpec((tm, LANES), lambda i: (i, 0)), pl.BlockSpec((tm, LANES), lambda i: (i, 0))],
        out_shape=[jax.ShapeDtypeStruct((m, LANES), F32), jax.ShapeDtypeStruct((m, LANES), jnp.int32)],
        compiler_params=_cparams("parallel"),
        name="router",
    )(x_f32, router_padded)


def _moe_combine_kernel(resid_ref, y1_ref, y2_ref, gate_ref, g_ref, b_ref, of_ref, ob_ref, *, alpha):
    gates = gate_ref[...]
    y = gates[:, 0:1] * y1_ref[...].astype(F32) + gates[:, 1:2] * y2_ref[...].astype(F32)
    z = _layer_norm(alpha * resid_ref[...] + y, g_ref[...], b_ref[...])
    of_ref[...] = z
    ob_ref[...] = z.astype(BF16)


def _moe_combine(resid, y1, y2, gates, g, b, alpha):
    m, d = resid.shape
    tm = _tile(m, (512, 256))
    row = lambda i: (i, 0)
    fixed = lambda i: (0, 0)
    return pl.pallas_call(
        functools.partial(_moe_combine_kernel, alpha=alpha),
        grid=(m // tm,),
        in_specs=[pl.BlockSpec((tm, d), row), pl.BlockSpec((tm, d), row), pl.BlockSpec((tm, d), row),
                  pl.BlockSpec((tm, LANES), row), pl.BlockSpec((1, d), fixed), pl.BlockSpec((1, d), fixed)],
        out_specs=[pl.BlockSpec((tm, d), row), pl.BlockSpec((tm, d), row)],
        out_shape=[jax.ShapeDtypeStruct((m, d), F32), jax.ShapeDtypeStruct((m, d), BF16)],
        compiler_params=_cparams("parallel"),
        name="moe_combine",
    )(resid, y1, y2, gates, g.reshape(1, d), b.reshape(1, d))


def _hgrn_masks(c, levels):
    t = np.arange(c)[:, None]
    s = np.arange(c)[None, :]
    tri = np.stack([(s <= t), (s >= t)]).astype(np.float32)
    fwd, bwd = [], []
    for l in range(levels):
        same = (t >> (l + 1)) == (s >> (l + 1))
        t_up = ((t >> l) & 1) == 1
        s_up = ((s >> l) & 1) == 1
        fwd.append(same & t_up & ~s_up)
        bwd.append(same & ~t_up & s_up)
    fwd.append(t == s)
    bwd.append(t == s)
    up = np.stack([np.broadcast_to(((t >> l) & 1) == 1, (c, A_HEAD_DIM)) for l in range(levels)])
    sign = np.stack([np.where(up, 1.0, -1.0), np.where(up, -1.0, 1.0)]).astype(np.float32)
    return tri, np.stack([np.stack(fwd), np.stack(bwd)]).astype(np.float32), sign


def _segment_reference(x, level, forward):
    c, w = x.shape
    half = 1 << level
    seg = 2 * half
    idx = half - 1 if forward else half
    if seg >= SUBLANES:
        xr = x.reshape(c // seg, seg, w)
        return jnp.broadcast_to(xr[:, idx:idx + 1, :], xr.shape).reshape(c, w)
    x3 = x.reshape(c // SUBLANES, SUBLANES, w)
    sub = lax.broadcasted_iota(jnp.int32, x3.shape, 1)
    r3 = jnp.broadcast_to(x3[:, idx:idx + 1, :], x3.shape)
    for j in range(1, SUBLANES // seg):
        row = j * seg + idx
        r3 = jnp.where(sub >= j * seg, jnp.broadcast_to(x3[:, row:row + 1, :], x3.shape), r3)
    return r3.reshape(c, w)


def _hgrn_chunk(q, k, v, log_f, state_t, tri, masks, signs, forward):
    c = q.shape[0]
    x = jnp.dot(tri, log_f, precision=lax.Precision.HIGHEST, preferred_element_type=F32)
    scores = masks[HGRN_LEVELS] * _dot_nt(q.astype(BF16), k.astype(BF16))
    for level in range(HGRN_LEVELS):
        ref = _segment_reference(x, level, forward)
        sign = signs[level]
        decay = jnp.exp(sign * (x - ref))
        z = (jnp.where(sign > 0, q, k) * decay).astype(BF16)
        scores = scores + masks[level] * _dot_nt(z, z)
    x_end = x[c - 1:c, :] if forward else x[0:1, :]
    q_dec = (q * jnp.exp(x)).astype(BF16)
    o = _dot(scores.astype(BF16), v.astype(BF16)) + _dot_nt(q_dec, state_t.astype(BF16))
    k_dec = (k * jnp.exp(x_end - x)).astype(BF16)
    new_state_t = state_t * jnp.exp(x_end) + _dot_tn(v.astype(BF16), k_dec)
    return o, new_state_t


def _hgrn_kernel(q_ref, v_ref, g_ref, zf_ref, zb_ref, lb_ref, nw_ref, tri_ref, msk_ref, sgn_ref, o_ref, acc_ref, *,
                 n_chunks):
    c = HGRN_CHUNK
    hd = A_HEAD_DIM

    def gates(z, lb):
        e = jnp.exp(-jnp.abs(z))
        inv = 1.0 / (1.0 + e)
        pos = z >= 0
        sig = jnp.where(pos, inv, e * inv)
        sig_neg = jnp.where(pos, e * inv, inv)
        f = lb + (1.0 - lb) * sig
        return jnp.log(jnp.maximum(f, MIN_FORGET)), (1.0 - lb) * sig_neg

    def load(ref, c0, h):
        return ref[0, pl.ds(c0, c), h * hd:(h + 1) * hd].astype(F32)

    def emit(c0, h, tot):
        ms = jnp.mean(tot * tot, axis=-1, keepdims=True)
        g = load(g_ref, c0, h)
        out = tot * lax.rsqrt(ms + RMS_EPS) * nw_ref[...] * (g * _sigmoid(g))
        o_ref[0, pl.ds(c0, c), h * hd:(h + 1) * hd] = out.astype(o_ref.dtype)

    def one_direction(h, c0, z_ref, state_t, direction):
        lb = lb_ref[:, h * hd:(h + 1) * hd]
        log_f, k = gates(load(z_ref, c0, h), lb)
        masks = [msk_ref[direction, l] for l in range(HGRN_LEVELS + 1)]
        signs = [sgn_ref[direction, l] for l in range(HGRN_LEVELS)]
        return _hgrn_chunk(load(q_ref, c0, h), k, load(v_ref, c0, h), log_f, state_t, tri_ref[direction], masks,
                           signs, direction == 0)

    def step(i, states, finalize):
        cf = pl.multiple_of(i * c, c)
        cb = pl.multiple_of((n_chunks - 1 - i) * c, c)
        new_states = []
        for h in range(HGRN_HEADS_PER_STEP):
            o_f, st_f = one_direction(h, cf, zf_ref, states[2 * h], 0)
            o_b, st_b = one_direction(h, cb, zb_ref, states[2 * h + 1], 1)
            cols = slice(h * hd, (h + 1) * hd)
            if finalize:
                emit(cf, h, acc_ref[pl.ds(cf, c), cols] + o_f)
                emit(cb, h, acc_ref[pl.ds(cb, c), cols] + o_b)
            else:
                acc_ref[pl.ds(cf, c), cols] = o_f
                acc_ref[pl.ds(cb, c), cols] = o_b
            new_states += [st_f, st_b]
        return tuple(new_states)

    zero = jnp.zeros((hd, hd), F32)
    half = n_chunks // 2
    states = lax.fori_loop(0, half, functools.partial(step, finalize=False), (zero,) * (2 * HGRN_HEADS_PER_STEP))
    lax.fori_loop(half, n_chunks, functools.partial(step, finalize=True), states)


def _hgrn(qvg, zz, lb, norm_w, n_heads):
    bsz, s, _ = qvg.shape
    c = HGRN_CHUNK
    assert s % (2 * c) == 0 and n_heads % HGRN_HEADS_PER_STEP == 0
    tri, masks, signs = _hgrn_masks(c, HGRN_LEVELS)
    hd = A_HEAD_DIM
    wide = HGRN_HEADS_PER_STEP * hd
    n_groups = n_heads // HGRN_HEADS_PER_STEP
    col = lambda off: pl.BlockSpec((1, s, wide), lambda b, h: (b, 0, off + h))
    const = lambda a: pl.BlockSpec(a.shape, lambda b, h: (0,) * a.ndim)
    return pl.pallas_call(
        functools.partial(_hgrn_kernel, n_chunks=s // c),
        grid=(bsz, n_groups),
        in_specs=[
            col(0), col(n_groups), col(2 * n_groups), col(0), col(n_groups),
            pl.BlockSpec((1, wide), lambda b, h: (0, h)),
            pl.BlockSpec((1, hd), lambda b, h: (0, 0)),
            const(tri), const(masks), const(signs),
        ],
        out_specs=pl.BlockSpec((1, s, wide), lambda b, h: (b, 0, h)),
        out_shape=jax.ShapeDtypeStruct((bsz, s, n_heads * hd), BF16),
        scratch_shapes=[pltpu.VMEM((s, wide), F32)],
        compiler_params=_cparams("parallel", "parallel"),
        name="hgrn2",
    )(qvg, qvg, qvg, zz, zz, lb.reshape(1, -1), norm_w.reshape(1, hd), jnp.asarray(tri), jnp.asarray(masks),
      jnp.asarray(signs))


def _dilated_kernel(q_ref, k_ref, v_ref, o_ref, qf_ref, kf_ref, vf_ref, oc_ref, lc_ref, *, seq):
    qf_ref[...] = q_ref[0].astype(F32)
    kf_ref[...] = k_ref[0].astype(F32)
    vf_ref[...] = v_ref[0].astype(F32)
    n_cfg = len(B_CONFIGS)
    for ci, (_, dil) in enumerate(B_CONFIGS):
        length = seq // dil
        tq = min(LANES, length)
        win = min(length, tq + 2 * BAND_RADIUS)
        head0 = lax.broadcasted_iota(jnp.int32, (tq, LANES), 1) < B_HEAD_DIM
        rel = lax.broadcasted_iota(jnp.int32, (tq, win), 1) - lax.broadcasted_iota(jnp.int32, (tq, win), 0)

        def rows(first, size, dil=dil):
            return pl.ds(first, size) if dil == 1 else pl.ds(first, size, stride=dil)

        def block(t, carry, ci=ci, dil=dil, length=length, tq=tq, win=win, head0=head0, rel=rel, rows=rows):
            res = t % dil
            q0 = (t // dil) * tq
            start = jnp.clip(q0 - BAND_RADIUS, 0, length - win)
            valid = jnp.abs(rel + (start - q0)) <= BAND_RADIUS
            q_rows = rows(q0 * dil + res, tq)
            k_rows = rows(start * dil + res, win)
            q = qf_ref[q_rows, :].astype(BF16)
            kw = kf_ref[k_rows, :].astype(BF16)
            vw = vf_ref[k_rows, :].astype(BF16)

            def one_head(mask):
                s = _dot_nt(jnp.where(mask, q, jnp.zeros_like(q)), kw)
                s = jnp.where(valid, s, MASK_VALUE)
                m = jnp.max(s, axis=-1, keepdims=True)
                p = jnp.exp2(s - m)
                l = jnp.sum(p, axis=-1, keepdims=True)
                return _dot(p.astype(BF16), vw) / l, m + jnp.log(l) * LOG2_E

            oa, la = one_head(head0)
            ob, lb = one_head(jnp.logical_not(head0))
            oc_ref[ci, q_rows, :] = jnp.where(head0, oa, ob)
            lc_ref[ci, q_rows, :] = jnp.where(head0, la, lb)
            return carry

        lax.fori_loop(0, dil * (length // tq), block, 0, unroll=8)

    tmix = min(256, seq)

    def mix(i, carry):
        r0 = pl.multiple_of(i * tmix, tmix)
        lses = [lc_ref[c, pl.ds(r0, tmix), :] for c in range(n_cfg)]
        top = lses[0]
        for l in lses[1:]:
            top = jnp.maximum(top, l)
        num = jnp.zeros((tmix, LANES), F32)
        den = jnp.zeros((tmix, LANES), F32)
        for c in range(n_cfg):
            w = jnp.exp2(lses[c] - top)
            num = num + w * oc_ref[c, pl.ds(r0, tmix), :]
            den = den + w
        o_ref[0, pl.ds(r0, tmix), :] = (num / den).astype(o_ref.dtype)
        return carry

    lax.fori_loop(0, seq // tmix, mix, 0)


def _dilated_attention(qk, v):
    bsz, s, w = v.shape
    n_pairs = w // LANES
    n_cfg = len(B_CONFIGS)
    return pl.pallas_call(
        functools.partial(_dilated_kernel, seq=s),
        grid=(bsz, n_pairs),
        in_specs=[
            pl.BlockSpec((1, s, LANES), lambda b, h: (b, 0, h)),
            pl.BlockSpec((1, s, LANES), lambda b, h: (b, 0, n_pairs + h)),
            pl.BlockSpec((1, s, LANES), lambda b, h: (b, 0, h)),
        ],
        out_specs=pl.BlockSpec((1, s, LANES), lambda b, h: (b, 0, h)),
        out_shape=jax.ShapeDtypeStruct((bsz, s, w), BF16),
        scratch_shapes=[pltpu.VMEM((s, LANES), F32)] * 3 + [pltpu.VMEM((n_cfg, s, LANES), F32)] * 2,
        compiler_params=_cparams("parallel", "parallel"),
        name="dilated_attn",
    )(qk, qk, v)


def _diff_kernel(q_ref, k_ref, v_ref, lam_ref, sub_ref, o_ref, *, tk, lambda_init):
    q = q_ref[0]
    tq = q.shape[0]
    s_len = k_ref.shape[1]
    lane = lax.broadcasted_iota(jnp.int32, q.shape, 1)
    zero = jnp.zeros_like(q)
    qs = (jnp.where(lane < C_HEAD_DIM, q, zero), jnp.where(lane >= C_HEAD_DIM, q, zero))
    tiles = [(t * LANES, (t + 1) * LANES) for t in range(tk // LANES)]
    m = [jnp.full((tq, 1), -jnp.inf, F32)] * 2
    l = [jnp.zeros((tq, 1), F32)] * 2
    acc = [jnp.zeros((tq, LANES), F32)] * 2
    for c in range(s_len // tk):
        lo, hi = c * tk, (c + 1) * tk
        for h in range(2):
            s = _dot_nt(qs[h], k_ref[0, lo:hi, :])
            m_tile = s[:, 0:LANES]
            for a, b in tiles[1:]:
                m_tile = jnp.maximum(m_tile, s[:, a:b])
            m_new = jnp.maximum(m[h], jnp.max(m_tile, axis=-1, keepdims=True))
            alpha = jnp.exp2(m[h] - m_new)
            p = jnp.exp2(s - m_new)
            l_tile = p[:, 0:LANES]
            for a, b in tiles[1:]:
                l_tile = l_tile + p[:, a:b]
            l[h] = alpha * l[h] + jnp.sum(l_tile, axis=-1, keepdims=True)
            acc[h] = alpha * acc[h] + _dot(p.astype(BF16), v_ref[0, lo:hi, :])
            m[h] = m_new
    outs = (acc[0] / l[0], acc[1] / l[1])
    lp = lam_ref[...]
    lam = (jnp.exp(jnp.sum(lp[0:1] * lp[1:2], axis=-1, keepdims=True))
           - jnp.exp(jnp.sum(lp[2:3] * lp[3:4], axis=-1, keepdims=True)) + lambda_init)
    o = outs[0] - lam * outs[1]
    ms_o = jnp.mean(o * o, axis=-1, keepdims=True)
    o_ref[0] = (o * lax.rsqrt(ms_o + RMS_EPS) * sub_ref[...] * (1.0 - lambda_init)).astype(o_ref.dtype)


def _diff_attention(qk, v, lam_params, subln_w, lambda_init):
    bsz, s, w = v.shape
    n_heads = w // LANES
    tq = _tile(s, (1024, 512, 256, 128))
    tk = _tile(s, (2048, 1024, 512, 256, 128))
    return pl.pallas_call(
        functools.partial(_diff_kernel, tk=tk, lambda_init=lambda_init),
        grid=(bsz, n_heads, s // tq),
        in_specs=[
            pl.BlockSpec((1, tq, LANES), lambda b, h, i: (b, i, h)),
            pl.BlockSpec((1, s, LANES), lambda b, h, i: (b, 0, n_heads + h)),
            pl.BlockSpec((1, s, LANES), lambda b, h, i: (b, 0, h)),
            pl.BlockSpec(lam_params.shape, lambda b, h, i: (0, 0)),
            pl.BlockSpec((1, LANES), lambda b, h, i: (0, 0)),
        ],
        out_specs=pl.BlockSpec((1, tq, LANES), lambda b, h, i: (b, i, h)),
        out_shape=jax.ShapeDtypeStruct((bsz, s, w), BF16),
        compiler_params=_cparams("parallel", "parallel", "arbitrary"),
        name="diff_attn",
    )(qk, qk, v, lam_params, subln_w.reshape(1, LANES))


def _rope_tables(seq, width):
    half = B_HEAD_DIM // 2
    inv = ROPE_THETA ** (-jnp.arange(0, B_HEAD_DIM, 2, dtype=F32) / B_HEAD_DIM)
    ang = jnp.arange(seq, dtype=F32)[:, None] * inv[None, :]
    cos, sin = jnp.cos(ang), jnp.sin(ang)
    reps = width // B_HEAD_DIM
    assert half * 2 == B_HEAD_DIM
    return jnp.tile(jnp.concatenate([cos, cos], axis=1), (1, reps)), jnp.tile(jnp.concatenate([-sin, sin], axis=1), (1, reps))


def _even_layer(x_f, x_b, bsz, seq, w_in, lb, norm_w, w_out, ln1, w1, w3, w2, layer, ln2, rope, alpha):
    d = x_f.shape[1]
    aw = d // 2
    n_heads_a = aw // A_HEAD_DIM
    w_in = w_in.astype(BF16)
    cols = lambda a, b: w_in[:, a * aw:b * aw]
    qvg = _proj(x_b, jnp.concatenate([cols(0, 1), cols(3, 5)], axis=1), BF16)
    zz = _proj(x_b, cols(1, 3), F32)
    cos_t, sin_t = rope
    scale_row = jnp.concatenate([jnp.full((1, aw), LOG2_E * B_HEAD_DIM ** -0.5, F32), jnp.ones((1, aw), F32)], axis=1)
    qk = _proj_rope(x_b, cols(5, 7), cos_t, sin_t, scale_row, seq)
    vb = _proj(x_b, cols(7, 8), BF16)
    oa = _hgrn(qvg.reshape(bsz, seq, -1), zz.reshape(bsz, seq, -1), lb, norm_w, n_heads_a)
    ob = _dilated_attention(qk.reshape(bsz, seq, 2 * aw), vb.reshape(bsz, seq, aw))
    w_out = w_out.astype(BF16)
    x_f, x_b = _out_ln([oa.reshape(bsz * seq, aw), ob.reshape(bsz * seq, aw)], [w_out[:aw], w_out[aw:]], x_f,
                       ln1[0], ln1[1], alpha)
    return _ffn_ln(x_b, x_f, w1[layer].astype(BF16), w3[layer].astype(BF16), w2[layer].astype(BF16), ln2[0], ln2[1],
                   alpha)


def _moe_dispatch(idx, n_tokens, tm):
    e_flat = idx[:, :2].reshape(-1)
    onehot = (e_flat[None, :] == jnp.arange(N_EXPERTS, dtype=jnp.int32)[:, None]).astype(jnp.int32)
    counts = jnp.sum(onehot, axis=1)
    tiles = (counts + tm - 1) // tm
    tile_end = jnp.cumsum(tiles)
    group_start = (tile_end - tiles) * tm
    dest = jnp.sum((jnp.cumsum(onehot, axis=1) - 1 + group_start[:, None]) * onehot, axis=0)
    n_tiles = (2 * n_tokens) // tm + N_EXPERTS
    src_tok = jnp.zeros((n_tiles * tm,), jnp.int32).at[dest].set(jnp.arange(2 * n_tokens, dtype=jnp.int32) // 2)
    tile_ids = jnp.arange(n_tiles, dtype=jnp.int32)
    tile_expert = jnp.minimum(jnp.sum((tile_ids[:, None] >= tile_end[None, :]).astype(jnp.int32), axis=1),
                              N_EXPERTS - 1)
    return src_tok, dest.reshape(n_tokens, 2), tile_expert, tile_end[-1:].astype(jnp.int32)


def _odd_layer(x_f, x_b, bsz, seq, w_in, lam_params, subln_w, w_out, ln1, router, w1, w3, w2, layer, ln2, rope,
               alpha, lambda_init):
    d = x_f.shape[1]
    n_tok = bsz * seq
    w_in = w_in.astype(BF16)
    cos_t, sin_t = rope
    scale_row = jnp.concatenate([jnp.full((1, d), LOG2_E * C_HEAD_DIM ** -0.5, F32), jnp.ones((1, d), F32)], axis=1)
    qk = _proj_rope(x_b, w_in[:, :2 * d], cos_t, sin_t, scale_row, seq)
    v = _proj(x_b, w_in[:, 2 * d:], BF16)
    o = _diff_attention(qk.reshape(bsz, seq, 2 * d), v.reshape(bsz, seq, d), lam_params.astype(F32), subln_w,
                        lambda_init)
    x_f, x_b = _out_ln([o.reshape(n_tok, d)], [w_out.astype(BF16)], x_f, ln1[0], ln1[1], alpha)
    router_padded = jnp.pad(router.astype(F32), ((0, 0), (0, LANES - N_EXPERTS)))
    gates, idx = _router(x_f, router_padded)
    tm = _tile(n_tok, (1024, 512, 256))
    src_tok, pos, tile_expert, n_active = _moe_dispatch(idx, n_tok, tm)
    n_tiles = src_tok.shape[0] // tm
    y = None
    for lo, hi in ((0, n_tiles // 2), (n_tiles // 2, n_tiles)):
        x_rows = jnp.take(x_b, src_tok[lo * tm:hi * tm], axis=0)
        n_act = jnp.clip(n_active - lo, 0, hi - lo)
        y = _moe_ffn(x_rows, tile_expert[lo:hi], n_act, w1, w3, w2, layer, tm, lo, n_tiles * tm, y)
    y1 = jnp.take(y, pos[:, 0], axis=0)
    y2 = jnp.take(y, pos[:, 1], axis=0)
    return _moe_combine(x_f, y1, y2, gates, ln2[0], ln2[1], alpha)


def kernel(x, ev_w_in, ev_lb_logits, ev_hgrn_norm, ev_w_out, ev_ln1_g, ev_ln1_b, ev_w1, ev_w3, ev_w2, ev_ln2_g,
           ev_ln2_b, od_w_in, od_lambda, od_subln, od_w_out, od_ln1_g, od_ln1_b, od_router, od_w1, od_w3, od_w2,
           od_ln2_g, od_ln2_b):
    bsz, seq, d = x.shape
    depth = ev_w_in.shape[0] + od_w_in.shape[0]
    alpha = (2 * depth) ** 0.25
    rope = _rope_tables(seq, 512)
    lb_soft = jax.nn.softmax(ev_lb_logits.astype(F32), axis=0)
    lower_bounds = jnp.cumsum(lb_soft, axis=0) - lb_soft[0]
    x_f = x.reshape(bsz * seq, d).astype(F32)
    x_b = x_f.astype(BF16)
    for layer in range(depth):
        j = layer // 2
        if layer % 2 == 0:
            x_f, x_b = _even_layer(x_f, x_b, bsz, seq, ev_w_in[j], lower_bounds[j], ev_hgrn_norm[j], ev_w_out[j],
                                   (ev_ln1_g[j], ev_ln1_b[j]), ev_w1, ev_w3, ev_w2, j,
                                   (ev_ln2_g[j], ev_ln2_b[j]), rope, alpha)
        else:
            lambda_init = 0.8 - 0.6 * math.exp(-0.3 * layer)
            x_f, x_b = _odd_layer(x_f, x_b, bsz, seq, od_w_in[j], od_lambda[j], od_subln[j], od_w_out[j],
                                  (od_ln1_g[j], od_ln1_b[j]), od_router[j], od_w1, od_w3, od_w2, j,
                                  (od_ln2_g[j], od_ln2_b[j]), rope, alpha, lambda_init)
    return x_f.reshape(bsz, seq, d).astype(x.dtype)
```

```python
import functools
import math

import numpy as np
import jax
import jax.numpy as jnp
from jax import lax
from jax.experimental import pallas as pl
from jax.experimental.pallas import tpu as pltpu

F32 = jnp.float32
BF16 = jnp.bfloat16

A_HEAD_DIM = 128
B_HEAD_DIM = 64
B_CONFIGS = ((128, 1), (512, 4), (2048, 16))
BAND_RADIUS = 64
C_HEAD_DIM = 64
N_EXPERTS = 8
ROPE_THETA = 10000.0
LN_EPS = 1e-5
RMS_EPS = 1e-5
MASK_VALUE = -1e30
MIN_FORGET = 1e-30
LOG2_E = math.log2(math.e)

LANES = 128
SUBLANES = 8
VMEM_LIMIT_BYTES = 56 * 1024 * 1024

ROW_CHAINS = 4

HGRN_CHUNK = 128
HGRN_LEVELS = 7
HGRN_HEADS_PER_STEP = 2


def _cparams(*sem):
    return pltpu.CompilerParams(dimension_semantics=sem, vmem_limit_bytes=VMEM_LIMIT_BYTES)


def _tile(n, prefs):
    for p in prefs:
        if n % p == 0:
            return p
    return n


def _dot(a, b):
    return jnp.dot(a, b, preferred_element_type=F32)


def _dot_nt(a, b):
    return lax.dot_general(a, b, (((1,), (1,)), ((), ())), preferred_element_type=F32)


def _dot_tn(a, b):
    return lax.dot_general(a, b, (((0,), (0,)), ((), ())), preferred_element_type=F32)


def _sigmoid(x):
    return 1.0 / (1.0 + jnp.exp(-x))


def _layer_norm(y, g, b):
    mu = jnp.mean(y, axis=-1, keepdims=True)
    d = y - mu
    var = jnp.mean(d * d, axis=-1, keepdims=True)
    return d * lax.rsqrt(var + LN_EPS) * g + b


def _proj_kernel(x_ref, w_ref, o_ref):
    o_ref[...] = _dot(x_ref[...], w_ref[...]).astype(o_ref.dtype)


def _proj(x, w, out_dtype):
    m, k = x.shape
    n = w.shape[1]
    tm = _tile(m, (1024, 512, 256))
    tn = n if n <= 1536 else _tile(n, (1024, 512, 256, 128))
    return pl.pallas_call(
        _proj_kernel,
        grid=(m // tm, n // tn),
        in_specs=[pl.BlockSpec((tm, k), lambda i, j: (i, 0)), pl.BlockSpec((k, tn), lambda i, j: (0, j))],
        out_specs=pl.BlockSpec((tm, tn), lambda i, j: (i, j)),
        out_shape=jax.ShapeDtypeStruct((m, n), out_dtype),
        compiler_params=_cparams("parallel", "arbitrary"),
        name="proj",
    )(x, w)


def _proj_rope_kernel(x_ref, w_ref, cos_ref, sin_ref, scale_ref, o_ref):
    tm, tn = o_ref.shape
    sub = tm // ROW_CHAINS
    w = w_ref[...]
    lane = lax.broadcasted_iota(jnp.int32, (sub, tn), 1)
    first_half = (lane % B_HEAD_DIM) < (B_HEAD_DIM // 2)
    reps = tn // cos_ref.shape[1]
    for k in range(ROW_CHAINS):
        rows = pl.ds(k * sub, sub)
        acc = _dot(x_ref[rows, :], w)
        partner = jnp.where(first_half, pltpu.roll(acc, tn - B_HEAD_DIM // 2, 1), pltpu.roll(acc, B_HEAD_DIM // 2, 1))
        cos = jnp.tile(cos_ref[rows, :], (1, reps))
        sin = jnp.tile(sin_ref[rows, :], (1, reps))
        o_ref[rows, :] = ((acc * cos + partner * sin) * scale_ref[...]).astype(o_ref.dtype)


def _proj_rope(x, w, cos_t, sin_t, scale_row, seq):
    m, k = x.shape
    n = w.shape[1]
    tm = _tile(seq, (1024, 512, 256))
    tn = _tile(n, (1024, 512, 256, 128))
    nsb = seq // tm
    return pl.pallas_call(
        _proj_rope_kernel,
        grid=(m // tm, n // tn),
        in_specs=[
            pl.BlockSpec((tm, k), lambda i, j: (i, 0)),
            pl.BlockSpec((k, tn), lambda i, j: (0, j)),
            pl.BlockSpec((tm, LANES), lambda i, j: (i % nsb, 0)),
            pl.BlockSpec((tm, LANES), lambda i, j: (i % nsb, 0)),
            pl.BlockSpec((1, tn), lambda i, j: (0, j)),
        ],
        out_specs=pl.BlockSpec((tm, tn), lambda i, j: (i, j)),
        out_shape=jax.ShapeDtypeStruct((m, n), BF16),
        compiler_params=_cparams("parallel", "arbitrary"),
        name="proj_rope",
    )(x, w, cos_t, sin_t, scale_row)


def _out_ln_kernel(*refs, n_in, alpha):
    xs = refs[:n_in]
    ws = refs[n_in:2 * n_in]
    resid_ref, g_ref, b_ref, of_ref, ob_ref = refs[2 * n_in:]
    sub = of_ref.shape[0] // ROW_CHAINS
    for k in range(ROW_CHAINS):
        rows = pl.ds(k * sub, sub)
        acc = _dot(xs[0][rows, :], ws[0][...])
        for x_ref, w_ref in zip(xs[1:], ws[1:]):
            acc = acc + _dot(x_ref[rows, :], w_ref[...])
        z = _layer_norm(alpha * resid_ref[rows, :] + acc, g_ref[...], b_ref[...])
        of_ref[rows, :] = z
        ob_ref[rows, :] = z.astype(BF16)


def _out_ln(xs, ws, resid, g, b, alpha):
    m, d = resid.shape
    tm = _tile(m, (1024, 512, 256))
    n_in = len(xs)
    in_specs = [pl.BlockSpec((tm, x.shape[1]), lambda i: (i, 0)) for x in xs]
    in_specs += [pl.BlockSpec(w.shape, lambda i: (0, 0)) for w in ws]
    in_specs += [pl.BlockSpec((tm, d), lambda i: (i, 0)), pl.BlockSpec((1, d), lambda i: (0, 0)),
                 pl.BlockSpec((1, d), lambda i: (0, 0))]
    return pl.pallas_call(
        functools.partial(_out_ln_kernel, n_in=n_in, alpha=alpha),
        grid=(m // tm,),
        in_specs=in_specs,
        out_specs=[pl.BlockSpec((tm, d), lambda i: (i, 0)), pl.BlockSpec((tm, d), lambda i: (i, 0))],
        out_shape=[jax.ShapeDtypeStruct((m, d), F32), jax.ShapeDtypeStruct((m, d), BF16)],
        compiler_params=_cparams("parallel"),
        name="out_ln",
    )(*xs, *ws, resid, g.reshape(1, d), b.reshape(1, d))


def _swiglu_accumulate(x_ref, w1, w3, w2, acc_ref):
    w1 = w1.astype(BF16)
    w3 = w3.astype(BF16)
    w2 = w2.astype(BF16)
    half = x_ref.shape[0] // 2
    rows = (pl.ds(0, half), pl.ds(half, half))
    pre = [(_dot(x_ref[r, :], w1), _dot(x_ref[r, :], w3)) for r in rows]
    for r, (h1, h3) in zip(rows, pre):
        h = (h1 * _sigmoid(h1)) * h3
        acc_ref[r, :] += _dot(h.astype(BF16), w2)


def _ffn_ln_kernel(x_ref, w1_ref, w3_ref, w2_ref, resid_ref, g_ref, b_ref, of_ref, ob_ref, acc_ref, *, alpha):
    j = pl.program_id(1)

    @pl.when(j == 0)
    def _():
        acc_ref[...] = jnp.zeros_like(acc_ref)

    _swiglu_accumulate(x_ref, w1_ref[...], w3_ref[...], w2_ref[...], acc_ref)

    @pl.when(j == pl.num_programs(1) - 1)
    def _():
        z = _layer_norm(alpha * resid_ref[...] + acc_ref[...], g_ref[...], b_ref[...])
        of_ref[...] = z
        ob_ref[...] = z.astype(BF16)


def _ffn_ln(x_bf, resid, w1, w3, w2, g, b, alpha):
    m, d = resid.shape
    ff = w1.shape[1]
    tm = _tile(m, (512, 256))
    tf = _tile(ff, (1408, 256, 128))
    return pl.pallas_call(
        functools.partial(_ffn_ln_kernel, alpha=alpha),
        grid=(m // tm, ff // tf),
        in_specs=[
            pl.BlockSpec((tm, d), lambda i, j: (i, 0)),
            pl.BlockSpec((d, tf), lambda i, j: (0, j)),
            pl.BlockSpec((d, tf), lambda i, j: (0, j)),
            pl.BlockSpec((tf, d), lambda i, j: (j, 0)),
            pl.BlockSpec((tm, d), lambda i, j: (i, 0)),
            pl.BlockSpec((1, d), lambda i, j: (0, 0)),
            pl.BlockSpec((1, d), lambda i, j: (0, 0)),
        ],
        out_specs=[pl.BlockSpec((tm, d), lambda i, j: (i, 0)), pl.BlockSpec((tm, d), lambda i, j: (i, 0))],
        out_shape=[jax.ShapeDtypeStruct((m, d), F32), jax.ShapeDtypeStruct((m, d), BF16)],
        scratch_shapes=[pltpu.VMEM((tm, d), F32)],
        compiler_params=_cparams("parallel", "arbitrary"),
        name="ffn_ln",
    )(x_bf, w1, w3, w2, resid, g.reshape(1, d), b.reshape(1, d))


def _moe_ffn_kernel(te_ref, na_ref, x_ref, w1_ref, w3_ref, w2_ref, *rest):
    o_ref, acc_ref = rest[-2:]
    i = pl.program_id(0)
    j = pl.program_id(1)
    active = i < na_ref[0]

    @pl.when(active & (j == 0))
    def _():
        acc_ref[...] = jnp.zeros_like(acc_ref)

    @pl.when(active)
    def _():
        _swiglu_accumulate(x_ref, w1_ref[0, 0], w3_ref[0, 0], w2_ref[0, 0], acc_ref)

    @pl.when(active & (j == pl.num_programs(1) - 1))
    def _():
        o_ref[...] = acc_ref[...].astype(o_ref.dtype)


def _moe_ffn(x_sorted, tile_expert, n_active, w1, w3, w2, layer, tm, tile_offset, total_rows, y_prev=None):
    p, d = x_sorted.shape
    ff = w1.shape[3]
    tf = _tile(ff, (256, 128))
    nf = ff // tf

    def row(i, na):
        return jnp.maximum(jnp.minimum(i, na[0] - 1), 0)

    def col(i, j, na):
        return jnp.where(i < na[0], j, nf - 1)

    operands = [tile_expert, n_active, x_sorted, w1, w3, w2]
    in_specs = [
        pl.BlockSpec((tm, d), lambda i, j, te, na: (row(i, na), 0)),
        pl.BlockSpec((1, 1, d, tf), lambda i, j, te, na: (layer, te[row(i, na)], 0, col(i, j, na))),
        pl.BlockSpec((1, 1, d, tf), lambda i, j, te, na: (layer, te[row(i, na)], 0, col(i, j, na))),
        pl.BlockSpec((1, 1, tf, d), lambda i, j, te, na: (layer, te[row(i, na)], col(i, j, na), 0)),
    ]
    aliases = {}
    if y_prev is not None:
        in_specs.append(pl.BlockSpec(memory_space=pl.ANY))
        aliases = {len(operands): 0}
        operands.append(y_prev)
    grid_spec = pltpu.PrefetchScalarGridSpec(
        num_scalar_prefetch=2,
        grid=(p // tm, nf),
        in_specs=in_specs,
        out_specs=pl.BlockSpec((tm, d), lambda i, j, te, na: (tile_offset + row(i, na), 0)),
        scratch_shapes=[pltpu.VMEM((tm, d), F32)],
    )
    return pl.pallas_call(
        _moe_ffn_kernel,
        grid_spec=grid_spec,
        out_shape=jax.ShapeDtypeStruct((total_rows, d), BF16),
        input_output_aliases=aliases,
        compiler_params=_cparams("arbitrary", "arbitrary"),
        name="moe_ffn",
    )(*operands)


def _router_kernel(x_ref, r_ref, gate_ref, idx_ref):
    logits = jnp.dot(x_ref[...], r_ref[...], precision=lax.Precision.HIGHEST, preferred_element_type=F32)
    lane = lax.broadcasted_iota(jnp.int32, logits.shape, 1)
    neg = jnp.float32(-jnp.inf)
    logits = jnp.where(lane < N_EXPERTS, logits, neg)
    v1 = jnp.max(logits, axis=-1, keepdims=True)
    i1 = jnp.min(jnp.where(logits == v1, lane, LANES), axis=-1, keepdims=True)
    rest = jnp.where(lane == i1, neg, logits)
    v2 = jnp.max(rest, axis=-1, keepdims=True)
    i2 = jnp.min(jnp.where(rest == v2, lane, LANES), axis=-1, keepdims=True)
    e = jnp.exp(v2 - v1)
    g1 = 1.0 / (1.0 + e)
    g2 = e / (1.0 + e)
    gate_ref[...] = jnp.where(lane == 0, g1, jnp.where(lane == 1, g2, 0.0))
    idx_ref[...] = jnp.where(lane == 0, i1, jnp.where(lane == 1, i2, 0))


def _router(x_f32, router_padded):
    m, d = x_f32.shape
    tm = _tile(m, (1024, 512, 256))
    return pl.pallas_call(
        _router_kernel,
        grid=(m // tm,),
        in_specs=[pl.BlockSpec((tm, d), lambda i: (i, 0)), pl.BlockSpec((d, LANES), lambda i: (0, 0))],
        out_specs=[pl.BlockSpec((tm, LANES), lambda i: (i, 0)), pl.BlockSpec((tm, LANES), lambda i: (i, 0))],
        out_shape=[jax.ShapeDtypeStruct((m, LANES), F32), jax.ShapeDtypeStruct((m, LANES), jnp.int32)],
        compiler_params=_cparams("parallel"),
        name="router",
    )(x_f32, router_padded)


def _moe_combine_kernel(resid_ref, y1_ref, y2_ref, gate_ref, g_ref, b_ref, of_ref, ob_ref, *, alpha):
    gates = gate_ref[...]
    y = gates[:, 0:1] * y1_ref[...].astype(F32) + gates[:, 1:2] * y2_ref[...].astype(F32)
    z = _layer_norm(alpha * resid_ref[...] + y, g_ref[...], b_ref[...])
    of_ref[...] = z
    ob_ref[...] = z.astype(BF16)


def _moe_combine(resid, y1, y2, gates, g, b, alpha):
    m, d = resid.shape
    tm = _tile(m, (512, 256))
    row = lambda i: (i, 0)
    fixed = lambda i: (0, 0)
    return pl.pallas_call(
        functools.partial(_moe_combine_kernel, alpha=alpha),
        grid=(m // tm,),
        in_specs=[pl.BlockSpec((tm, d), row), pl.BlockSpec((tm, d), row), pl.BlockSpec((tm, d), row),
                  pl.BlockSpec((tm, LANES), row), pl.BlockSpec((1, d), fixed), pl.BlockSpec((1, d), fixed)],
        out_specs=[pl.BlockSpec((tm, d), row), pl.BlockSpec((tm, d), row)],
        out_shape=[jax.ShapeDtypeStruct((m, d), F32), jax.ShapeDtypeStruct((m, d), BF16)],
        compiler_params=_cparams("parallel"),
        name="moe_combine",
    )(resid, y1, y2, gates, g.reshape(1, d), b.reshape(1, d))


def _hgrn_masks(c, levels):
    t = np.arange(c)[:, None]
    s = np.arange(c)[None, :]
    tri = np.stack([(s <= t), (s >= t)]).astype(np.float32)
    fwd, bwd = [], []
    for l in range(levels):
        same = (t >> (l + 1)) == (s >> (l + 1))
        t_up = ((t >> l) & 1) == 1
        s_up = ((s >> l) & 1) == 1
        fwd.append(same & t_up & ~s_up)
        bwd.append(same & ~t_up & s_up)
    fwd.append(t == s)
    bwd.append(t == s)
    up = np.stack([np.broadcast_to(((t >> l) & 1) == 1, (c, A_HEAD_DIM)) for l in range(levels)])
    sign = np.stack([np.where(up, 1.0, -1.0), np.where(up, -1.0, 1.0)]).astype(np.float32)
    return tri, np.stack([np.stack(fwd), np.stack(bwd)]).astype(np.float32), sign


def _segment_reference(x, level, forward):
    c, w = x.shape
    half = 1 << level
    seg = 2 * half
    idx = half - 1 if forward else half
    if seg >= SUBLANES:
        xr = x.reshape(c // seg, seg, w)
        return jnp.broadcast_to(xr[:, idx:idx + 1, :], xr.shape).reshape(c, w)
    x3 = x.reshape(c // SUBLANES, SUBLANES, w)
    sub = lax.broadcasted_iota(jnp.int32, x3.shape, 1)
    r3 = jnp.broadcast_to(x3[:, idx:idx + 1, :], x3.shape)
    for j in range(1, SUBLANES // seg):
        row = j * seg + idx
        r3 = jnp.where(sub >= j * seg, jnp.broadcast_to(x3[:, row:row + 1, :], x3.shape), r3)
    return r3.reshape(c, w)


def _hgrn_chunk(q, k, v, log_f, state_t, tri, masks, signs, forward):
    c = q.shape[0]
    x = jnp.dot(tri, log_f, precision=lax.Precision.HIGHEST, preferred_element_type=F32)
    scores = masks[HGRN_LEVELS] * _dot_nt(q.astype(BF16), k.astype(BF16))
    for level in range(HGRN_LEVELS):
        ref = _segment_reference(x, level, forward)
        sign = signs[level]
        decay = jnp.exp(sign * (x - ref))
        z = (jnp.where(sign > 0, q, k) * decay).astype(BF16)
        scores = scores + masks[level] * _dot_nt(z, z)
    x_end = x[c - 1:c, :] if forward else x[0:1, :]
    q_dec = (q * jnp.exp(x)).astype(BF16)
    o = _dot(scores.astype(BF16), v.astype(BF16)) + _dot_nt(q_dec, state_t.astype(BF16))
    k_dec = (k * jnp.exp(x_end - x)).astype(BF16)
    new_state_t = state_t * jnp.exp(x_end) + _dot_tn(v.astype(BF16), k_dec)
    return o, new_state_t


def _hgrn_kernel(q_ref, v_ref, g_ref, zf_ref, zb_ref, lb_ref, nw_ref, tri_ref, msk_ref, sgn_ref, o_ref, acc_ref, *,
                 n_chunks):
    c = HGRN_CHUNK
    hd = A_HEAD_DIM

    def gates(z, lb):
        e = jnp.exp(-jnp.abs(z))
        inv = 1.0 / (1.0 + e)
        pos = z >= 0
        sig = jnp.where(pos, inv, e * inv)
        sig_neg = jnp.where(pos, e * inv, inv)
        f = lb + (1.0 - lb) * sig
        return jnp.log(jnp.maximum(f, MIN_FORGET)), (1.0 - lb) * sig_neg

    def load(ref, c0, h):
        return ref[0, pl.ds(c0, c), h * hd:(h + 1) * hd].astype(F32)

    def emit(c0, h, tot):
        ms = jnp.mean(tot * tot, axis=-1, keepdims=True)
        g = load(g_ref, c0, h)
        out = tot * lax.rsqrt(ms + RMS_EPS) * nw_ref[...] * (g * _sigmoid(g))
        o_ref[0, pl.ds(c0, c), h * hd:(h + 1) * hd] = out.astype(o_ref.dtype)

    def one_direction(h, c0, z_ref, state_t, direction):
        lb = lb_ref[:, h * hd:(h + 1) * hd]
        log_f, k = gates(load(z_ref, c0, h), lb)
        masks = [msk_ref[direction, l] for l in range(HGRN_LEVELS + 1)]
        signs = [sgn_ref[direction, l] for l in range(HGRN_LEVELS)]
        return _hgrn_chunk(load(q_ref, c0, h), k, load(v_ref, c0, h), log_f, state_t, tri_ref[direction], masks,
                           signs, direction == 0)

    def step(i, states, finalize):
        cf = pl.multiple_of(i * c, c)
        cb = pl.multiple_of((n_chunks - 1 - i) * c, c)
        new_states = []
        for h in range(HGRN_HEADS_PER_STEP):
            o_f, st_f = one_direction(h, cf, zf_ref, states[2 * h], 0)
            o_b, st_b = one_direction(h, cb, zb_ref, states[2 * h + 1], 1)
            cols = slice(h * hd, (h + 1) * hd)
            if finalize:
                emit(cf, h, acc_ref[pl.ds(cf, c), cols] + o_f)
                emit(cb, h, acc_ref[pl.ds(cb, c), cols] + o_b)
            else:
                acc_ref[pl.ds(cf, c), cols] = o_f
                acc_ref[pl.ds(cb, c), cols] = o_b
            new_states += [st_f, st_b]
        return tuple(new_states)

    zero = jnp.zeros((hd, hd), F32)
    half = n_chunks // 2
    states = lax.fori_loop(0, half, functools.partial(step, finalize=False), (zero,) * (2 * HGRN_HEADS_PER_STEP))
    lax.fori_loop(half, n_chunks, functools.partial(step, finalize=True), states)


def _hgrn(qvg, zz, lb, norm_w, n_heads):
    bsz, s, _ = qvg.shape
    c = HGRN_CHUNK
    assert s % (2 * c) == 0 and n_heads % HGRN_HEADS_PER_STEP == 0
    tri, masks, signs = _hgrn_masks(c, HGRN_LEVELS)
    hd = A_HEAD_DIM
    wide = HGRN_HEADS_PER_STEP * hd
    n_groups = n_heads // HGRN_HEADS_PER_STEP
    col = lambda off: pl.BlockSpec((1, s, wide), lambda b, h: (b, 0, off + h))
    const = lambda a: pl.BlockSpec(a.shape, lambda b, h: (0,) * a.ndim)
    return pl.pallas_call(
        functools.partial(_hgrn_kernel, n_chunks=s // c),
        grid=(bsz, n_groups),
        in_specs=[
            col(0), col(n_groups), col(2 * n_groups), col(0), col(n_groups),
            pl.BlockSpec((1, wide), lambda b, h: (0, h)),
            pl.BlockSpec((1, hd), lambda b, h: (0, 0)),
            const(tri), const(masks), const(signs),
        ],
        out_specs=pl.BlockSpec((1, s, wide), lambda b, h: (b, 0, h)),
        out_shape=jax.ShapeDtypeStruct((bsz, s, n_heads * hd), BF16),
        scratch_shapes=[pltpu.VMEM((s, wide), F32)],
        compiler_params=_cparams("parallel", "parallel"),
        name="hgrn2",
    )(qvg, qvg, qvg, zz, zz, lb.reshape(1, -1), norm_w.reshape(1, hd), jnp.asarray(tri), jnp.asarray(masks),
      jnp.asarray(signs))


def _dilated_kernel(q_ref, k_ref, v_ref, o_ref, qf_ref, kf_ref, vf_ref, oc_ref, lc_ref, *, seq):
    qf_ref[...] = q_ref[0].astype(F32)
    kf_ref[...] = k_ref[0].astype(F32)
    vf_ref[...] = v_ref[0].astype(F32)
    n_cfg = len(B_CONFIGS)
    for ci, (_, dil) in enumerate(B_CONFIGS):
        length = seq // dil
        tq = min(LANES, length)
        win = min(length, tq + 2 * BAND_RADIUS)
        head0 = lax.broadcasted_iota(jnp.int32, (tq, LANES), 1) < B_HEAD_DIM
        rel = lax.broadcasted_iota(jnp.int32, (tq, win), 1) - lax.broadcasted_iota(jnp.int32, (tq, win), 0)

        def rows(first, size, dil=dil):
            return pl.ds(first, size) if dil == 1 else pl.ds(first, size, stride=dil)

        def block(t, carry, ci=ci, dil=dil, length=length, tq=tq, win=win, head0=head0, rel=rel, rows=rows):
            res = t % dil
            q0 = (t // dil) * tq
            start = jnp.clip(q0 - BAND_RADIUS, 0, length - win)
            valid = jnp.abs(rel + (start - q0)) <= BAND_RADIUS
            q_rows = rows(q0 * dil + res, tq)
            k_rows = rows(start * dil + res, win)
            q = qf_ref[q_rows, :].astype(BF16)
            kw = kf_ref[k_rows, :].astype(BF16)
            vw = vf_ref[k_rows, :].astype(BF16)

            def one_head(mask):
                s = _dot_nt(jnp.where(mask, q, jnp.zeros_like(q)), kw)
                s = jnp.where(valid, s, MASK_VALUE)
                m = jnp.max(s, axis=-1, keepdims=True)
                p = jnp.exp2(s - m)
                l = jnp.sum(p, axis=-1, keepdims=True)
                return _dot(p.astype(BF16), vw) / l, m + jnp.log(l) * LOG2_E

            oa, la = one_head(head0)
            ob, lb = one_head(jnp.logical_not(head0))
            oc_ref[ci, q_rows, :] = jnp.where(head0, oa, ob)
            lc_ref[ci, q_rows, :] = jnp.where(head0, la, lb)
            return carry

        lax.fori_loop(0, dil * (length // tq), block, 0, unroll=8)

    tmix = min(256, seq)

    def mix(i, carry):
        r0 = pl.multiple_of(i * tmix, tmix)
        lses = [lc_ref[c, pl.ds(r0, tmix), :] for c in range(n_cfg)]
        top = lses[0]
        for l in lses[1:]:
            top = jnp.maximum(top, l)
        num = jnp.zeros((tmix, LANES), F32)
        den = jnp.zeros((tmix, LANES), F32)
        for c in range(n_cfg):
            w = jnp.exp2(lses[c] - top)
            num = num + w * oc_ref[c, pl.ds(r0, tmix), :]
            den = den + w
        o_ref[0, pl.ds(r0, tmix), :] = (num / den).astype(o_ref.dtype)
        return carry

    lax.fori_loop(0, seq // tmix, mix, 0)


def _dilated_attention(qk, v):
    bsz, s, w = v.shape
    n_pairs = w // LANES
    n_cfg = len(B_CONFIGS)
    return pl.pallas_call(
        functools.partial(_dilated_kernel, seq=s),
        grid=(bsz, n_pairs),
        in_specs=[
            pl.BlockSpec((1, s, LANES), lambda b, h: (b, 0, h)),
            pl.BlockSpec((1, s, LANES), lambda b, h: (b, 0, n_pairs + h)),
            pl.BlockSpec((1, s, LANES), lambda b, h: (b, 0, h)),
        ],
        out_specs=pl.BlockSpec((1, s, LANES), lambda b, h: (b, 0, h)),
        out_shape=jax.ShapeDtypeStruct((bsz, s, w), BF16),
        scratch_shapes=[pltpu.VMEM((s, LANES), F32)] * 3 + [pltpu.VMEM((n_cfg, s, LANES), F32)] * 2,
        compiler_params=_cparams("parallel", "parallel"),
        name="dilated_attn",
    )(qk, qk, v)


def _diff_kernel(q_ref, k_ref, v_ref, lam_ref, sub_ref, o_ref, *, tk, lambda_init):
    q = q_ref[0]
    tq = q.shape[0]
    s_len = k_ref.shape[1]
    lane = lax.broadcasted_iota(jnp.int32, q.shape, 1)
    zero = jnp.zeros_like(q)
    qs = (jnp.where(lane < C_HEAD_DIM, q, zero), jnp.where(lane >= C_HEAD_DIM, q, zero))
    tiles = [(t * LANES, (t + 1) * LANES) for t in range(tk // LANES)]
    m = [jnp.full((tq, 1), -jnp.inf, F32)] * 2
    l = [jnp.zeros((tq, 1), F32)] * 2
    acc = [jnp.zeros((tq, LANES), F32)] * 2
    for c in range(s_len // tk):
        lo, hi = c * tk, (c + 1) * tk
        for h in range(2):
            s = _dot_nt(qs[h], k_ref[0, lo:hi, :])
            m_tile = s[:, 0:LANES]
            for a, b in tiles[1:]:
                m_tile = jnp.maximum(m_tile, s[:, a:b])
            m_new = jnp.maximum(m[h], jnp.max(m_tile, axis=-1, keepdims=True))
            alpha = jnp.exp2(m[h] - m_new)
            p = jnp.exp2(s - m_new)
            l_tile = p[:, 0:LANES]
            for a, b in tiles[1:]:
                l_tile = l_tile + p[:, a:b]
            l[h] = alpha * l[h] + jnp.sum(l_tile, axis=-1, keepdims=True)
            acc[h] = alpha * acc[h] + _dot(p.astype(BF16), v_ref[0, lo:hi, :])
            m[h] = m_new
    outs = (acc[0] / l[0], acc[1] / l[1])
    lp = lam_ref[...]
    lam = (jnp.exp(jnp.sum(lp[0:1] * lp[1:2], axis=-1, keepdims=True))
           - jnp.exp(jnp.sum(lp[2:3] * lp[3:4], axis=-1, keepdims=True)) + lambda_init)
    o = outs[0] - lam * outs[1]
    ms_o = jnp.mean(o * o, axis=-1, keepdims=True)
    o_ref[0] = (o * lax.rsqrt(ms_o + RMS_EPS) * sub_ref[...] * (1.0 - lambda_init)).astype(o_ref.dtype)


def _diff_attention(qk, v, lam_params, subln_w, lambda_init):
    bsz, s, w = v.shape
    n_heads = w // LANES
    tq = _tile(s, (1024, 512, 256, 128))
    tk = _tile(s, (2048, 1024, 512, 256, 128))
    return pl.pallas_call(
        functools.partial(_diff_kernel, tk=tk, lambda_init=lambda_init),
        grid=(bsz, n_heads, s // tq),
        in_specs=[
            pl.BlockSpec((1, tq, LANES), lambda b, h, i: (b, i, h)),
            pl.BlockSpec((1, s, LANES), lambda b, h, i: (b, 0, n_heads + h)),
            pl.BlockSpec((1, s, LANES), lambda b, h, i: (b, 0, h)),
            pl.BlockSpec(lam_params.shape, lambda b, h, i: (0, 0)),
            pl.BlockSpec((1, LANES), lambda b, h, i: (0, 0)),
        ],
        out_specs=pl.BlockSpec((1, tq, LANES), lambda b, h, i: (b, i, h)),
        out_shape=jax.ShapeDtypeStruct((bsz, s, w), BF16),
        compiler_params=_cparams("parallel", "parallel", "arbitrary"),
        name="diff_attn",
    )(qk, qk, v, lam_params, subln_w.reshape(1, LANES))


def _rope_tables(seq, width):
    half = B_HEAD_DIM // 2
    inv = ROPE_THETA ** (-jnp.arange(0, B_HEAD_DIM, 2, dtype=F32) / B_HEAD_DIM)
    ang = jnp.arange(seq, dtype=F32)[:, None] * inv[None, :]
    cos, sin = jnp.cos(ang), jnp.sin(ang)
    reps = width // B_HEAD_DIM
    assert half * 2 == B_HEAD_DIM
    return jnp.tile(jnp.concatenate([cos, cos], axis=1), (1, reps)), jnp.tile(jnp.concatenate([-sin, sin], axis=1), (1, reps))


def _even_layer(x_f, x_b, bsz, seq, w_in, lb, norm_w, w_out, ln1, w1, w3, w2, layer, ln2, rope, alpha):
    d = x_f.shape[1]
    aw = d // 2
    n_heads_a = aw // A_HEAD_DIM
    w_in = w_in.astype(BF16)
    cols = lambda a, b: w_in[:, a * aw:b * aw]
    qvg = _proj(x_b, jnp.concatenate([cols(0, 1), cols(3, 5)], axis=1), BF16)
    zz = _proj(x_b, cols(1, 3), F32)
    cos_t, sin_t = rope
    scale_row = jnp.concatenate([jnp.full((1, aw), LOG2_E * B_HEAD_DIM ** -0.5, F32), jnp.ones((1, aw), F32)], axis=1)
    qk = _proj_rope(x_b, cols(5, 7), cos_t, sin_t, scale_row, seq)
    vb = _proj(x_b, cols(7, 8), BF16)
    oa = _hgrn(qvg.reshape(bsz, seq, -1), zz.reshape(bsz, seq, -1), lb, norm_w, n_heads_a)
    ob = _dilated_attention(qk.reshape(bsz, seq, 2 * aw), vb.reshape(bsz, seq, aw))
    w_out = w_out.astype(BF16)
    x_f, x_b = _out_ln([oa.reshape(bsz * seq, aw), ob.reshape(bsz * seq, aw)], [w_out[:aw], w_out[aw:]], x_f,
                       ln1[0], ln1[1], alpha)
    return _ffn_ln(x_b, x_f, w1[layer].astype(BF16), w3[layer].astype(BF16), w2[layer].astype(BF16), ln2[0], ln2[1],
                   alpha)


def _moe_dispatch(idx, n_tokens, tm):
    e_flat = idx[:, :2].reshape(-1)
    onehot = (e_flat[None, :] == jnp.arange(N_EXPERTS, dtype=jnp.int32)[:, None]).astype(jnp.int32)
    counts = jnp.sum(onehot, axis=1)
    tiles = (counts + tm - 1) // tm
    tile_end = jnp.cumsum(tiles)
    group_start = (tile_end - tiles) * tm
    dest = jnp.sum((jnp.cumsum(onehot, axis=1) - 1 + group_start[:, None]) * onehot, axis=0)
    n_tiles = (2 * n_tokens) // tm + N_EXPERTS
    src_tok = jnp.zeros((n_tiles * tm,), jnp.int32).at[dest].set(jnp.arange(2 * n_tokens, dtype=jnp.int32) // 2)
    tile_ids = jnp.arange(n_tiles, dtype=jnp.int32)
    tile_expert = jnp.minimum(jnp.sum((tile_ids[:, None] >= tile_end[None, :]).astype(jnp.int32), axis=1),
                              N_EXPERTS - 1)
    return src_tok, dest.reshape(n_tokens, 2), tile_expert, tile_end[-1:].astype(jnp.int32)


def _odd_layer(x_f, x_b, bsz, seq, w_in, lam_params, subln_w, w_out, ln1, router, w1, w3, w2, layer, ln2, rope,
               alpha, lambda_init):
    d = x_f.shape[1]
    n_tok = bsz * seq
    w_in = w_in.astype(BF16)
    cos_t, sin_t = rope
    scale_row = jnp.concatenate([jnp.full((1, d), LOG2_E * C_HEAD_DIM ** -0.5, F32), jnp.ones((1, d), F32)], axis=1)
    qk = _proj_rope(x_b, w_in[:, :2 * d], cos_t, sin_t, scale_row, seq)
    v = _proj(x_b, w_in[:, 2 * d:], BF16)
    o = _diff_attention(qk.reshape(bsz, seq, 2 * d), v.reshape(bsz, seq, d), lam_params.astype(F32), subln_w,
                        lambda_init)
    x_f, x_b = _out_ln([o.reshape(n_tok, d)], [w_out.astype(BF16)], x_f, ln1[0], ln1[1], alpha)
    router_padded = jnp.pad(router.astype(F32), ((0, 0), (0, LANES - N_EXPERTS)))
    gates, idx = _router(x_f, router_padded)
    tm = _tile(n_tok, (1024, 512, 256))
    src_tok, pos, tile_expert, n_active = _moe_dispatch(idx, n_tok, tm)
    n_tiles = src_tok.shape[0] // tm
    y = None
    for lo, hi in ((0, n_tiles // 2), (n_tiles // 2, n_tiles)):
        x_rows = jnp.take(x_b, src_tok[lo * tm:hi * tm], axis=0, mode="clip")
        n_act = jnp.clip(n_active - lo, 0, hi - lo)
        y = _moe_ffn(x_rows, tile_expert[lo:hi], n_act, w1, w3, w2, layer, tm, lo, n_tiles * tm, y)
    y1 = jnp.take(y, pos[:, 0], axis=0, mode="clip")
    y2 = jnp.take(y, pos[:, 1], axis=0, mode="clip")
    return _moe_combine(x_f, y1, y2, gates, ln2[0], ln2[1], alpha)


def kernel(x, ev_w_in, ev_lb_logits, ev_hgrn_norm, ev_w_out, ev_ln1_g, ev_ln1_b, ev_w1, ev_w3, ev_w2, ev_ln2_g,
           ev_ln2_b, od_w_in, od_lambda, od_subln, od_w_out, od_ln1_g, od_ln1_b, od_router, od_w1, od_w3, od_w2,
           od_ln2_g, od_ln2_b):
    bsz, seq, d = x.shape
    depth = ev_w_in.shape[0] + od_w_in.shape[0]
    alpha = (2 * depth) ** 0.25
    rope = _rope_tables(seq, LANES)
    lb_soft = jax.nn.softmax(ev_lb_logits.astype(F32), axis=0)
    lower_bounds = jnp.cumsum(lb_soft, axis=0) - lb_soft[0]
    x_f = x.reshape(bsz * seq, d).astype(F32)
    x_b = x_f.astype(BF16)
    for layer in range(depth):
        j = layer // 2
        if layer % 2 == 0:
            x_f, x_b = _even_layer(x_f, x_b, bsz, seq, ev_w_in[j], lower_bounds[j], ev_hgrn_norm[j], ev_w_out[j],
                                   (ev_ln1_g[j], ev_ln1_b[j]), ev_w1, ev_w3, ev_w2, j,
                                   (ev_ln2_g[j], ev_ln2_b[j]), rope, alpha)
        else:
            lambda_init = 0.8 - 0.6 * math.exp(-0.3 * layer)
            x_f, x_b = _odd_layer(x_f, x_b, bsz, seq, od_w_in[j], od_lambda[j], od_subln[j], od_w_out[j],
                                  (od_ln1_g[j], od_ln1_b[j]), od_router[j], od_w1, od_w3, od_w2, j,
                                  (od_ln2_g[j], od_ln2_b[j]), rope, alpha, lambda_init)
    return x_f.reshape(bsz, seq, d).astype(x.dtype)
```

```python
import functools
import math

import numpy as np
import jax
import jax.numpy as jnp
from jax import lax
from jax.experimental import pallas as pl
from jax.experimental.pallas import tpu as pltpu

F32 = jnp.float32
BF16 = jnp.bfloat16

A_HEAD_DIM = 128
B_HEAD_DIM = 64
B_CONFIGS = ((128, 1), (512, 4), (2048, 16))
BAND_RADIUS = 64
C_HEAD_DIM = 64
N_EXPERTS = 8
ROPE_THETA = 10000.0
LN_EPS = 1e-5
RMS_EPS = 1e-5
MASK_VALUE = -1e30
MIN_FORGET = 1e-30
LOG2_E = math.log2(math.e)

LANES = 128
SUBLANES = 8
VMEM_LIMIT_BYTES = 56 * 1024 * 1024

ROW_CHAINS = 4

HGRN_CHUNK = 128
HGRN_LEVELS = 7
HGRN_HEADS_PER_STEP = 2


def _cparams(*sem):
    return pltpu.CompilerParams(dimension_semantics=sem, vmem_limit_bytes=VMEM_LIMIT_BYTES)


def _tile(n, prefs):
    for p in prefs:
        if n % p == 0:
            return p
    return n


def _dot(a, b):
    return jnp.dot(a, b, preferred_element_type=F32)


def _dot_nt(a, b):
    return lax.dot_general(a, b, (((1,), (1,)), ((), ())), preferred_element_type=F32)


def _dot_tn(a, b):
    return lax.dot_general(a, b, (((0,), (0,)), ((), ())), preferred_element_type=F32)


def _sigmoid(x):
    return 1.0 / (1.0 + jnp.exp(-x))


def _layer_norm(y, g, b):
    mu = jnp.mean(y, axis=-1, keepdims=True)
    d = y - mu
    var = jnp.mean(d * d, axis=-1, keepdims=True)
    return d * lax.rsqrt(var + LN_EPS) * g + b


def _proj_kernel(x_ref, w_ref, o_ref):
    o_ref[...] = _dot(x_ref[...], w_ref[...]).astype(o_ref.dtype)


def _proj(x, w, out_dtype):
    m, k = x.shape
    n = w.shape[1]
    tm = _tile(m, (1024, 512, 256))
    tn = n if n <= 1536 else _tile(n, (1024, 512, 256, 128))
    return pl.pallas_call(
        _proj_kernel,
        grid=(m // tm, n // tn),
        in_specs=[pl.BlockSpec((tm, k), lambda i, j: (i, 0)), pl.BlockSpec((k, tn), lambda i, j: (0, j))],
        out_specs=pl.BlockSpec((tm, tn), lambda i, j: (i, j)),
        out_shape=jax.ShapeDtypeStruct((m, n), out_dtype),
        compiler_params=_cparams("parallel", "arbitrary"),
        name="proj",
    )(x, w)


def _proj_rope_kernel(x_ref, w_ref, cos_ref, sin_ref, scale_ref, o_ref):
    tm, tn = o_ref.shape
    sub = tm // ROW_CHAINS
    w = w_ref[...]
    lane = lax.broadcasted_iota(jnp.int32, (sub, tn), 1)
    first_half = (lane % B_HEAD_DIM) < (B_HEAD_DIM // 2)
    reps = tn // cos_ref.shape[1]
    for k in range(ROW_CHAINS):
        rows = pl.ds(k * sub, sub)
        acc = _dot(x_ref[rows, :], w)
        partner = jnp.where(first_half, pltpu.roll(acc, tn - B_HEAD_DIM // 2, 1), pltpu.roll(acc, B_HEAD_DIM // 2, 1))
        cos = jnp.tile(cos_ref[rows, :], (1, reps))
        sin = jnp.tile(sin_ref[rows, :], (1, reps))
        o_ref[rows, :] = ((acc * cos + partner * sin) * scale_ref[...]).astype(o_ref.dtype)


def _proj_rope(x, w, cos_t, sin_t, scale_row, seq):
    m, k = x.shape
    n = w.shape[1]
    tm = _tile(seq, (1024, 512, 256))
    tn = _tile(n, (1024, 512, 256, 128))
    nsb = seq // tm
    return pl.pallas_call(
        _proj_rope_kernel,
        grid=(m // tm, n // tn),
        in_specs=[
            pl.BlockSpec((tm, k), lambda i, j: (i, 0)),
            pl.BlockSpec((k, tn), lambda i, j: (0, j)),
            pl.BlockSpec((tm, LANES), lambda i, j: (i % nsb, 0)),
            pl.BlockSpec((tm, LANES), lambda i, j: (i % nsb, 0)),
            pl.BlockSpec((1, tn), lambda i, j: (0, j)),
        ],
        out_specs=pl.BlockSpec((tm, tn), lambda i, j: (i, j)),
        out_shape=jax.ShapeDtypeStruct((m, n), BF16),
        compiler_params=_cparams("parallel", "arbitrary"),
        name="proj_rope",
    )(x, w, cos_t, sin_t, scale_row)


def _out_ln_kernel(*refs, n_in, alpha):
    xs = refs[:n_in]
    ws = refs[n_in:2 * n_in]
    resid_ref, g_ref, b_ref, of_ref, ob_ref = refs[2 * n_in:]
    sub = of_ref.shape[0] // ROW_CHAINS
    for k in range(ROW_CHAINS):
        rows = pl.ds(k * sub, sub)
        acc = _dot(xs[0][rows, :], ws[0][...])
        for x_ref, w_ref in zip(xs[1:], ws[1:]):
            acc = acc + _dot(x_ref[rows, :], w_ref[...])
        z = _layer_norm(alpha * resid_ref[rows, :] + acc, g_ref[...], b_ref[...])
        of_ref[rows, :] = z
        ob_ref[rows, :] = z.astype(BF16)


def _out_ln(xs, ws, resid, g, b, alpha):
    m, d = resid.shape
    tm = _tile(m, (1024, 512, 256))
    n_in = len(xs)
    in_specs = [pl.BlockSpec((tm, x.shape[1]), lambda i: (i, 0)) for x in xs]
    in_specs += [pl.BlockSpec(w.shape, lambda i: (0, 0)) for w in ws]
    in_specs += [pl.BlockSpec((tm, d), lambda i: (i, 0)), pl.BlockSpec((1, d), lambda i: (0, 0)),
                 pl.BlockSpec((1, d), lambda i: (0, 0))]
    return pl.pallas_call(
        functools.partial(_out_ln_kernel, n_in=n_in, alpha=alpha),
        grid=(m // tm,),
        in_specs=in_specs,
        out_specs=[pl.BlockSpec((tm, d), lambda i: (i, 0)), pl.BlockSpec((tm, d), lambda i: (i, 0))],
        out_shape=[jax.ShapeDtypeStruct((m, d), F32), jax.ShapeDtypeStruct((m, d), BF16)],
        compiler_params=_cparams("parallel"),
        name="out_ln",
    )(*xs, *ws, resid, g.reshape(1, d), b.reshape(1, d))


def _swiglu_accumulate(x_ref, w1, w3, w2, acc_ref):
    sub = x_ref.shape[0] // ROW_CHAINS
    for k in range(ROW_CHAINS):
        rows = pl.ds(k * sub, sub)
        h1 = _dot(x_ref[rows, :], w1)
        h3 = _dot(x_ref[rows, :], w3)
        h = (h1 * _sigmoid(h1)) * h3
        acc_ref[rows, :] += _dot(h.astype(BF16), w2)


def _ffn_ln_kernel(x_ref, w1_ref, w3_ref, w2_ref, resid_ref, g_ref, b_ref, of_ref, ob_ref, acc_ref, *, alpha):
    j = pl.program_id(1)

    @pl.when(j == 0)
    def _():
        acc_ref[...] = jnp.zeros_like(acc_ref)

    _swiglu_accumulate(x_ref, w1_ref[...], w3_ref[...], w2_ref[...], acc_ref)

    @pl.when(j == pl.num_programs(1) - 1)
    def _():
        z = _layer_norm(alpha * resid_ref[...] + acc_ref[...], g_ref[...], b_ref[...])
        of_ref[...] = z
        ob_ref[...] = z.astype(BF16)


def _ffn_ln(x_bf, resid, w1, w3, w2, g, b, alpha):
    m, d = resid.shape
    ff = w1.shape[1]
    tm = _tile(m, (1024, 512, 256))
    tf = _tile(ff, (1408, 256, 128))
    return pl.pallas_call(
        functools.partial(_ffn_ln_kernel, alpha=alpha),
        grid=(m // tm, ff // tf),
        in_specs=[
            pl.BlockSpec((tm, d), lambda i, j: (i, 0)),
            pl.BlockSpec((d, tf), lambda i, j: (0, j)),
            pl.BlockSpec((d, tf), lambda i, j: (0, j)),
            pl.BlockSpec((tf, d), lambda i, j: (j, 0)),
            pl.BlockSpec((tm, d), lambda i, j: (i, 0)),
            pl.BlockSpec((1, d), lambda i, j: (0, 0)),
            pl.BlockSpec((1, d), lambda i, j: (0, 0)),
        ],
        out_specs=[pl.BlockSpec((tm, d), lambda i, j: (i, 0)), pl.BlockSpec((tm, d), lambda i, j: (i, 0))],
        out_shape=[jax.ShapeDtypeStruct((m, d), F32), jax.ShapeDtypeStruct((m, d), BF16)],
        scratch_shapes=[pltpu.VMEM((tm, d), F32)],
        compiler_params=_cparams("parallel", "arbitrary"),
        name="ffn_ln",
    )(x_bf, w1, w3, w2, resid, g.reshape(1, d), b.reshape(1, d))


def _cast_kernel(w_ref, o_ref):
    o_ref[...] = w_ref[...].astype(o_ref.dtype)


def _layer_weights_bf16(w, layer):
    rows, cols = w.shape[-2:]
    n_mat = math.prod(w.shape[1:-2])
    flat = w.reshape(w.shape[0] * n_mat * rows, cols)
    out = pl.pallas_call(
        _cast_kernel,
        grid=(n_mat,),
        in_specs=[pl.BlockSpec((rows, cols), lambda i: (layer * n_mat + i, 0))],
        out_specs=pl.BlockSpec((rows, cols), lambda i: (i, 0)),
        out_shape=jax.ShapeDtypeStruct((n_mat * rows, cols), BF16),
        compiler_params=_cparams("parallel"),
        name="cast_weights",
    )(flat)
    return out.reshape(w.shape[1:])


def _moe_ffn_kernel(te_ref, na_ref, x_ref, w1_ref, w3_ref, w2_ref, *rest):
    o_ref, acc_ref = rest[-2:]
    i = pl.program_id(0)
    j = pl.program_id(1)
    active = i < na_ref[0]

    @pl.when(active & (j == 0))
    def _():
        acc_ref[...] = jnp.zeros_like(acc_ref)

    @pl.when(active)
    def _():
        _swiglu_accumulate(x_ref, w1_ref[0], w3_ref[0], w2_ref[0], acc_ref)

    @pl.when(active & (j == pl.num_programs(1) - 1))
    def _():
        o_ref[...] = acc_ref[...].astype(o_ref.dtype)


def _moe_ffn(x_sorted, tile_expert, n_active, w1, w3, w2, tm, tile_offset, total_rows, y_prev=None):
    p, d = x_sorted.shape
    ff = w1.shape[2]
    tf = _tile(ff, (1408, 256, 128))
    nf = ff // tf

    def row(i, na):
        return jnp.maximum(jnp.minimum(i, na[0] - 1), 0)

    def col(i, j, na):
        return jnp.where(i < na[0], j, nf - 1)

    operands = [tile_expert, n_active, x_sorted, w1, w3, w2]
    in_specs = [
        pl.BlockSpec((tm, d), lambda i, j, te, na: (row(i, na), 0)),
        pl.BlockSpec((1, d, tf), lambda i, j, te, na: (te[row(i, na)], 0, col(i, j, na))),
        pl.BlockSpec((1, d, tf), lambda i, j, te, na: (te[row(i, na)], 0, col(i, j, na))),
        pl.BlockSpec((1, tf, d), lambda i, j, te, na: (te[row(i, na)], col(i, j, na), 0)),
    ]
    aliases = {}
    if y_prev is not None:
        in_specs.append(pl.BlockSpec(memory_space=pl.ANY))
        aliases = {len(operands): 0}
        operands.append(y_prev)
    grid_spec = pltpu.PrefetchScalarGridSpec(
        num_scalar_prefetch=2,
        grid=(p // tm, nf),
        in_specs=in_specs,
        out_specs=pl.BlockSpec((tm, d), lambda i, j, te, na: (tile_offset + row(i, na), 0)),
        scratch_shapes=[pltpu.VMEM((tm, d), F32)],
    )
    return pl.pallas_call(
        _moe_ffn_kernel,
        grid_spec=grid_spec,
        out_shape=jax.ShapeDtypeStruct((total_rows, d), BF16),
        input_output_aliases=aliases,
        compiler_params=_cparams("arbitrary", "arbitrary"),
        name="moe_ffn",
    )(*operands)


def _router_kernel(x_ref, r_ref, gate_ref, idx_ref):
    logits = jnp.dot(x_ref[...], r_ref[...], precision=lax.Precision.HIGHEST, preferred_element_type=F32)
    lane = lax.broadcasted_iota(jnp.int32, logits.shape, 1)
    neg = jnp.float32(-jnp.inf)
    logits = jnp.where(lane < N_EXPERTS, logits, neg)
    v1 = jnp.max(logits, axis=-1, keepdims=True)
    i1 = jnp.min(jnp.where(logits == v1, lane, LANES), axis=-1, keepdims=True)
    rest = jnp.where(lane == i1, neg, logits)
    v2 = jnp.max(rest, axis=-1, keepdims=True)
    i2 = jnp.min(jnp.where(rest == v2, lane, LANES), axis=-1, keepdims=True)
    e = jnp.exp(v2 - v1)
    g1 = 1.0 / (1.0 + e)
    g2 = e / (1.0 + e)
    gate_ref[...] = jnp.where(lane == 0, g1, jnp.where(lane == 1, g2, 0.0))
    idx_ref[...] = jnp.where(lane == 0, i1, jnp.where(lane == 1, i2, 0))


def _router(x_f32, router_padded):
    m, d = x_f32.shape
    tm = _tile(m, (1024, 512, 256))
    return pl.pallas_call(
        _router_kernel,
        grid=(m // tm,),
        in_specs=[pl.BlockSpec((tm, d), lambda i: (i, 0)), pl.BlockSpec((d, LANES), lambda i: (0, 0))],
        out_specs=[pl.BlockSpec((tm, LANES), lambda i: (i, 0)), pl.BlockSpec((tm, LANES), lambda i: (i, 0))],
        out_shape=[jax.ShapeDtypeStruct((m, LANES), F32), jax.ShapeDtypeStruct((m, LANES), jnp.int32)],
        compiler_params=_cparams("parallel"),
        name="router",
    )(x_f32, router_padded)


def _moe_combine_kernel(resid_ref, y1_ref, y2_ref, gate_ref, g_ref, b_ref, of_ref, ob_ref, *, alpha):
    gates = gate_ref[...]
    y = gates[:, 0:1] * y1_ref[...].astype(F32) + gates[:, 1:2] * y2_ref[...].astype(F32)
    z = _layer_norm(alpha * resid_ref[...] + y, g_ref[...], b_ref[...])
    of_ref[...] = z
    ob_ref[...] = z.astype(BF16)


def _moe_combine(resid, y1, y2, gates, g, b, alpha):
    m, d = resid.shape
    tm = _tile(m, (512, 256))
    row = lambda i: (i, 0)
    fixed = lambda i: (0, 0)
    return pl.pallas_call(
        functools.partial(_moe_combine_kernel, alpha=alpha),
        grid=(m // tm,),
        in_specs=[pl.BlockSpec((tm, d), row), pl.BlockSpec((tm, d), row), pl.BlockSpec((tm, d), row),
                  pl.BlockSpec((tm, LANES), row), pl.BlockSpec((1, d), fixed), pl.BlockSpec((1, d), fixed)],
        out_specs=[pl.BlockSpec((tm, d), row), pl.BlockSpec((tm, d), row)],
        out_shape=[jax.ShapeDtypeStruct((m, d), F32), jax.ShapeDtypeStruct((m, d), BF16)],
        compiler_params=_cparams("parallel"),
        name="moe_combine",
    )(resid, y1, y2, gates, g.reshape(1, d), b.reshape(1, d))


def _hgrn_masks(c, levels):
    t = np.arange(c)[:, None]
    s = np.arange(c)[None, :]
    tri = np.stack([(s <= t), (s >= t)]).astype(np.float32)
    fwd, bwd = [], []
    for l in range(levels):
        same = (t >> (l + 1)) == (s >> (l + 1))
        t_up = ((t >> l) & 1) == 1
        s_up = ((s >> l) & 1) == 1
        fwd.append(same & t_up & ~s_up)
        bwd.append(same & ~t_up & s_up)
    fwd.append(t == s)
    bwd.append(t == s)
    up = np.stack([np.broadcast_to(((t >> l) & 1) == 1, (c, A_HEAD_DIM)) for l in range(levels)])
    sign = np.stack([np.where(up, 1.0, -1.0), np.where(up, -1.0, 1.0)]).astype(np.float32)
    return tri, np.stack([np.stack(fwd), np.stack(bwd)]).astype(np.float32), sign


def _segment_reference(x, level, forward):
    c, w = x.shape
    half = 1 << level
    seg = 2 * half
    idx = half - 1 if forward else half
    if seg >= SUBLANES:
        xr = x.reshape(c // seg, seg, w)
        return jnp.broadcast_to(xr[:, idx:idx + 1, :], xr.shape).reshape(c, w)
    x3 = x.reshape(c // SUBLANES, SUBLANES, w)
    sub = lax.broadcasted_iota(jnp.int32, x3.shape, 1)
    r3 = jnp.broadcast_to(x3[:, idx:idx + 1, :], x3.shape)
    for j in range(1, SUBLANES // seg):
        row = j * seg + idx
        r3 = jnp.where(sub >= j * seg, jnp.broadcast_to(x3[:, row:row + 1, :], x3.shape), r3)
    return r3.reshape(c, w)


def _hgrn_chunk(q, k, v, log_f, state_t, tri, masks, signs, forward):
    c = q.shape[0]
    x = jnp.dot(tri, log_f, precision=lax.Precision.HIGHEST, preferred_element_type=F32)
    scores = masks[HGRN_LEVELS] * _dot_nt(q.astype(BF16), k.astype(BF16))
    for level in range(HGRN_LEVELS):
        ref = _segment_reference(x, level, forward)
        sign = signs[level]
        decay = jnp.exp(sign * (x - ref))
        z = (jnp.where(sign > 0, q, k) * decay).astype(BF16)
        scores = scores + masks[level] * _dot_nt(z, z)
    x_end = x[c - 1:c, :] if forward else x[0:1, :]
    q_dec = (q * jnp.exp(x)).astype(BF16)
    o = _dot(scores.astype(BF16), v.astype(BF16)) + _dot_nt(q_dec, state_t.astype(BF16))
    k_dec = (k * jnp.exp(x_end - x)).astype(BF16)
    new_state_t = state_t * jnp.exp(x_end) + _dot_tn(v.astype(BF16), k_dec)
    return o, new_state_t


def _hgrn_kernel(q_ref, v_ref, g_ref, zf_ref, zb_ref, lb_ref, nw_ref, tri_ref, msk_ref, sgn_ref, o_ref, acc_ref, *,
                 n_chunks):
    c = HGRN_CHUNK
    hd = A_HEAD_DIM

    def gates(z, lb):
        e = jnp.exp(-jnp.abs(z))
        inv = 1.0 / (1.0 + e)
        pos = z >= 0
        sig = jnp.where(pos, inv, e * inv)
        sig_neg = jnp.where(pos, e * inv, inv)
        f = lb + (1.0 - lb) * sig
        return jnp.log(jnp.maximum(f, MIN_FORGET)), (1.0 - lb) * sig_neg

    def load(ref, c0, h):
        return ref[0, pl.ds(c0, c), h * hd:(h + 1) * hd].astype(F32)

    def emit(c0, h, tot):
        ms = jnp.mean(tot * tot, axis=-1, keepdims=True)
        g = load(g_ref, c0, h)
        out = tot * lax.rsqrt(ms + RMS_EPS) * nw_ref[...] * (g * _sigmoid(g))
        o_ref[0, pl.ds(c0, c), h * hd:(h + 1) * hd] = out.astype(o_ref.dtype)

    def one_direction(h, c0, z_ref, state_t, direction):
        lb = lb_ref[:, h * hd:(h + 1) * hd]
        log_f, k = gates(load(z_ref, c0, h), lb)
        masks = [msk_ref[direction, l] for l in range(HGRN_LEVELS + 1)]
        signs = [sgn_ref[direction, l] for l in range(HGRN_LEVELS)]
        return _hgrn_chunk(load(q_ref, c0, h), k, load(v_ref, c0, h), log_f, state_t, tri_ref[direction], masks,
                           signs, direction == 0)

    def step(i, states, finalize):
        cf = pl.multiple_of(i * c, c)
        cb = pl.multiple_of((n_chunks - 1 - i) * c, c)
        new_states = []
        for h in range(HGRN_HEADS_PER_STEP):
            o_f, st_f = one_direction(h, cf, zf_ref, states[2 * h], 0)
            o_b, st_b = one_direction(h, cb, zb_ref, states[2 * h + 1], 1)
            cols = slice(h * hd, (h + 1) * hd)
            if finalize:
                emit(cf, h, acc_ref[pl.ds(cf, c), cols] + o_f)
                emit(cb, h, acc_ref[pl.ds(cb, c), cols] + o_b)
            else:
                acc_ref[pl.ds(cf, c), cols] = o_f
                acc_ref[pl.ds(cb, c), cols] = o_b
            new_states += [st_f, st_b]
        return tuple(new_states)

    zero = jnp.zeros((hd, hd), F32)
    half = n_chunks // 2
    states = lax.fori_loop(0, half, functools.partial(step, finalize=False), (zero,) * (2 * HGRN_HEADS_PER_STEP))
    lax.fori_loop(half, n_chunks, functools.partial(step, finalize=True), states)


def _hgrn(qvg, zz, lb, norm_w, n_heads):
    bsz, s, _ = qvg.shape
    c = HGRN_CHUNK
    assert s % (2 * c) == 0 and n_heads % HGRN_HEADS_PER_STEP == 0
    tri, masks, signs = _hgrn_masks(c, HGRN_LEVELS)
    hd = A_HEAD_DIM
    wide = HGRN_HEADS_PER_STEP * hd
    n_groups = n_heads // HGRN_HEADS_PER_STEP
    col = lambda off: pl.BlockSpec((1, s, wide), lambda b, h: (b, 0, off + h))
    const = lambda a: pl.BlockSpec(a.shape, lambda b, h: (0,) * a.ndim)
    return pl.pallas_call(
        functools.partial(_hgrn_kernel, n_chunks=s // c),
        grid=(bsz, n_groups),
        in_specs=[
            col(0), col(n_groups), col(2 * n_groups), col(0), col(n_groups),
            pl.BlockSpec((1, wide), lambda b, h: (0, h)),
            pl.BlockSpec((1, hd), lambda b, h: (0, 0)),
            const(tri), const(masks), const(signs),
        ],
        out_specs=pl.BlockSpec((1, s, wide), lambda b, h: (b, 0, h)),
        out_shape=jax.ShapeDtypeStruct((bsz, s, n_heads * hd), BF16),
        scratch_shapes=[pltpu.VMEM((s, wide), F32)],
        compiler_params=_cparams("parallel", "parallel"),
        name="hgrn2",
    )(qvg, qvg, qvg, zz, zz, lb.reshape(1, -1), norm_w.reshape(1, hd), jnp.asarray(tri), jnp.asarray(masks),
      jnp.asarray(signs))


def _dilated_kernel(q_ref, k_ref, v_ref, o_ref, qf_ref, kf_ref, vf_ref, oc_ref, lc_ref, *, seq):
    qf_ref[...] = q_ref[0].astype(F32)
    kf_ref[...] = k_ref[0].astype(F32)
    vf_ref[...] = v_ref[0].astype(F32)
    n_cfg = len(B_CONFIGS)
    for ci, (_, dil) in enumerate(B_CONFIGS):
        length = seq // dil
        tq = min(LANES, length)
        win = min(length, tq + 2 * BAND_RADIUS)
        head0 = lax.broadcasted_iota(jnp.int32, (tq, LANES), 1) < B_HEAD_DIM
        rel = lax.broadcasted_iota(jnp.int32, (tq, win), 1) - lax.broadcasted_iota(jnp.int32, (tq, win), 0)

        def rows(first, size, dil=dil):
            return pl.ds(first, size) if dil == 1 else pl.ds(first, size, stride=dil)

        def block(t, carry, ci=ci, dil=dil, length=length, tq=tq, win=win, head0=head0, rel=rel, rows=rows):
            res = t % dil
            q0 = (t // dil) * tq
            start = jnp.clip(q0 - BAND_RADIUS, 0, length - win)
            valid = jnp.abs(rel + (start - q0)) <= BAND_RADIUS
            q_rows = rows(q0 * dil + res, tq)
            k_rows = rows(start * dil + res, win)
            q = qf_ref[q_rows, :].astype(BF16)
            kw = kf_ref[k_rows, :].astype(BF16)
            vw = vf_ref[k_rows, :].astype(BF16)

            def one_head(mask):
                s = _dot_nt(jnp.where(mask, q, jnp.zeros_like(q)), kw)
                s = jnp.where(valid, s, MASK_VALUE)
                m = jnp.max(s, axis=-1, keepdims=True)
                p = jnp.exp2(s - m)
                l = jnp.sum(p, axis=-1, keepdims=True)
                return _dot(p.astype(BF16), vw) / l, m + jnp.log(l) * LOG2_E

            oa, la = one_head(head0)
            ob, lb = one_head(jnp.logical_not(head0))
            oc_ref[ci, q_rows, :] = jnp.where(head0, oa, ob)
            lc_ref[ci, q_rows, :] = jnp.where(head0, la, lb)
            return carry

        lax.fori_loop(0, dil * (length // tq), block, 0, unroll=8)

    tmix = min(256, seq)

    def mix(i, carry):
        r0 = pl.multiple_of(i * tmix, tmix)
        lses = [lc_ref[c, pl.ds(r0, tmix), :] for c in range(n_cfg)]
        top = lses[0]
        for l in lses[1:]:
            top = jnp.maximum(top, l)
        num = jnp.zeros((tmix, LANES), F32)
        den = jnp.zeros((tmix, LANES), F32)
        for c in range(n_cfg):
            w = jnp.exp2(lses[c] - top)
            num = num + w * oc_ref[c, pl.ds(r0, tmix), :]
            den = den + w
        o_ref[0, pl.ds(r0, tmix), :] = (num / den).astype(o_ref.dtype)
        return carry

    lax.fori_loop(0, seq // tmix, mix, 0)


def _dilated_attention(qk, v):
    bsz, s, w = v.shape
    n_pairs = w // LANES
    n_cfg = len(B_CONFIGS)
    return pl.pallas_call(
        functools.partial(_dilated_kernel, seq=s),
        grid=(bsz, n_pairs),
        in_specs=[
            pl.BlockSpec((1, s, LANES), lambda b, h: (b, 0, h)),
            pl.BlockSpec((1, s, LANES), lambda b, h: (b, 0, n_pairs + h)),
            pl.BlockSpec((1, s, LANES), lambda b, h: (b, 0, h)),
        ],
        out_specs=pl.BlockSpec((1, s, LANES), lambda b, h: (b, 0, h)),
        out_shape=jax.ShapeDtypeStruct((bsz, s, w), BF16),
        scratch_shapes=[pltpu.VMEM((s, LANES), F32)] * 3 + [pltpu.VMEM((n_cfg, s, LANES), F32)] * 2,
        compiler_params=_cparams("parallel", "parallel"),
        name="dilated_attn",
    )(qk, qk, v)


def _diff_kernel(q_ref, k_ref, v_ref, lam_ref, sub_ref, o_ref, *, tk, lambda_init):
    q = q_ref[0]
    tq = q.shape[0]
    s_len = k_ref.shape[1]
    lane = lax.broadcasted_iota(jnp.int32, q.shape, 1)
    zero = jnp.zeros_like(q)
    qs = (jnp.where(lane < C_HEAD_DIM, q, zero), jnp.where(lane >= C_HEAD_DIM, q, zero))
    tiles = [(t * LANES, (t + 1) * LANES) for t in range(tk // LANES)]
    m = [jnp.full((tq, 1), -jnp.inf, F32)] * 2
    l = [jnp.zeros((tq, 1), F32)] * 2
    acc = [jnp.zeros((tq, LANES), F32)] * 2
    for c in range(s_len // tk):
        lo, hi = c * tk, (c + 1) * tk
        for h in range(2):
            s = _dot_nt(qs[h], k_ref[0, lo:hi, :])
            m_tile = s[:, 0:LANES]
            for a, b in tiles[1:]:
                m_tile = jnp.maximum(m_tile, s[:, a:b])
            m_new = jnp.maximum(m[h], jnp.max(m_tile, axis=-1, keepdims=True))
            alpha = jnp.exp2(m[h] - m_new)
            p = jnp.exp2(s - m_new)
            l_tile = p[:, 0:LANES]
            for a, b in tiles[1:]:
                l_tile = l_tile + p[:, a:b]
            l[h] = alpha * l[h] + jnp.sum(l_tile, axis=-1, keepdims=True)
            acc[h] = alpha * acc[h] + _dot(p.astype(BF16), v_ref[0, lo:hi, :])
            m[h] = m_new
    outs = (acc[0] / l[0], acc[1] / l[1])
    lp = lam_ref[...]
    lam = (jnp.exp(jnp.sum(lp[0:1] * lp[1:2], axis=-1, keepdims=True))
           - jnp.exp(jnp.sum(lp[2:3] * lp[3:4], axis=-1, keepdims=True)) + lambda_init)
    o = outs[0] - lam * outs[1]
    ms_o = jnp.mean(o * o, axis=-1, keepdims=True)
    o_ref[0] = (o * lax.rsqrt(ms_o + RMS_EPS) * sub_ref[...] * (1.0 - lambda_init)).astype(o_ref.dtype)


def _diff_attention(qk, v, lam_params, subln_w, lambda_init):
    bsz, s, w = v.shape
    n_heads = w // LANES
    tq = _tile(s, (1024, 512, 256, 128))
    tk = _tile(s, (2048, 1024, 512, 256, 128))
    return pl.pallas_call(
        functools.partial(_diff_kernel, tk=tk, lambda_init=lambda_init),
        grid=(bsz, n_heads, s // tq),
        in_specs=[
            pl.BlockSpec((1, tq, LANES), lambda b, h, i: (b, i, h)),
            pl.BlockSpec((1, s, LANES), lambda b, h, i: (b, 0, n_heads + h)),
            pl.BlockSpec((1, s, LANES), lambda b, h, i: (b, 0, h)),
            pl.BlockSpec(lam_params.shape, lambda b, h, i: (0, 0)),
            pl.BlockSpec((1, LANES), lambda b, h, i: (0, 0)),
        ],
        out_specs=pl.BlockSpec((1, tq, LANES), lambda b, h, i: (b, i, h)),
        out_shape=jax.ShapeDtypeStruct((bsz, s, w), BF16),
        compiler_params=_cparams("parallel", "parallel", "arbitrary"),
        name="diff_attn",
    )(qk, qk, v, lam_params, subln_w.reshape(1, LANES))


def _rope_tables(seq, width):
    half = B_HEAD_DIM // 2
    inv = ROPE_THETA ** (-jnp.arange(0, B_HEAD_DIM, 2, dtype=F32) / B_HEAD_DIM)
    ang = jnp.arange(seq, dtype=F32)[:, None] * inv[None, :]
    cos, sin = jnp.cos(ang), jnp.sin(ang)
    reps = width // B_HEAD_DIM
    assert half * 2 == B_HEAD_DIM
    return jnp.tile(jnp.concatenate([cos, cos], axis=1), (1, reps)), jnp.tile(jnp.concatenate([-sin, sin], axis=1), (1, reps))


def _even_layer(x_f, x_b, bsz, seq, w_in, lb, norm_w, w_out, ln1, w1, w3, w2, layer, ln2, rope, alpha):
    d = x_f.shape[1]
    aw = d // 2
    n_heads_a = aw // A_HEAD_DIM
    w_in = w_in.astype(BF16)
    cols = lambda a, b: w_in[:, a * aw:b * aw]
    qvg = _proj(x_b, jnp.concatenate([cols(0, 1), cols(3, 5)], axis=1), BF16)
    zz = _proj(x_b, cols(1, 3), F32)
    cos_t, sin_t = rope
    scale_row = jnp.concatenate([jnp.full((1, aw), LOG2_E * B_HEAD_DIM ** -0.5, F32), jnp.ones((1, aw), F32)], axis=1)
    qk = _proj_rope(x_b, cols(5, 7), cos_t, sin_t, scale_row, seq)
    vb = _proj(x_b, cols(7, 8), BF16)
    oa = _hgrn(qvg.reshape(bsz, seq, -1), zz.reshape(bsz, seq, -1), lb, norm_w, n_heads_a)
    ob = _dilated_attention(qk.reshape(bsz, seq, 2 * aw), vb.reshape(bsz, seq, aw))
    w_out = w_out.astype(BF16)
    x_f, x_b = _out_ln([oa.reshape(bsz * seq, aw), ob.reshape(bsz * seq, aw)], [w_out[:aw], w_out[aw:]], x_f,
                       ln1[0], ln1[1], alpha)
    return _ffn_ln(x_b, x_f, w1[layer].astype(BF16), w3[layer].astype(BF16), w2[layer].astype(BF16), ln2[0], ln2[1],
                   alpha)


def _moe_dispatch(idx, n_tokens, tm):
    e_flat = idx[:, :2].reshape(-1)
    onehot = (e_flat[None, :] == jnp.arange(N_EXPERTS, dtype=jnp.int32)[:, None]).astype(jnp.int32)
    counts = jnp.sum(onehot, axis=1)
    tiles = (counts + tm - 1) // tm
    tile_end = jnp.cumsum(tiles)
    group_start = (tile_end - tiles) * tm
    dest = jnp.sum((jnp.cumsum(onehot, axis=1) - 1 + group_start[:, None]) * onehot, axis=0)
    n_tiles = (2 * n_tokens) // tm + N_EXPERTS
    src_tok = jnp.zeros((n_tiles * tm,), jnp.int32).at[dest].set(jnp.arange(2 * n_tokens, dtype=jnp.int32) // 2)
    tile_ids = jnp.arange(n_tiles, dtype=jnp.int32)
    tile_expert = jnp.minimum(jnp.sum((tile_ids[:, None] >= tile_end[None, :]).astype(jnp.int32), axis=1),
                              N_EXPERTS - 1)
    return src_tok, dest.reshape(n_tokens, 2), tile_expert, tile_end[-1:].astype(jnp.int32)


def _odd_layer(x_f, x_b, bsz, seq, w_in, lam_params, subln_w, w_out, ln1, router, w1, w3, w2, layer, ln2, rope,
               alpha, lambda_init):
    d = x_f.shape[1]
    n_tok = bsz * seq
    w_in = w_in.astype(BF16)
    cos_t, sin_t = rope
    scale_row = jnp.concatenate([jnp.full((1, d), LOG2_E * C_HEAD_DIM ** -0.5, F32), jnp.ones((1, d), F32)], axis=1)
    qk = _proj_rope(x_b, w_in[:, :2 * d], cos_t, sin_t, scale_row, seq)
    v = _proj(x_b, w_in[:, 2 * d:], BF16)
    o = _diff_attention(qk.reshape(bsz, seq, 2 * d), v.reshape(bsz, seq, d), lam_params.astype(F32), subln_w,
                        lambda_init)
    x_f, x_b = _out_ln([o.reshape(n_tok, d)], [w_out.astype(BF16)], x_f, ln1[0], ln1[1], alpha)
    router_padded = jnp.pad(router.astype(F32), ((0, 0), (0, LANES - N_EXPERTS)))
    gates, idx = _router(x_f, router_padded)
    tm = _tile(n_tok, (1024, 512, 256))
    src_tok, pos, tile_expert, n_active = _moe_dispatch(idx, n_tok, tm)
    n_tiles = src_tok.shape[0] // tm
    w1, w3, w2 = (_layer_weights_bf16(w, layer) for w in (w1, w3, w2))
    y = None
    for lo, hi in ((0, n_tiles // 2), (n_tiles // 2, n_tiles)):
        x_rows = jnp.take(x_b, src_tok[lo * tm:hi * tm], axis=0, mode="clip")
        n_act = jnp.clip(n_active - lo, 0, hi - lo)
        y = _moe_ffn(x_rows, tile_expert[lo:hi], n_act, w1, w3, w2, tm, lo, n_tiles * tm, y)
    y1 = jnp.take(y, pos[:, 0], axis=0, mode="clip")
    y2 = jnp.take(y, pos[:, 1], axis=0, mode="clip")
    return _moe_combine(x_f, y1, y2, gates, ln2[0], ln2[1], alpha)


def kernel(x, ev_w_in, ev_lb_logits, ev_hgrn_norm, ev_w_out, ev_ln1_g, ev_ln1_b, ev_w1, ev_w3, ev_w2, ev_ln2_g,
           ev_ln2_b, od_w_in, od_lambda, od_subln, od_w_out, od_ln1_g, od_ln1_b, od_router, od_w1, od_w3, od_w2,
           od_ln2_g, od_ln2_b):
    bsz, seq, d = x.shape
    depth = ev_w_in.shape[0] + od_w_in.shape[0]
    alpha = (2 * depth) ** 0.25
    rope = _rope_tables(seq, LANES)
    lb_soft = jax.nn.softmax(ev_lb_logits.astype(F32), axis=0)
    lower_bounds = jnp.cumsum(lb_soft, axis=0) - lb_soft[0]
    x_f = x.reshape(bsz * seq, d).astype(F32)
    x_b = x_f.astype(BF16)
    for layer in range(depth):
        j = layer // 2
        if layer % 2 == 0:
            x_f, x_b = _even_layer(x_f, x_b, bsz, seq, ev_w_in[j], lower_bounds[j], ev_hgrn_norm[j], ev_w_out[j],
                                   (ev_ln1_g[j], ev_ln1_b[j]), ev_w1, ev_w3, ev_w2, j,
                                   (ev_ln2_g[j], ev_ln2_b[j]), rope, alpha)
        else:
            lambda_init = 0.8 - 0.6 * math.exp(-0.3 * layer)
            x_f, x_b = _odd_layer(x_f, x_b, bsz, seq, od_w_in[j], od_lambda[j], od_subln[j], od_w_out[j],
                                  (od_ln1_g[j], od_ln1_b[j]), od_router[j], od_w1, od_w3, od_w2, j,
                                  (od_ln2_g[j], od_ln2_b[j]), rope, alpha, lambda_init)
    return x_f.reshape(bsz, seq, d).astype(x.dtype)
```

```python
import functools
import math

import numpy as np
import jax
import jax.numpy as jnp
from jax import lax
from jax.experimental import pallas as pl
from jax.experimental.pallas import tpu as pltpu

F32 = jnp.float32
BF16 = jnp.bfloat16

A_HEAD_DIM = 128
B_HEAD_DIM = 64
B_CONFIGS = ((128, 1), (512, 4), (2048, 16))
BAND_RADIUS = 64
C_HEAD_DIM = 64
N_EXPERTS = 8
ROPE_THETA = 10000.0
LN_EPS = 1e-5
RMS_EPS = 1e-5
MASK_VALUE = -1e30
MIN_FORGET = 1e-30
LOG2_E = math.log2(math.e)

LANES = 128
SUBLANES = 8
VMEM_LIMIT_BYTES = 56 * 1024 * 1024

ROW_CHAINS = 4
FFN_ROW_CHAINS = 2

HGRN_CHUNK = 128
HGRN_LEVELS = 7
HGRN_HEADS_PER_STEP = 2


def _cparams(*sem):
    return pltpu.CompilerParams(dimension_semantics=sem, vmem_limit_bytes=VMEM_LIMIT_BYTES)


def _tile(n, prefs):
    for p in prefs:
        if n % p == 0:
            return p
    return n


def _dot(a, b):
    return jnp.dot(a, b, preferred_element_type=F32)


def _dot_nt(a, b):
    return lax.dot_general(a, b, (((1,), (1,)), ((), ())), preferred_element_type=F32)


def _dot_tn(a, b):
    return lax.dot_general(a, b, (((0,), (0,)), ((), ())), preferred_element_type=F32)


def _sigmoid(x):
    return 1.0 / (1.0 + jnp.exp(-x))


def _layer_norm(y, g, b):
    mu = jnp.mean(y, axis=-1, keepdims=True)
    d = y - mu
    var = jnp.mean(d * d, axis=-1, keepdims=True)
    return d * lax.rsqrt(var + LN_EPS) * g + b


def _proj_kernel(x_ref, w_ref, o_ref):
    o_ref[...] = _dot(x_ref[...], w_ref[...]).astype(o_ref.dtype)


def _proj(x, w, out_dtype):
    m, k = x.shape
    n = w.shape[1]
    tm = _tile(m, (1024, 512, 256))
    tn = n if n <= 1536 else _tile(n, (1024, 512, 256, 128))
    return pl.pallas_call(
        _proj_kernel,
        grid=(m // tm, n // tn),
        in_specs=[pl.BlockSpec((tm, k), lambda i, j: (i, 0)), pl.BlockSpec((k, tn), lambda i, j: (0, j))],
        out_specs=pl.BlockSpec((tm, tn), lambda i, j: (i, j)),
        out_shape=jax.ShapeDtypeStruct((m, n), out_dtype),
        compiler_params=_cparams("parallel", "arbitrary"),
        name="proj",
    )(x, w)


def _proj_rope_kernel(x_ref, w_ref, cos_ref, sin_ref, scale_ref, o_ref):
    tm, tn = o_ref.shape
    sub = tm // ROW_CHAINS
    w = w_ref[...]
    lane = lax.broadcasted_iota(jnp.int32, (sub, tn), 1)
    first_half = (lane % B_HEAD_DIM) < (B_HEAD_DIM // 2)
    reps = tn // cos_ref.shape[1]
    for k in range(ROW_CHAINS):
        rows = pl.ds(k * sub, sub)
        acc = _dot(x_ref[rows, :], w)
        partner = jnp.where(first_half, pltpu.roll(acc, tn - B_HEAD_DIM // 2, 1), pltpu.roll(acc, B_HEAD_DIM // 2, 1))
        cos = jnp.tile(cos_ref[rows, :], (1, reps))
        sin = jnp.tile(sin_ref[rows, :], (1, reps))
        o_ref[rows, :] = ((acc * cos + partner * sin) * scale_ref[...]).astype(o_ref.dtype)


def _proj_rope(x, w, cos_t, sin_t, scale_row, seq):
    m, k = x.shape
    n = w.shape[1]
    tm = _tile(seq, (1024, 512, 256))
    tn = _tile(n, (1024, 512, 256, 128))
    nsb = seq // tm
    return pl.pallas_call(
        _proj_rope_kernel,
        grid=(m // tm, n // tn),
        in_specs=[
            pl.BlockSpec((tm, k), lambda i, j: (i, 0)),
            pl.BlockSpec((k, tn), lambda i, j: (0, j)),
            pl.BlockSpec((tm, LANES), lambda i, j: (i % nsb, 0)),
            pl.BlockSpec((tm, LANES), lambda i, j: (i % nsb, 0)),
            pl.BlockSpec((1, tn), lambda i, j: (0, j)),
        ],
        out_specs=pl.BlockSpec((tm, tn), lambda i, j: (i, j)),
        out_shape=jax.ShapeDtypeStruct((m, n), BF16),
        compiler_params=_cparams("parallel", "arbitrary"),
        name="proj_rope",
    )(x, w, cos_t, sin_t, scale_row)


def _out_ln_kernel(*refs, n_in, alpha):
    xs = refs[:n_in]
    ws = refs[n_in:2 * n_in]
    resid_ref, g_ref, b_ref, of_ref, ob_ref = refs[2 * n_in:]
    sub = of_ref.shape[0] // ROW_CHAINS
    for k in range(ROW_CHAINS):
        rows = pl.ds(k * sub, sub)
        acc = _dot(xs[0][rows, :], ws[0][...])
        for x_ref, w_ref in zip(xs[1:], ws[1:]):
            acc = acc + _dot(x_ref[rows, :], w_ref[...])
        z = _layer_norm(alpha * resid_ref[rows, :] + acc, g_ref[...], b_ref[...])
        of_ref[rows, :] = z
        ob_ref[rows, :] = z.astype(BF16)


def _out_ln(xs, ws, resid, g, b, alpha):
    m, d = resid.shape
    tm = _tile(m, (1024, 512, 256))
    n_in = len(xs)
    in_specs = [pl.BlockSpec((tm, x.shape[1]), lambda i: (i, 0)) for x in xs]
    in_specs += [pl.BlockSpec(w.shape, lambda i: (0, 0)) for w in ws]
    in_specs += [pl.BlockSpec((tm, d), lambda i: (i, 0)), pl.BlockSpec((1, d), lambda i: (0, 0)),
                 pl.BlockSpec((1, d), lambda i: (0, 0))]
    return pl.pallas_call(
        functools.partial(_out_ln_kernel, n_in=n_in, alpha=alpha),
        grid=(m // tm,),
        in_specs=in_specs,
        out_specs=[pl.BlockSpec((tm, d), lambda i: (i, 0)), pl.BlockSpec((tm, d), lambda i: (i, 0))],
        out_shape=[jax.ShapeDtypeStruct((m, d), F32), jax.ShapeDtypeStruct((m, d), BF16)],
        compiler_params=_cparams("parallel"),
        name="out_ln",
    )(*xs, *ws, resid, g.reshape(1, d), b.reshape(1, d))


def _swiglu_accumulate(x_ref, w1, w3, w2, acc_ref):
    sub = x_ref.shape[0] // FFN_ROW_CHAINS
    for k in range(FFN_ROW_CHAINS):
        rows = pl.ds(k * sub, sub)
        h1 = _dot(x_ref[rows, :], w1)
        h3 = _dot(x_ref[rows, :], w3)
        h = (h1 * _sigmoid(h1)) * h3
        acc_ref[rows, :] += _dot(h.astype(BF16), w2)


def _ffn_ln_kernel(x_ref, w1_ref, w3_ref, w2_ref, resid_ref, g_ref, b_ref, of_ref, ob_ref, acc_ref, *, alpha):
    j = pl.program_id(1)

    @pl.when(j == 0)
    def _():
        acc_ref[...] = jnp.zeros_like(acc_ref)

    _swiglu_accumulate(x_ref, w1_ref[...], w3_ref[...], w2_ref[...], acc_ref)

    @pl.when(j == pl.num_programs(1) - 1)
    def _():
        z = _layer_norm(alpha * resid_ref[...] + acc_ref[...], g_ref[...], b_ref[...])
        of_ref[...] = z
        ob_ref[...] = z.astype(BF16)


def _ffn_ln(x_bf, resid, w1, w3, w2, g, b, alpha):
    m, d = resid.shape
    ff = w1.shape[1]
    tm = _tile(m, (1024, 512, 256))
    tf = _tile(ff, (1408, 256, 128))
    return pl.pallas_call(
        functools.partial(_ffn_ln_kernel, alpha=alpha),
        grid=(m // tm, ff // tf),
        in_specs=[
            pl.BlockSpec((tm, d), lambda i, j: (i, 0)),
            pl.BlockSpec((d, tf), lambda i, j: (0, j)),
            pl.BlockSpec((d, tf), lambda i, j: (0, j)),
            pl.BlockSpec((tf, d), lambda i, j: (j, 0)),
            pl.BlockSpec((tm, d), lambda i, j: (i, 0)),
            pl.BlockSpec((1, d), lambda i, j: (0, 0)),
            pl.BlockSpec((1, d), lambda i, j: (0, 0)),
        ],
        out_specs=[pl.BlockSpec((tm, d), lambda i, j: (i, 0)), pl.BlockSpec((tm, d), lambda i, j: (i, 0))],
        out_shape=[jax.ShapeDtypeStruct((m, d), F32), jax.ShapeDtypeStruct((m, d), BF16)],
        scratch_shapes=[pltpu.VMEM((tm, d), F32)],
        compiler_params=_cparams("parallel", "arbitrary"),
        name="ffn_ln",
    )(x_bf, w1, w3, w2, resid, g.reshape(1, d), b.reshape(1, d))


def _cast_kernel(w_ref, o_ref):
    o_ref[...] = w_ref[...].astype(o_ref.dtype)


def _layer_weights_bf16(w, layer):
    cols = w.shape[-1]
    rows = math.prod(w.shape[1:-1])
    flat = w.reshape(w.shape[0] * rows, cols)
    tr = _tile(rows, (512, 256, 128))
    steps = rows // tr
    out = pl.pallas_call(
        _cast_kernel,
        grid=(steps,),
        in_specs=[pl.BlockSpec((tr, cols), lambda i: (layer * steps + i, 0))],
        out_specs=pl.BlockSpec((tr, cols), lambda i: (i, 0)),
        out_shape=jax.ShapeDtypeStruct((rows, cols), BF16),
        compiler_params=_cparams("parallel"),
        name="cast_weights",
    )(flat)
    return out.reshape(w.shape[1:])


def _moe_ffn_kernel(te_ref, na_ref, x_ref, w1_ref, w3_ref, w2_ref, *rest):
    o_ref, acc_ref = rest[-2:]
    i = pl.program_id(0)
    j = pl.program_id(1)
    active = i < na_ref[0]

    @pl.when(active & (j == 0))
    def _():
        acc_ref[...] = jnp.zeros_like(acc_ref)

    @pl.when(active)
    def _():
        _swiglu_accumulate(x_ref, w1_ref[0], w3_ref[0], w2_ref[0], acc_ref)

    @pl.when(active & (j == pl.num_programs(1) - 1))
    def _():
        o_ref[...] = acc_ref[...].astype(o_ref.dtype)


def _moe_ffn(x_sorted, tile_expert, n_active, w1, w3, w2, tm, tile_offset, total_rows, y_prev=None):
    p, d = x_sorted.shape
    ff = w1.shape[2]
    tf = _tile(ff, (1408, 256, 128))
    nf = ff // tf

    def row(i, na):
        return jnp.maximum(jnp.minimum(i, na[0] - 1), 0)

    def col(i, j, na):
        return jnp.where(i < na[0], j, nf - 1)

    operands = [tile_expert, n_active, x_sorted, w1, w3, w2]
    in_specs = [
        pl.BlockSpec((tm, d), lambda i, j, te, na: (row(i, na), 0)),
        pl.BlockSpec((1, d, tf), lambda i, j, te, na: (te[row(i, na)], 0, col(i, j, na))),
        pl.BlockSpec((1, d, tf), lambda i, j, te, na: (te[row(i, na)], 0, col(i, j, na))),
        pl.BlockSpec((1, tf, d), lambda i, j, te, na: (te[row(i, na)], col(i, j, na), 0)),
    ]
    aliases = {}
    if y_prev is not None:
        in_specs.append(pl.BlockSpec(memory_space=pl.ANY))
        aliases = {len(operands): 0}
        operands.append(y_prev)
    grid_spec = pltpu.PrefetchScalarGridSpec(
        num_scalar_prefetch=2,
        grid=(p // tm, nf),
        in_specs=in_specs,
        out_specs=pl.BlockSpec((tm, d), lambda i, j, te, na: (tile_offset + row(i, na), 0)),
        scratch_shapes=[pltpu.VMEM((tm, d), F32)],
    )
    return pl.pallas_call(
        _moe_ffn_kernel,
        grid_spec=grid_spec,
        out_shape=jax.ShapeDtypeStruct((total_rows, d), BF16),
        input_output_aliases=aliases,
        compiler_params=_cparams("arbitrary", "arbitrary"),
        name="moe_ffn",
    )(*operands)


def _router_kernel(x_ref, r_ref, gate_ref, idx_ref):
    logits = jnp.dot(x_ref[...], r_ref[...], precision=lax.Precision.HIGHEST, preferred_element_type=F32)
    lane = lax.broadcasted_iota(jnp.int32, logits.shape, 1)
    neg = jnp.float32(-jnp.inf)
    logits = jnp.where(lane < N_EXPERTS, logits, neg)
    v1 = jnp.max(logits, axis=-1, keepdims=True)
    i1 = jnp.min(jnp.where(logits == v1, lane, LANES), axis=-1, keepdims=True)
    rest = jnp.where(lane == i1, neg, logits)
    v2 = jnp.max(rest, axis=-1, keepdims=True)
    i2 = jnp.min(jnp.where(rest == v2, lane, LANES), axis=-1, keepdims=True)
    e = jnp.exp(v2 - v1)
    g1 = 1.0 / (1.0 + e)
    g2 = e / (1.0 + e)
    gate_ref[...] = jnp.where(lane == 0, g1, jnp.where(lane == 1, g2, 0.0))
    idx_ref[...] = jnp.where(lane == 0, i1, jnp.where(lane == 1, i2, 0))


def _router(x_f32, router_padded):
    m, d = x_f32.shape
    tm = _tile(m, (1024, 512, 256))
    return pl.pallas_call(
        _router_kernel,
        grid=(m // tm,),
        in_specs=[pl.BlockSpec((tm, d), lambda i: (i, 0)), pl.BlockSpec((d, LANES), lambda i: (0, 0))],
        out_specs=[pl.BlockSpec((tm, LANES), lambda i: (i, 0)), pl.BlockSpec((tm, LANES), lambda i: (i, 0))],
        out_shape=[jax.ShapeDtypeStruct((m, LANES), F32), jax.ShapeDtypeStruct((m, LANES), jnp.int32)],
        compiler_params=_cparams("parallel"),
        name="router",
    )(x_f32, router_padded)


def _moe_combine_kernel(resid_ref, y1_ref, y2_ref, gate_ref, g_ref, b_ref, of_ref, ob_ref, *, alpha):
    gates = gate_ref[...]
    y = gates[:, 0:1] * y1_ref[...].astype(F32) + gates[:, 1:2] * y2_ref[...].astype(F32)
    z = _layer_norm(alpha * resid_ref[...] + y, g_ref[...], b_ref[...])
    of_ref[...] = z
    ob_ref[...] = z.astype(BF16)


def _moe_combine(resid, y1, y2, gates, g, b, alpha):
    m, d = resid.shape
    tm = _tile(m, (512, 256))
    row = lambda i: (i, 0)
    fixed = lambda i: (0, 0)
    return pl.pallas_call(
        functools.partial(_moe_combine_kernel, alpha=alpha),
        grid=(m // tm,),
        in_specs=[pl.BlockSpec((tm, d), row), pl.BlockSpec((tm, d), row), pl.BlockSpec((tm, d), row),
                  pl.BlockSpec((tm, LANES), row), pl.BlockSpec((1, d), fixed), pl.BlockSpec((1, d), fixed)],
        out_specs=[pl.BlockSpec((tm, d), row), pl.BlockSpec((tm, d), row)],
        out_shape=[jax.ShapeDtypeStruct((m, d), F32), jax.ShapeDtypeStruct((m, d), BF16)],
        compiler_params=_cparams("parallel"),
        name="moe_combine",
    )(resid, y1, y2, gates, g.reshape(1, d), b.reshape(1, d))


def _hgrn_masks(c, levels):
    t = np.arange(c)[:, None]
    s = np.arange(c)[None, :]
    tri = np.stack([(s <= t), (s >= t)]).astype(np.float32)
    fwd, bwd = [], []
    for l in range(levels):
        same = (t >> (l + 1)) == (s >> (l + 1))
        t_up = ((t >> l) & 1) == 1
        s_up = ((s >> l) & 1) == 1
        fwd.append(same & t_up & ~s_up)
        bwd.append(same & ~t_up & s_up)
    fwd.append(t == s)
    bwd.append(t == s)
    up = np.stack([np.broadcast_to(((t >> l) & 1) == 1, (c, A_HEAD_DIM)) for l in range(levels)])
    sign = np.stack([np.where(up, 1.0, -1.0), np.where(up, -1.0, 1.0)]).astype(np.float32)
    return tri, np.stack([np.stack(fwd), np.stack(bwd)]).astype(np.float32), sign


def _segment_reference(x, level, forward):
    c, w = x.shape
    half = 1 << level
    seg = 2 * half
    idx = half - 1 if forward else half
    if seg >= SUBLANES:
        xr = x.reshape(c // seg, seg, w)
        return jnp.broadcast_to(xr[:, idx:idx + 1, :], xr.shape).reshape(c, w)
    x3 = x.reshape(c // SUBLANES, SUBLANES, w)
    sub = lax.broadcasted_iota(jnp.int32, x3.shape, 1)
    r3 = jnp.broadcast_to(x3[:, idx:idx + 1, :], x3.shape)
    for j in range(1, SUBLANES // seg):
        row = j * seg + idx
        r3 = jnp.where(sub >= j * seg, jnp.broadcast_to(x3[:, row:row + 1, :], x3.shape), r3)
    return r3.reshape(c, w)


def _hgrn_chunk(q, k, v, log_f, state_t, tri, masks, signs, forward):
    c = q.shape[0]
    x = jnp.dot(tri, log_f, precision=lax.Precision.HIGHEST, preferred_element_type=F32)
    scores = masks[HGRN_LEVELS] * _dot_nt(q.astype(BF16), k.astype(BF16))
    for level in range(HGRN_LEVELS):
        ref = _segment_reference(x, level, forward)
        sign = signs[level]
        decay = jnp.exp(sign * (x - ref))
        z = (jnp.where(sign > 0, q, k) * decay).astype(BF16)
        scores = scores + masks[level] * _dot_nt(z, z)
    x_end = x[c - 1:c, :] if forward else x[0:1, :]
    q_dec = (q * jnp.exp(x)).astype(BF16)
    o = _dot(scores.astype(BF16), v.astype(BF16)) + _dot_nt(q_dec, state_t.astype(BF16))
    k_dec = (k * jnp.exp(x_end - x)).astype(BF16)
    new_state_t = state_t * jnp.exp(x_end) + _dot_tn(v.astype(BF16), k_dec)
    return o, new_state_t


def _hgrn_kernel(q_ref, v_ref, g_ref, zf_ref, zb_ref, lb_ref, nw_ref, tri_ref, msk_ref, sgn_ref, o_ref, acc_ref, *,
                 n_chunks):
    c = HGRN_CHUNK
    hd = A_HEAD_DIM

    def gates(z, lb):
        e = jnp.exp(-jnp.abs(z))
        inv = 1.0 / (1.0 + e)
        pos = z >= 0
        sig = jnp.where(pos, inv, e * inv)
        sig_neg = jnp.where(pos, e * inv, inv)
        f = lb + (1.0 - lb) * sig
        return jnp.log(jnp.maximum(f, MIN_FORGET)), (1.0 - lb) * sig_neg

    def load(ref, c0, h):
        return ref[0, pl.ds(c0, c), h * hd:(h + 1) * hd].astype(F32)

    def emit(c0, h, tot):
        ms = jnp.mean(tot * tot, axis=-1, keepdims=True)
        g = load(g_ref, c0, h)
        out = tot * lax.rsqrt(ms + RMS_EPS) * nw_ref[...] * (g * _sigmoid(g))
        o_ref[0, pl.ds(c0, c), h * hd:(h + 1) * hd] = out.astype(o_ref.dtype)

    def one_direction(h, c0, z_ref, state_t, direction):
        lb = lb_ref[:, h * hd:(h + 1) * hd]
        log_f, k = gates(load(z_ref, c0, h), lb)
        masks = [msk_ref[direction, l] for l in range(HGRN_LEVELS + 1)]
        signs = [sgn_ref[direction, l] for l in range(HGRN_LEVELS)]
        return _hgrn_chunk(load(q_ref, c0, h), k, load(v_ref, c0, h), log_f, state_t, tri_ref[direction], masks,
                           signs, direction == 0)

    def step(i, states, finalize):
        cf = pl.multiple_of(i * c, c)
        cb = pl.multiple_of((n_chunks - 1 - i) * c, c)
        new_states = []
        for h in range(HGRN_HEADS_PER_STEP):
            o_f, st_f = one_direction(h, cf, zf_ref, states[2 * h], 0)
            o_b, st_b = one_direction(h, cb, zb_ref, states[2 * h + 1], 1)
            cols = slice(h * hd, (h + 1) * hd)
            if finalize:
                emit(cf, h, acc_ref[pl.ds(cf, c), cols] + o_f)
                emit(cb, h, acc_ref[pl.ds(cb, c), cols] + o_b)
            else:
                acc_ref[pl.ds(cf, c), cols] = o_f
                acc_ref[pl.ds(cb, c), cols] = o_b
            new_states += [st_f, st_b]
        return tuple(new_states)

    zero = jnp.zeros((hd, hd), F32)
    half = n_chunks // 2
    states = lax.fori_loop(0, half, functools.partial(step, finalize=False), (zero,) * (2 * HGRN_HEADS_PER_STEP))
    lax.fori_loop(half, n_chunks, functools.partial(step, finalize=True), states)


def _hgrn(qvg, zz, lb, norm_w, n_heads):
    bsz, s, _ = qvg.shape
    c = HGRN_CHUNK
    assert s % (2 * c) == 0 and n_heads % HGRN_HEADS_PER_STEP == 0
    tri, masks, signs = _hgrn_masks(c, HGRN_LEVELS)
    hd = A_HEAD_DIM
    wide = HGRN_HEADS_PER_STEP * hd
    n_groups = n_heads // HGRN_HEADS_PER_STEP
    col = lambda off: pl.BlockSpec((1, s, wide), lambda b, h: (b, 0, off + h))
    const = lambda a: pl.BlockSpec(a.shape, lambda b, h: (0,) * a.ndim)
    return pl.pallas_call(
        functools.partial(_hgrn_kernel, n_chunks=s // c),
        grid=(bsz, n_groups),
        in_specs=[
            col(0), col(n_groups), col(2 * n_groups), col(0), col(n_groups),
            pl.BlockSpec((1, wide), lambda b, h: (0, h)),
            pl.BlockSpec((1, hd), lambda b, h: (0, 0)),
            const(tri), const(masks), const(signs),
        ],
        out_specs=pl.BlockSpec((1, s, wide), lambda b, h: (b, 0, h)),
        out_shape=jax.ShapeDtypeStruct((bsz, s, n_heads * hd), BF16),
        scratch_shapes=[pltpu.VMEM((s, wide), F32)],
        compiler_params=_cparams("parallel", "parallel"),
        name="hgrn2",
    )(qvg, qvg, qvg, zz, zz, lb.reshape(1, -1), norm_w.reshape(1, hd), jnp.asarray(tri), jnp.asarray(masks),
      jnp.asarray(signs))


def _dilated_kernel(q_ref, k_ref, v_ref, o_ref, qf_ref, kf_ref, vf_ref, oc_ref, lc_ref, *, seq):
    qf_ref[...] = q_ref[0].astype(F32)
    kf_ref[...] = k_ref[0].astype(F32)
    vf_ref[...] = v_ref[0].astype(F32)
    n_cfg = len(B_CONFIGS)
    for ci, (_, dil) in enumerate(B_CONFIGS):
        length = seq // dil
        tq = min(LANES, length)
        win = min(length, tq + 2 * BAND_RADIUS)
        head0 = lax.broadcasted_iota(jnp.int32, (tq, LANES), 1) < B_HEAD_DIM
        rel = lax.broadcasted_iota(jnp.int32, (tq, win), 1) - lax.broadcasted_iota(jnp.int32, (tq, win), 0)

        def rows(first, size, dil=dil):
            return pl.ds(first, size) if dil == 1 else pl.ds(first, size, stride=dil)

        def block(t, carry, ci=ci, dil=dil, length=length, tq=tq, win=win, head0=head0, rel=rel, rows=rows):
            res = t % dil
            q0 = (t // dil) * tq
            start = jnp.clip(q0 - BAND_RADIUS, 0, length - win)
            valid = jnp.abs(rel + (start - q0)) <= BAND_RADIUS
            q_rows = rows(q0 * dil + res, tq)
            k_rows = rows(start * dil + res, win)
            q = qf_ref[q_rows, :].astype(BF16)
            kw = kf_ref[k_rows, :].astype(BF16)
            vw = vf_ref[k_rows, :].astype(BF16)

            def one_head(mask):
                s = _dot_nt(jnp.where(mask, q, jnp.zeros_like(q)), kw)
                s = jnp.where(valid, s, MASK_VALUE)
                m = jnp.max(s, axis=-1, keepdims=True)
                p = jnp.exp2(s - m)
                l = jnp.sum(p, axis=-1, keepdims=True)
                return _dot(p.astype(BF16), vw) / l, m + jnp.log(l) * LOG2_E

            oa, la = one_head(head0)
            ob, lb = one_head(jnp.logical_not(head0))
            oc_ref[ci, q_rows, :] = jnp.where(head0, oa, ob)
            lc_ref[ci, q_rows, :] = jnp.where(head0, la, lb)
            return carry

        lax.fori_loop(0, dil * (length // tq), block, 0, unroll=8)

    tmix = min(256, seq)

    def mix(i, carry):
        r0 = pl.multiple_of(i * tmix, tmix)
        lses = [lc_ref[c, pl.ds(r0, tmix), :] for c in range(n_cfg)]
        top = lses[0]
        for l in lses[1:]:
            top = jnp.maximum(top, l)
        num = jnp.zeros((tmix, LANES), F32)
        den = jnp.zeros((tmix, LANES), F32)
        for c in range(n_cfg):
            w = jnp.exp2(lses[c] - top)
            num = num + w * oc_ref[c, pl.ds(r0, tmix), :]
            den = den + w
        o_ref[0, pl.ds(r0, tmix), :] = (num / den).astype(o_ref.dtype)
        return carry

    lax.fori_loop(0, seq // tmix, mix, 0)


def _dilated_attention(qk, v):
    bsz, s, w = v.shape
    n_pairs = w // LANES
    n_cfg = len(B_CONFIGS)
    return pl.pallas_call(
        functools.partial(_dilated_kernel, seq=s),
        grid=(bsz, n_pairs),
        in_specs=[
            pl.BlockSpec((1, s, LANES), lambda b, h: (b, 0, h)),
            pl.BlockSpec((1, s, LANES), lambda b, h: (b, 0, n_pairs + h)),
            pl.BlockSpec((1, s, LANES), lambda b, h: (b, 0, h)),
        ],
        out_specs=pl.BlockSpec((1, s, LANES), lambda b, h: (b, 0, h)),
        out_shape=jax.ShapeDtypeStruct((bsz, s, w), BF16),
        scratch_shapes=[pltpu.VMEM((s, LANES), F32)] * 3 + [pltpu.VMEM((n_cfg, s, LANES), F32)] * 2,
        compiler_params=_cparams("parallel", "parallel"),
        name="dilated_attn",
    )(qk, qk, v)


def _diff_kernel(q_ref, k_ref, v_ref, lam_ref, sub_ref, o_ref, *, tk, lambda_init):
    q = q_ref[0]
    tq = q.shape[0]
    s_len = k_ref.shape[1]
    lane = lax.broadcasted_iota(jnp.int32, q.shape, 1)
    zero = jnp.zeros_like(q)
    qs = (jnp.where(lane < C_HEAD_DIM, q, zero), jnp.where(lane >= C_HEAD_DIM, q, zero))
    tiles = [(t * LANES, (t + 1) * LANES) for t in range(tk // LANES)]
    m = [jnp.full((tq, 1), -jnp.inf, F32)] * 2
    l = [jnp.zeros((tq, 1), F32)] * 2
    acc = [jnp.zeros((tq, LANES), F32)] * 2
    for c in range(s_len // tk):
        lo, hi = c * tk, (c + 1) * tk
        for h in range(2):
            s = _dot_nt(qs[h], k_ref[0, lo:hi, :])
            m_tile = s[:, 0:LANES]
            for a, b in tiles[1:]:
                m_tile = jnp.maximum(m_tile, s[:, a:b])
            m_new = jnp.maximum(m[h], jnp.max(m_tile, axis=-1, keepdims=True))
            alpha = jnp.exp2(m[h] - m_new)
            p = jnp.exp2(s - m_new)
            l_tile = p[:, 0:LANES]
            for a, b in tiles[1:]:
                l_tile = l_tile + p[:, a:b]
            l[h] = alpha * l[h] + jnp.sum(l_tile, axis=-1, keepdims=True)
            acc[h] = alpha * acc[h] + _dot(p.astype(BF16), v_ref[0, lo:hi, :])
            m[h] = m_new
    outs = (acc[0] / l[0], acc[1] / l[1])
    lp = lam_ref[...]
    lam = (jnp.exp(jnp.sum(lp[0:1] * lp[1:2], axis=-1, keepdims=True))
           - jnp.exp(jnp.sum(lp[2:3] * lp[3:4], axis=-1, keepdims=True)) + lambda_init)
    o = outs[0] - lam * outs[1]
    ms_o = jnp.mean(o * o, axis=-1, keepdims=True)
    o_ref[0] = (o * lax.rsqrt(ms_o + RMS_EPS) * sub_ref[...] * (1.0 - lambda_init)).astype(o_ref.dtype)


def _diff_attention(qk, v, lam_params, subln_w, lambda_init):
    bsz, s, w = v.shape
    n_heads = w // LANES
    tq = _tile(s, (1024, 512, 256, 128))
    tk = _tile(s, (2048, 1024, 512, 256, 128))
    return pl.pallas_call(
        functools.partial(_diff_kernel, tk=tk, lambda_init=lambda_init),
        grid=(bsz, n_heads, s // tq),
        in_specs=[
            pl.BlockSpec((1, tq, LANES), lambda b, h, i: (b, i, h)),
            pl.BlockSpec((1, s, LANES), lambda b, h, i: (b, 0, n_heads + h)),
            pl.BlockSpec((1, s, LANES), lambda b, h, i: (b, 0, h)),
            pl.BlockSpec(lam_params.shape, lambda b, h, i: (0, 0)),
            pl.BlockSpec((1, LANES), lambda b, h, i: (0, 0)),
        ],
        out_specs=pl.BlockSpec((1, tq, LANES), lambda b, h, i: (b, i, h)),
        out_shape=jax.ShapeDtypeStruct((bsz, s, w), BF16),
        compiler_params=_cparams("parallel", "parallel", "arbitrary"),
        name="diff_attn",
    )(qk, qk, v, lam_params, subln_w.reshape(1, LANES))


def _rope_tables(seq, width):
    half = B_HEAD_DIM // 2
    inv = ROPE_THETA ** (-jnp.arange(0, B_HEAD_DIM, 2, dtype=F32) / B_HEAD_DIM)
    ang = jnp.arange(seq, dtype=F32)[:, None] * inv[None, :]
    cos, sin = jnp.cos(ang), jnp.sin(ang)
    reps = width // B_HEAD_DIM
    assert half * 2 == B_HEAD_DIM
    return jnp.tile(jnp.concatenate([cos, cos], axis=1), (1, reps)), jnp.tile(jnp.concatenate([-sin, sin], axis=1), (1, reps))


def _even_layer(x_f, x_b, bsz, seq, w_in, lb, norm_w, w_out, ln1, w1, w3, w2, layer, ln2, rope, alpha):
    d = x_f.shape[1]
    aw = d // 2
    n_heads_a = aw // A_HEAD_DIM
    w_in = w_in.astype(BF16)
    cols = lambda a, b: w_in[:, a * aw:b * aw]
    qvg = _proj(x_b, jnp.concatenate([cols(0, 1), cols(3, 5)], axis=1), BF16)
    zz = _proj(x_b, cols(1, 3), F32)
    cos_t, sin_t = rope
    scale_row = jnp.concatenate([jnp.full((1, aw), LOG2_E * B_HEAD_DIM ** -0.5, F32), jnp.ones((1, aw), F32)], axis=1)
    qk = _proj_rope(x_b, cols(5, 7), cos_t, sin_t, scale_row, seq)
    vb = _proj(x_b, cols(7, 8), BF16)
    oa = _hgrn(qvg.reshape(bsz, seq, -1), zz.reshape(bsz, seq, -1), lb, norm_w, n_heads_a)
    ob = _dilated_attention(qk.reshape(bsz, seq, 2 * aw), vb.reshape(bsz, seq, aw))
    w_out = w_out.astype(BF16)
    x_f, x_b = _out_ln([oa.reshape(bsz * seq, aw), ob.reshape(bsz * seq, aw)], [w_out[:aw], w_out[aw:]], x_f,
                       ln1[0], ln1[1], alpha)
    return _ffn_ln(x_b, x_f, w1[layer].astype(BF16), w3[layer].astype(BF16), w2[layer].astype(BF16), ln2[0], ln2[1],
                   alpha)


def _moe_dispatch(idx, n_tokens, tm):
    e_flat = idx[:, :2].reshape(-1)
    onehot = (e_flat[None, :] == jnp.arange(N_EXPERTS, dtype=jnp.int32)[:, None]).astype(jnp.int32)
    counts = jnp.sum(onehot, axis=1)
    tiles = (counts + tm - 1) // tm
    tile_end = jnp.cumsum(tiles)
    group_start = (tile_end - tiles) * tm
    dest = jnp.sum((jnp.cumsum(onehot, axis=1) - 1 + group_start[:, None]) * onehot, axis=0)
    n_tiles = (2 * n_tokens) // tm + N_EXPERTS
    src_tok = jnp.zeros((n_tiles * tm,), jnp.int32).at[dest].set(jnp.arange(2 * n_tokens, dtype=jnp.int32) // 2)
    tile_ids = jnp.arange(n_tiles, dtype=jnp.int32)
    tile_expert = jnp.minimum(jnp.sum((tile_ids[:, None] >= tile_end[None, :]).astype(jnp.int32), axis=1),
                              N_EXPERTS - 1)
    return src_tok, dest.reshape(n_tokens, 2), tile_expert, tile_end[-1:].astype(jnp.int32)


def _odd_layer(x_f, x_b, bsz, seq, w_in, lam_params, subln_w, w_out, ln1, router, w1, w3, w2, layer, ln2, rope,
               alpha, lambda_init):
    d = x_f.shape[1]
    n_tok = bsz * seq
    w_in = w_in.astype(BF16)
    cos_t, sin_t = rope
    scale_row = jnp.concatenate([jnp.full((1, d), LOG2_E * C_HEAD_DIM ** -0.5, F32), jnp.ones((1, d), F32)], axis=1)
    qk = _proj_rope(x_b, w_in[:, :2 * d], cos_t, sin_t, scale_row, seq)
    v = _proj(x_b, w_in[:, 2 * d:], BF16)
    o = _diff_attention(qk.reshape(bsz, seq, 2 * d), v.reshape(bsz, seq, d), lam_params.astype(F32), subln_w,
                        lambda_init)
    x_f, x_b = _out_ln([o.reshape(n_tok, d)], [w_out.astype(BF16)], x_f, ln1[0], ln1[1], alpha)
    router_padded = jnp.pad(router.astype(F32), ((0, 0), (0, LANES - N_EXPERTS)))
    gates, idx = _router(x_f, router_padded)
    tm = _tile(n_tok, (1024, 512, 256))
    src_tok, pos, tile_expert, n_active = _moe_dispatch(idx, n_tok, tm)
    n_tiles = src_tok.shape[0] // tm
    w1, w3, w2 = (_layer_weights_bf16(w, layer) for w in (w1, w3, w2))
    y = None
    for lo, hi in ((0, n_tiles // 2), (n_tiles // 2, n_tiles)):
        x_rows = jnp.take(x_b, src_tok[lo * tm:hi * tm], axis=0, mode="clip")
        n_act = jnp.clip(n_active - lo, 0, hi - lo)
        y = _moe_ffn(x_rows, tile_expert[lo:hi], n_act, w1, w3, w2, tm, lo, n_tiles * tm, y)
    y1 = jnp.take(y, pos[:, 0], axis=0, mode="clip")
    y2 = jnp.take(y, pos[:, 1], axis=0, mode="clip")
    return _moe_combine(x_f, y1, y2, gates, ln2[0], ln2[1], alpha)


def kernel(x, ev_w_in, ev_lb_logits, ev_hgrn_norm, ev_w_out, ev_ln1_g, ev_ln1_b, ev_w1, ev_w3, ev_w2, ev_ln2_g,
           ev_ln2_b, od_w_in, od_lambda, od_subln, od_w_out, od_ln1_g, od_ln1_b, od_router, od_w1, od_w3, od_w2,
           od_ln2_g, od_ln2_b):
    bsz, seq, d = x.shape
    depth = ev_w_in.shape[0] + od_w_in.shape[0]
    alpha = (2 * depth) ** 0.25
    rope = _rope_tables(seq, LANES)
    lb_soft = jax.nn.softmax(ev_lb_logits.astype(F32), axis=0)
    lower_bounds = jnp.cumsum(lb_soft, axis=0) - lb_soft[0]
    x_f = x.reshape(bsz * seq, d).astype(F32)
    x_b = x_f.astype(BF16)
    for layer in range(depth):
        j = layer // 2
        if layer % 2 == 0:
            x_f, x_b = _even_layer(x_f, x_b, bsz, seq, ev_w_in[j], lower_bounds[j], ev_hgrn_norm[j], ev_w_out[j],
                                   (ev_ln1_g[j], ev_ln1_b[j]), ev_w1, ev_w3, ev_w2, j,
                                   (ev_ln2_g[j], ev_ln2_b[j]), rope, alpha)
        else:
            lambda_init = 0.8 - 0.6 * math.exp(-0.3 * layer)
            x_f, x_b = _odd_layer(x_f, x_b, bsz, seq, od_w_in[j], od_lambda[j], od_subln[j], od_w_out[j],
                                  (od_ln1_g[j], od_ln1_b[j]), od_router[j], od_w1, od_w3, od_w2, j,
                                  (od_ln2_g[j], od_ln2_b[j]), rope, alpha, lambda_init)
    return x_f.reshape(bsz, seq, d).astype(x.dtype)
```

```python
import functools
import math

import numpy as np
import jax
import jax.numpy as jnp
from jax import lax
from jax.experimental import pallas as pl
from jax.experimental.pallas import tpu as pltpu

F32 = jnp.float32
BF16 = jnp.bfloat16

A_HEAD_DIM = 128
B_HEAD_DIM = 64
B_CONFIGS = ((128, 1), (512, 4), (2048, 16))
BAND_RADIUS = 64
C_HEAD_DIM = 64
N_EXPERTS = 8
ROPE_THETA = 10000.0
LN_EPS = 1e-5
RMS_EPS = 1e-5
MASK_VALUE = -1e30
MIN_FORGET = 1e-30
LOG2_E = math.log2(math.e)

LANES = 128
SUBLANES = 8
VMEM_LIMIT_BYTES = 56 * 1024 * 1024

ROW_CHAINS = 4
FFN_ROW_CHAINS = 2

HGRN_CHUNK = 128
HGRN_LEVELS = 7
HGRN_HEADS_PER_STEP = 2


def _cparams(*sem):
    return pltpu.CompilerParams(dimension_semantics=sem, vmem_limit_bytes=VMEM_LIMIT_BYTES)


def _tile(n, prefs):
    for p in prefs:
        if n % p == 0:
            return p
    return n


def _dot(a, b):
    return jnp.dot(a, b, preferred_element_type=F32)


def _dot_nt(a, b):
    return lax.dot_general(a, b, (((1,), (1,)), ((), ())), preferred_element_type=F32)


def _dot_tn(a, b):
    return lax.dot_general(a, b, (((0,), (0,)), ((), ())), preferred_element_type=F32)


def _sigmoid(x):
    return 1.0 / (1.0 + jnp.exp(-x))


def _layer_norm(y, g, b):
    mu = jnp.mean(y, axis=-1, keepdims=True)
    d = y - mu
    var = jnp.mean(d * d, axis=-1, keepdims=True)
    return d * lax.rsqrt(var + LN_EPS) * g + b


def _proj_kernel(x_ref, w_ref, o_ref):
    o_ref[...] = _dot(x_ref[...], w_ref[...]).astype(o_ref.dtype)


def _proj(x, w, out_dtype):
    m, k = x.shape
    n = w.shape[1]
    tm = _tile(m, (1024, 512, 256))
    tn = n if n <= 1536 else _tile(n, (1024, 512, 256, 128))
    return pl.pallas_call(
        _proj_kernel,
        grid=(m // tm, n // tn),
        in_specs=[pl.BlockSpec((tm, k), lambda i, j: (i, 0)), pl.BlockSpec((k, tn), lambda i, j: (0, j))],
        out_specs=pl.BlockSpec((tm, tn), lambda i, j: (i, j)),
        out_shape=jax.ShapeDtypeStruct((m, n), out_dtype),
        compiler_params=_cparams("parallel", "arbitrary"),
        name="proj",
    )(x, w)


def _proj_rope_kernel(x_ref, w_ref, cos_ref, sin_ref, scale_ref, o_ref):
    tm, tn = o_ref.shape
    sub = tm // ROW_CHAINS
    w = w_ref[...]
    lane = lax.broadcasted_iota(jnp.int32, (sub, tn), 1)
    first_half = (lane % B_HEAD_DIM) < (B_HEAD_DIM // 2)
    reps = tn // cos_ref.shape[1]
    for k in range(ROW_CHAINS):
        rows = pl.ds(k * sub, sub)
        acc = _dot(x_ref[rows, :], w)
        partner = jnp.where(first_half, pltpu.roll(acc, tn - B_HEAD_DIM // 2, 1), pltpu.roll(acc, B_HEAD_DIM // 2, 1))
        cos = jnp.tile(cos_ref[rows, :], (1, reps))
        sin = jnp.tile(sin_ref[rows, :], (1, reps))
        o_ref[rows, :] = ((acc * cos + partner * sin) * scale_ref[...]).astype(o_ref.dtype)


def _proj_rope(x, w, cos_t, sin_t, scale_row, seq):
    m, k = x.shape
    n = w.shape[1]
    tm = _tile(seq, (1024, 512, 256))
    tn = _tile(n, (1024, 512, 256, 128))
    nsb = seq // tm
    return pl.pallas_call(
        _proj_rope_kernel,
        grid=(m // tm, n // tn),
        in_specs=[
            pl.BlockSpec((tm, k), lambda i, j: (i, 0)),
            pl.BlockSpec((k, tn), lambda i, j: (0, j)),
            pl.BlockSpec((tm, LANES), lambda i, j: (i % nsb, 0)),
            pl.BlockSpec((tm, LANES), lambda i, j: (i % nsb, 0)),
            pl.BlockSpec((1, tn), lambda i, j: (0, j)),
        ],
        out_specs=pl.BlockSpec((tm, tn), lambda i, j: (i, j)),
        out_shape=jax.ShapeDtypeStruct((m, n), BF16),
        compiler_params=_cparams("parallel", "arbitrary"),
        name="proj_rope",
    )(x, w, cos_t, sin_t, scale_row)


def _out_ln_kernel(*refs, n_in, alpha):
    xs = refs[:n_in]
    ws = refs[n_in:2 * n_in]
    resid_ref, g_ref, b_ref, of_ref, ob_ref = refs[2 * n_in:]
    sub = of_ref.shape[0] // ROW_CHAINS
    for k in range(ROW_CHAINS):
        rows = pl.ds(k * sub, sub)
        acc = _dot(xs[0][rows, :], ws[0][...])
        for x_ref, w_ref in zip(xs[1:], ws[1:]):
            acc = acc + _dot(x_ref[rows, :], w_ref[...])
        z = _layer_norm(alpha * resid_ref[rows, :] + acc, g_ref[...], b_ref[...])
        of_ref[rows, :] = z
        ob_ref[rows, :] = z.astype(BF16)


def _out_ln(xs, ws, resid, g, b, alpha):
    m, d = resid.shape
    tm = _tile(m, (1024, 512, 256))
    n_in = len(xs)
    in_specs = [pl.BlockSpec((tm, x.shape[1]), lambda i: (i, 0)) for x in xs]
    in_specs += [pl.BlockSpec(w.shape, lambda i: (0, 0)) for w in ws]
    in_specs += [pl.BlockSpec((tm, d), lambda i: (i, 0)), pl.BlockSpec((1, d), lambda i: (0, 0)),
                 pl.BlockSpec((1, d), lambda i: (0, 0))]
    return pl.pallas_call(
        functools.partial(_out_ln_kernel, n_in=n_in, alpha=alpha),
        grid=(m // tm,),
        in_specs=in_specs,
        out_specs=[pl.BlockSpec((tm, d), lambda i: (i, 0)), pl.BlockSpec((tm, d), lambda i: (i, 0))],
        out_shape=[jax.ShapeDtypeStruct((m, d), F32), jax.ShapeDtypeStruct((m, d), BF16)],
        compiler_params=_cparams("parallel"),
        name="out_ln",
    )(*xs, *ws, resid, g.reshape(1, d), b.reshape(1, d))


def _swiglu_accumulate(x_ref, w1, w3, w2, acc_ref):
    sub = x_ref.shape[0] // FFN_ROW_CHAINS
    for k in range(FFN_ROW_CHAINS):
        rows = pl.ds(k * sub, sub)
        h1 = _dot(x_ref[rows, :], w1)
        h3 = _dot(x_ref[rows, :], w3)
        h = (h1 * _sigmoid(h1)) * h3
        acc_ref[rows, :] += _dot(h.astype(BF16), w2)


def _ffn_ln_kernel(x_ref, w1_ref, w3_ref, w2_ref, resid_ref, g_ref, b_ref, of_ref, ob_ref, acc_ref, *, alpha):
    j = pl.program_id(1)

    @pl.when(j == 0)
    def _():
        acc_ref[...] = jnp.zeros_like(acc_ref)

    _swiglu_accumulate(x_ref, w1_ref[...], w3_ref[...], w2_ref[...], acc_ref)

    @pl.when(j == pl.num_programs(1) - 1)
    def _():
        z = _layer_norm(alpha * resid_ref[...] + acc_ref[...], g_ref[...], b_ref[...])
        of_ref[...] = z
        ob_ref[...] = z.astype(BF16)


def _ffn_ln(x_bf, resid, w1, w3, w2, g, b, alpha):
    m, d = resid.shape
    ff = w1.shape[1]
    tm = _tile(m, (1024, 512, 256))
    tf = _tile(ff, (1408, 256, 128))
    return pl.pallas_call(
        functools.partial(_ffn_ln_kernel, alpha=alpha),
        grid=(m // tm, ff // tf),
        in_specs=[
            pl.BlockSpec((tm, d), lambda i, j: (i, 0)),
            pl.BlockSpec((d, tf), lambda i, j: (0, j)),
            pl.BlockSpec((d, tf), lambda i, j: (0, j)),
            pl.BlockSpec((tf, d), lambda i, j: (j, 0)),
            pl.BlockSpec((tm, d), lambda i, j: (i, 0)),
            pl.BlockSpec((1, d), lambda i, j: (0, 0)),
            pl.BlockSpec((1, d), lambda i, j: (0, 0)),
        ],
        out_specs=[pl.BlockSpec((tm, d), lambda i, j: (i, 0)), pl.BlockSpec((tm, d), lambda i, j: (i, 0))],
        out_shape=[jax.ShapeDtypeStruct((m, d), F32), jax.ShapeDtypeStruct((m, d), BF16)],
        scratch_shapes=[pltpu.VMEM((tm, d), F32)],
        compiler_params=_cparams("parallel", "arbitrary"),
        name="ffn_ln",
    )(x_bf, w1, w3, w2, resid, g.reshape(1, d), b.reshape(1, d))


def _cast_kernel(w_ref, o_ref):
    o_ref[...] = w_ref[...].astype(o_ref.dtype)


def _layer_weights_bf16(w, layer):
    cols = w.shape[-1]
    rows = math.prod(w.shape[1:-1])
    flat = w.reshape(w.shape[0] * rows, cols)
    tr = _tile(rows, (512, 256, 128))
    steps = rows // tr
    out = pl.pallas_call(
        _cast_kernel,
        grid=(steps,),
        in_specs=[pl.BlockSpec((tr, cols), lambda i: (layer * steps + i, 0))],
        out_specs=pl.BlockSpec((tr, cols), lambda i: (i, 0)),
        out_shape=jax.ShapeDtypeStruct((rows, cols), BF16),
        compiler_params=_cparams("parallel"),
        name="cast_weights",
    )(flat)
    return out.reshape(w.shape[1:])


def _moe_ffn_kernel(te_ref, na_ref, x_ref, w1_ref, w3_ref, w2_ref, *rest):
    o_ref, acc_ref = rest[-2:]
    i = pl.program_id(0)
    j = pl.program_id(1)
    active = i < na_ref[0]

    @pl.when(active & (j == 0))
    def _():
        acc_ref[...] = jnp.zeros_like(acc_ref)

    @pl.when(active)
    def _():
        _swiglu_accumulate(x_ref, w1_ref[0], w3_ref[0], w2_ref[0], acc_ref)

    @pl.when(active & (j == pl.num_programs(1) - 1))
    def _():
        o_ref[...] = acc_ref[...].astype(o_ref.dtype)


def _moe_ffn(x_sorted, tile_expert, n_active, w1, w3, w2, tm, tile_offset, total_rows, y_prev=None):
    p, d = x_sorted.shape
    ff = w1.shape[2]
    tf = _tile(ff, (1408, 256, 128))
    nf = ff // tf

    def row(i, na):
        return jnp.maximum(jnp.minimum(i, na[0] - 1), 0)

    def col(i, j, na):
        return jnp.where(i < na[0], j, nf - 1)

    operands = [tile_expert, n_active, x_sorted, w1, w3, w2]
    in_specs = [
        pl.BlockSpec((tm, d), lambda i, j, te, na: (row(i, na), 0)),
        pl.BlockSpec((1, d, tf), lambda i, j, te, na: (te[row(i, na)], 0, col(i, j, na))),
        pl.BlockSpec((1, d, tf), lambda i, j, te, na: (te[row(i, na)], 0, col(i, j, na))),
        pl.BlockSpec((1, tf, d), lambda i, j, te, na: (te[row(i, na)], col(i, j, na), 0)),
    ]
    aliases = {}
    if y_prev is not None:
        in_specs.append(pl.BlockSpec(memory_space=pl.ANY))
        aliases = {len(operands): 0}
        operands.append(y_prev)
    grid_spec = pltpu.PrefetchScalarGridSpec(
        num_scalar_prefetch=2,
        grid=(p // tm, nf),
        in_specs=in_specs,
        out_specs=pl.BlockSpec((tm, d), lambda i, j, te, na: (tile_offset + row(i, na), 0)),
        scratch_shapes=[pltpu.VMEM((tm, d), F32)],
    )
    return pl.pallas_call(
        _moe_ffn_kernel,
        grid_spec=grid_spec,
        out_shape=jax.ShapeDtypeStruct((total_rows, d), BF16),
        input_output_aliases=aliases,
        compiler_params=_cparams("arbitrary", "arbitrary"),
        name="moe_ffn",
    )(*operands)


def _router_kernel(x_ref, r_ref, gate_ref, idx_ref):
    logits = jnp.dot(x_ref[...], r_ref[...], precision=lax.Precision.HIGHEST, preferred_element_type=F32)
    lane = lax.broadcasted_iota(jnp.int32, logits.shape, 1)
    neg = jnp.float32(-jnp.inf)
    logits = jnp.where(lane < N_EXPERTS, logits, neg)
    v1 = jnp.max(logits, axis=-1, keepdims=True)
    i1 = jnp.min(jnp.where(logits == v1, lane, LANES), axis=-1, keepdims=True)
    rest = jnp.where(lane == i1, neg, logits)
    v2 = jnp.max(rest, axis=-1, keepdims=True)
    i2 = jnp.min(jnp.where(rest == v2, lane, LANES), axis=-1, keepdims=True)
    e = jnp.exp(v2 - v1)
    g1 = 1.0 / (1.0 + e)
    g2 = e / (1.0 + e)
    gate_ref[...] = jnp.where(lane == 0, g1, jnp.where(lane == 1, g2, 0.0))
    idx_ref[...] = jnp.where(lane == 0, i1, jnp.where(lane == 1, i2, 0))


def _router(x_f32, router_padded):
    m, d = x_f32.shape
    tm = _tile(m, (1024, 512, 256))
    return pl.pallas_call(
        _router_kernel,
        grid=(m // tm,),
        in_specs=[pl.BlockSpec((tm, d), lambda i: (i, 0)), pl.BlockSpec((d, LANES), lambda i: (0, 0))],
        out_specs=[pl.BlockSpec((tm, LANES), lambda i: (i, 0)), pl.BlockSpec((tm, LANES), lambda i: (i, 0))],
        out_shape=[jax.ShapeDtypeStruct((m, LANES), F32), jax.ShapeDtypeStruct((m, LANES), jnp.int32)],
        compiler_params=_cparams("parallel"),
        name="router",
    )(x_f32, router_padded)


def _moe_combine_kernel(resid_ref, y1_ref, y2_ref, gate_ref, g_ref, b_ref, of_ref, ob_ref, *, alpha):
    gates = gate_ref[...]
    y = gates[:, 0:1] * y1_ref[...].astype(F32) + gates[:, 1:2] * y2_ref[...].astype(F32)
    z = _layer_norm(alpha * resid_ref[...] + y, g_ref[...], b_ref[...])
    of_ref[...] = z
    ob_ref[...] = z.astype(BF16)


def _moe_combine(resid, y1, y2, gates, g, b, alpha):
    m, d = resid.shape
    tm = _tile(m, (512, 256))
    row = lambda i: (i, 0)
    fixed = lambda i: (0, 0)
    return pl.pallas_call(
        functools.partial(_moe_combine_kernel, alpha=alpha),
        grid=(m // tm,),
        in_specs=[pl.BlockSpec((tm, d), row), pl.BlockSpec((tm, d), row), pl.BlockSpec((tm, d), row),
                  pl.BlockSpec((tm, LANES), row), pl.BlockSpec((1, d), fixed), pl.BlockSpec((1, d), fixed)],
        out_specs=[pl.BlockSpec((tm, d), row), pl.BlockSpec((tm, d), row)],
        out_shape=[jax.ShapeDtypeStruct((m, d), F32), jax.ShapeDtypeStruct((m, d), BF16)],
        compiler_params=_cparams("parallel"),
        name="moe_combine",
    )(resid, y1, y2, gates, g.reshape(1, d), b.reshape(1, d))


def _hgrn_masks(c, levels):
    t = np.arange(c)[:, None]
    s = np.arange(c)[None, :]
    tri = np.stack([(s <= t), (s >= t)]).astype(np.float32)
    fwd, bwd = [], []
    for l in range(levels):
        same = (t >> (l + 1)) == (s >> (l + 1))
        t_up = ((t >> l) & 1) == 1
        s_up = ((s >> l) & 1) == 1
        fwd.append(same & t_up & ~s_up)
        bwd.append(same & ~t_up & s_up)
    fwd.append(t == s)
    bwd.append(t == s)
    up = np.stack([np.broadcast_to(((t >> l) & 1) == 1, (c, A_HEAD_DIM)) for l in range(levels)])
    sign = np.stack([np.where(up, 1.0, -1.0), np.where(up, -1.0, 1.0)]).astype(np.float32)
    return tri, np.stack([np.stack(fwd), np.stack(bwd)]).astype(np.float32), sign


def _segment_reference(x, level, forward):
    c, w = x.shape
    half = 1 << level
    seg = 2 * half
    idx = half - 1 if forward else half
    if seg >= SUBLANES:
        xr = x.reshape(c // seg, seg, w)
        return jnp.broadcast_to(xr[:, idx:idx + 1, :], xr.shape).reshape(c, w)
    x3 = x.reshape(c // SUBLANES, SUBLANES, w)
    sub = lax.broadcasted_iota(jnp.int32, x3.shape, 1)
    r3 = jnp.broadcast_to(x3[:, idx:idx + 1, :], x3.shape)
    for j in range(1, SUBLANES // seg):
        row = j * seg + idx
        r3 = jnp.where(sub >= j * seg, jnp.broadcast_to(x3[:, row:row + 1, :], x3.shape), r3)
    return r3.reshape(c, w)


def _hgrn_chunk(q, k, v, log_f, state_t, tri, masks, signs, forward):
    c = q.shape[0]
    x = jnp.dot(tri, log_f, precision=lax.Precision.HIGHEST, preferred_element_type=F32)
    scores = masks[HGRN_LEVELS] * _dot_nt(q.astype(BF16), k.astype(BF16))
    for level in range(HGRN_LEVELS):
        ref = _segment_reference(x, level, forward)
        sign = signs[level]
        decay = jnp.exp(sign * (x - ref))
        z = (jnp.where(sign > 0, q, k) * decay).astype(BF16)
        scores = scores + masks[level] * _dot_nt(z, z)
    x_end = x[c - 1:c, :] if forward else x[0:1, :]
    q_dec = (q * jnp.exp(x)).astype(BF16)
    o = _dot(scores.astype(BF16), v.astype(BF16)) + _dot_nt(q_dec, state_t.astype(BF16))
    k_dec = (k * jnp.exp(x_end - x)).astype(BF16)
    new_state_t = state_t * jnp.exp(x_end) + _dot_tn(v.astype(BF16), k_dec)
    return o, new_state_t


def _hgrn_kernel(q_ref, v_ref, g_ref, zf_ref, zb_ref, lb_ref, nw_ref, tri_ref, msk_ref, sgn_ref, o_ref, acc_ref, *,
                 n_chunks):
    c = HGRN_CHUNK
    hd = A_HEAD_DIM

    def gates(z, lb):
        e = jnp.exp(-jnp.abs(z))
        inv = 1.0 / (1.0 + e)
        pos = z >= 0
        sig = jnp.where(pos, inv, e * inv)
        sig_neg = jnp.where(pos, e * inv, inv)
        f = lb + (1.0 - lb) * sig
        return jnp.log(jnp.maximum(f, MIN_FORGET)), (1.0 - lb) * sig_neg

    def load(ref, c0, h):
        return ref[0, pl.ds(c0, c), h * hd:(h + 1) * hd].astype(F32)

    def emit(c0, h, tot):
        ms = jnp.mean(tot * tot, axis=-1, keepdims=True)
        g = load(g_ref, c0, h)
        out = tot * lax.rsqrt(ms + RMS_EPS) * nw_ref[...] * (g * _sigmoid(g))
        o_ref[0, pl.ds(c0, c), h * hd:(h + 1) * hd] = out.astype(o_ref.dtype)

    def one_direction(h, c0, z_ref, state_t, direction):
        lb = lb_ref[:, h * hd:(h + 1) * hd]
        log_f, k = gates(load(z_ref, c0, h), lb)
        masks = [msk_ref[direction, l] for l in range(HGRN_LEVELS + 1)]
        signs = [sgn_ref[direction, l] for l in range(HGRN_LEVELS)]
        return _hgrn_chunk(load(q_ref, c0, h), k, load(v_ref, c0, h), log_f, state_t, tri_ref[direction], masks,
                           signs, direction == 0)

    def step(i, states, finalize):
        cf = pl.multiple_of(i * c, c)
        cb = pl.multiple_of((n_chunks - 1 - i) * c, c)
        new_states = []
        for h in range(HGRN_HEADS_PER_STEP):
            o_f, st_f = one_direction(h, cf, zf_ref, states[2 * h], 0)
            o_b, st_b = one_direction(h, cb, zb_ref, states[2 * h + 1], 1)
            cols = slice(h * hd, (h + 1) * hd)
            if finalize:
                emit(cf, h, acc_ref[pl.ds(cf, c), cols] + o_f)
                emit(cb, h, acc_ref[pl.ds(cb, c), cols] + o_b)
            else:
                acc_ref[pl.ds(cf, c), cols] = o_f
                acc_ref[pl.ds(cb, c), cols] = o_b
            new_states += [st_f, st_b]
        return tuple(new_states)

    zero = jnp.zeros((hd, hd), F32)
    half = n_chunks // 2
    states = lax.fori_loop(0, half, functools.partial(step, finalize=False), (zero,) * (2 * HGRN_HEADS_PER_STEP))
    lax.fori_loop(half, n_chunks, functools.partial(step, finalize=True), states)


def _hgrn(qvg, zz, lb, norm_w, n_heads):
    bsz, s, _ = qvg.shape
    c = HGRN_CHUNK
    assert s % (2 * c) == 0 and n_heads % HGRN_HEADS_PER_STEP == 0
    tri, masks, signs = _hgrn_masks(c, HGRN_LEVELS)
    hd = A_HEAD_DIM
    wide = HGRN_HEADS_PER_STEP * hd
    n_groups = n_heads // HGRN_HEADS_PER_STEP
    col = lambda off: pl.BlockSpec((1, s, wide), lambda b, h: (b, 0, off + h))
    const = lambda a: pl.BlockSpec(a.shape, lambda b, h: (0,) * a.ndim)
    return pl.pallas_call(
        functools.partial(_hgrn_kernel, n_chunks=s // c),
        grid=(bsz, n_groups),
        in_specs=[
            col(0), col(n_groups), col(2 * n_groups), col(0), col(n_groups),
            pl.BlockSpec((1, wide), lambda b, h: (0, h)),
            pl.BlockSpec((1, hd), lambda b, h: (0, 0)),
            const(tri), const(masks), const(signs),
        ],
        out_specs=pl.BlockSpec((1, s, wide), lambda b, h: (b, 0, h)),
        out_shape=jax.ShapeDtypeStruct((bsz, s, n_heads * hd), BF16),
        scratch_shapes=[pltpu.VMEM((s, wide), F32)],
        compiler_params=_cparams("parallel", "parallel"),
        name="hgrn2",
    )(qvg, qvg, qvg, zz, zz, lb.reshape(1, -1), norm_w.reshape(1, hd), jnp.asarray(tri), jnp.asarray(masks),
      jnp.asarray(signs))


def _dilated_kernel(q_ref, k_ref, v_ref, o_ref, qf_ref, kf_ref, vf_ref, oc_ref, lc_ref, *, seq):
    qf_ref[...] = q_ref[0].astype(F32)
    kf_ref[...] = k_ref[0].astype(F32)
    vf_ref[...] = v_ref[0].astype(F32)
    n_cfg = len(B_CONFIGS)
    for ci, (_, dil) in enumerate(B_CONFIGS):
        length = seq // dil
        tq = length if length <= LANES + 2 * BAND_RADIUS else LANES
        win = min(length, tq + 2 * BAND_RADIUS)
        head0 = lax.broadcasted_iota(jnp.int32, (tq, LANES), 1) < B_HEAD_DIM
        rel = lax.broadcasted_iota(jnp.int32, (tq, win), 1) - lax.broadcasted_iota(jnp.int32, (tq, win), 0)

        def rows(first, size, dil=dil):
            return pl.ds(first, size) if dil == 1 else pl.ds(first, size, stride=dil)

        def block(t, carry, ci=ci, dil=dil, length=length, tq=tq, win=win, head0=head0, rel=rel, rows=rows):
            res = t % dil
            q0 = (t // dil) * tq
            start = jnp.clip(q0 - BAND_RADIUS, 0, length - win)
            valid = jnp.abs(rel + (start - q0)) <= BAND_RADIUS
            q_rows = rows(q0 * dil + res, tq)
            k_rows = rows(start * dil + res, win)
            q = qf_ref[q_rows, :].astype(BF16)
            kw = kf_ref[k_rows, :].astype(BF16)
            vw = vf_ref[k_rows, :].astype(BF16)

            def one_head(mask):
                s = _dot_nt(jnp.where(mask, q, jnp.zeros_like(q)), kw)
                s = jnp.where(valid, s, MASK_VALUE)
                m = jnp.max(s, axis=-1, keepdims=True)
                p = jnp.exp2(s - m)
                l = jnp.sum(p, axis=-1, keepdims=True)
                return _dot(p.astype(BF16), vw) / l, m + jnp.log(l) * LOG2_E

            oa, la = one_head(head0)
            ob, lb = one_head(jnp.logical_not(head0))
            oc_ref[ci, q_rows, :] = jnp.where(head0, oa, ob)
            lc_ref[ci, q_rows, :] = jnp.where(head0, la, lb)
            return carry

        n_blocks = dil * (length // tq)
        lax.fori_loop(0, n_blocks, block, 0, unroll=max(1, min(8, 8 * LANES // tq, n_blocks)))

    tmix = min(256, seq)

    def mix(i, carry):
        r0 = pl.multiple_of(i * tmix, tmix)
        lses = [lc_ref[c, pl.ds(r0, tmix), :] for c in range(n_cfg)]
        top = lses[0]
        for l in lses[1:]:
            top = jnp.maximum(top, l)
        num = jnp.zeros((tmix, LANES), F32)
        den = jnp.zeros((tmix, LANES), F32)
        for c in range(n_cfg):
            w = jnp.exp2(lses[c] - top)
            num = num + w * oc_ref[c, pl.ds(r0, tmix), :]
            den = den + w
        o_ref[0, pl.ds(r0, tmix), :] = (num / den).astype(o_ref.dtype)
        return carry

    lax.fori_loop(0, seq // tmix, mix, 0)


def _dilated_attention(qk, v):
    bsz, s, w = v.shape
    n_pairs = w // LANES
    n_cfg = len(B_CONFIGS)
    return pl.pallas_call(
        functools.partial(_dilated_kernel, seq=s),
        grid=(bsz, n_pairs),
        in_specs=[
            pl.BlockSpec((1, s, LANES), lambda b, h: (b, 0, h)),
            pl.BlockSpec((1, s, LANES), lambda b, h: (b, 0, n_pairs + h)),
            pl.BlockSpec((1, s, LANES), lambda b, h: (b, 0, h)),
        ],
        out_specs=pl.BlockSpec((1, s, LANES), lambda b, h: (b, 0, h)),
        out_shape=jax.ShapeDtypeStruct((bsz, s, w), BF16),
        scratch_shapes=[pltpu.VMEM((s, LANES), F32)] * 3 + [pltpu.VMEM((n_cfg, s, LANES), F32)] * 2,
        compiler_params=_cparams("parallel", "parallel"),
        name="dilated_attn",
    )(qk, qk, v)


def _diff_kernel(q_ref, k_ref, v_ref, lam_ref, sub_ref, o_ref, *, tk, lambda_init):
    q = q_ref[0]
    tq = q.shape[0]
    s_len = k_ref.shape[1]
    lane = lax.broadcasted_iota(jnp.int32, q.shape, 1)
    zero = jnp.zeros_like(q)
    qs = (jnp.where(lane < C_HEAD_DIM, q, zero), jnp.where(lane >= C_HEAD_DIM, q, zero))
    tiles = [(t * LANES, (t + 1) * LANES) for t in range(tk // LANES)]
    m = [jnp.full((tq, 1), -jnp.inf, F32)] * 2
    l = [jnp.zeros((tq, 1), F32)] * 2
    acc = [jnp.zeros((tq, LANES), F32)] * 2
    for c in range(s_len // tk):
        lo, hi = c * tk, (c + 1) * tk
        for h in range(2):
            s = _dot_nt(qs[h], k_ref[0, lo:hi, :])
            m_tile = s[:, 0:LANES]
            for a, b in tiles[1:]:
                m_tile = jnp.maximum(m_tile, s[:, a:b])
            m_new = jnp.maximum(m[h], jnp.max(m_tile, axis=-1, keepdims=True))
            alpha = jnp.exp2(m[h] - m_new)
            p = jnp.exp2(s - m_new)
            l_tile = p[:, 0:LANES]
            for a, b in tiles[1:]:
                l_tile = l_tile + p[:, a:b]
            l[h] = alpha * l[h] + jnp.sum(l_tile, axis=-1, keepdims=True)
            acc[h] = alpha * acc[h] + _dot(p.astype(BF16), v_ref[0, lo:hi, :])
            m[h] = m_new
    outs = (acc[0] / l[0], acc[1] / l[1])
    lp = lam_ref[...]
    lam = (jnp.exp(jnp.sum(lp[0:1] * lp[1:2], axis=-1, keepdims=True))
           - jnp.exp(jnp.sum(lp[2:3] * lp[3:4], axis=-1, keepdims=True)) + lambda_init)
    o = outs[0] - lam * outs[1]
    ms_o = jnp.mean(o * o, axis=-1, keepdims=True)
    o_ref[0] = (o * lax.rsqrt(ms_o + RMS_EPS) * sub_ref[...] * (1.0 - lambda_init)).astype(o_ref.dtype)


def _diff_attention(qk, v, lam_params, subln_w, lambda_init):
    bsz, s, w = v.shape
    n_heads = w // LANES
    tq = _tile(s, (1024, 512, 256, 128))
    tk = _tile(s, (2048, 1024, 512, 256, 128))
    return pl.pallas_call(
        functools.partial(_diff_kernel, tk=tk, lambda_init=lambda_init),
        grid=(bsz, n_heads, s // tq),
        in_specs=[
            pl.BlockSpec((1, tq, LANES), lambda b, h, i: (b, i, h)),
            pl.BlockSpec((1, s, LANES), lambda b, h, i: (b, 0, n_heads + h)),
            pl.BlockSpec((1, s, LANES), lambda b, h, i: (b, 0, h)),
            pl.BlockSpec(lam_params.shape, lambda b, h, i: (0, 0)),
            pl.BlockSpec((1, LANES), lambda b, h, i: (0, 0)),
        ],
        out_specs=pl.BlockSpec((1, tq, LANES), lambda b, h, i: (b, i, h)),
        out_shape=jax.ShapeDtypeStruct((bsz, s, w), BF16),
        compiler_params=_cparams("parallel", "parallel", "arbitrary"),
        name="diff_attn",
    )(qk, qk, v, lam_params, subln_w.reshape(1, LANES))


def _rope_tables(seq, width):
    half = B_HEAD_DIM // 2
    inv = ROPE_THETA ** (-jnp.arange(0, B_HEAD_DIM, 2, dtype=F32) / B_HEAD_DIM)
    ang = jnp.arange(seq, dtype=F32)[:, None] * inv[None, :]
    cos, sin = jnp.cos(ang), jnp.sin(ang)
    reps = width // B_HEAD_DIM
    assert half * 2 == B_HEAD_DIM
    return jnp.tile(jnp.concatenate([cos, cos], axis=1), (1, reps)), jnp.tile(jnp.concatenate([-sin, sin], axis=1), (1, reps))


def _even_layer(x_f, x_b, bsz, seq, w_in, lb, norm_w, w_out, ln1, w1, w3, w2, layer, ln2, rope, alpha):
    d = x_f.shape[1]
    aw = d // 2
    n_heads_a = aw // A_HEAD_DIM
    w_in = w_in.astype(BF16)
    cols = lambda a, b: w_in[:, a * aw:b * aw]
    qvg = _proj(x_b, jnp.concatenate([cols(0, 1), cols(3, 5)], axis=1), BF16)
    zz = _proj(x_b, cols(1, 3), F32)
    cos_t, sin_t = rope
    scale_row = jnp.concatenate([jnp.full((1, aw), LOG2_E * B_HEAD_DIM ** -0.5, F32), jnp.ones((1, aw), F32)], axis=1)
    qk = _proj_rope(x_b, cols(5, 7), cos_t, sin_t, scale_row, seq)
    vb = _proj(x_b, cols(7, 8), BF16)
    oa = _hgrn(qvg.reshape(bsz, seq, -1), zz.reshape(bsz, seq, -1), lb, norm_w, n_heads_a)
    ob = _dilated_attention(qk.reshape(bsz, seq, 2 * aw), vb.reshape(bsz, seq, aw))
    w_out = w_out.astype(BF16)
    x_f, x_b = _out_ln([oa.reshape(bsz * seq, aw), ob.reshape(bsz * seq, aw)], [w_out[:aw], w_out[aw:]], x_f,
                       ln1[0], ln1[1], alpha)
    return _ffn_ln(x_b, x_f, w1[layer].astype(BF16), w3[layer].astype(BF16), w2[layer].astype(BF16), ln2[0], ln2[1],
                   alpha)


def _moe_dispatch(idx, n_tokens, tm):
    e_flat = idx[:, :2].reshape(-1)
    onehot = (e_flat[None, :] == jnp.arange(N_EXPERTS, dtype=jnp.int32)[:, None]).astype(jnp.int32)
    counts = jnp.sum(onehot, axis=1)
    tiles = (counts + tm - 1) // tm
    tile_end = jnp.cumsum(tiles)
    group_start = (tile_end - tiles) * tm
    dest = jnp.sum((jnp.cumsum(onehot, axis=1) - 1 + group_start[:, None]) * onehot, axis=0)
    n_tiles = (2 * n_tokens) // tm + N_EXPERTS
    src_tok = jnp.zeros((n_tiles * tm,), jnp.int32).at[dest].set(jnp.arange(2 * n_tokens, dtype=jnp.int32) // 2,
                                                                 unique_indices=True, mode="promise_in_bounds")
    tile_ids = jnp.arange(n_tiles, dtype=jnp.int32)
    tile_expert = jnp.minimum(jnp.sum((tile_ids[:, None] >= tile_end[None, :]).astype(jnp.int32), axis=1),
                              N_EXPERTS - 1)
    return src_tok, dest.reshape(n_tokens, 2), tile_expert, tile_end[-1:].astype(jnp.int32)


def _odd_layer(x_f, x_b, bsz, seq, w_in, lam_params, subln_w, w_out, ln1, router, w1, w3, w2, layer, ln2, rope,
               alpha, lambda_init):
    d = x_f.shape[1]
    n_tok = bsz * seq
    w_in = w_in.astype(BF16)
    cos_t, sin_t = rope
    scale_row = jnp.concatenate([jnp.full((1, d), LOG2_E * C_HEAD_DIM ** -0.5, F32), jnp.ones((1, d), F32)], axis=1)
    qk = _proj_rope(x_b, w_in[:, :2 * d], cos_t, sin_t, scale_row, seq)
    v = _proj(x_b, w_in[:, 2 * d:], BF16)
    o = _diff_attention(qk.reshape(bsz, seq, 2 * d), v.reshape(bsz, seq, d), lam_params.astype(F32), subln_w,
                        lambda_init)
    x_f, x_b = _out_ln([o.reshape(n_tok, d)], [w_out.astype(BF16)], x_f, ln1[0], ln1[1], alpha)
    router_padded = jnp.pad(router.astype(F32), ((0, 0), (0, LANES - N_EXPERTS)))
    gates, idx = _router(x_f, router_padded)
    tm = _tile(n_tok, (1024, 512, 256))
    src_tok, pos, tile_expert, n_active = _moe_dispatch(idx, n_tok, tm)
    n_tiles = src_tok.shape[0] // tm
    w1, w3, w2 = (_layer_weights_bf16(w, layer) for w in (w1, w3, w2))
    y = None
    for lo, hi in ((0, n_tiles // 2), (n_tiles // 2, n_tiles)):
        x_rows = jnp.take(x_b, src_tok[lo * tm:hi * tm], axis=0, mode="clip")
        n_act = jnp.clip(n_active - lo, 0, hi - lo)
        y = _moe_ffn(x_rows, tile_expert[lo:hi], n_act, w1, w3, w2, tm, lo, n_tiles * tm, y)
    y1 = jnp.take(y, pos[:, 0], axis=0, mode="clip")
    y2 = jnp.take(y, pos[:, 1], axis=0, mode="clip")
    return _moe_combine(x_f, y1, y2, gates, ln2[0], ln2[1], alpha)


def kernel(x, ev_w_in, ev_lb_logits, ev_hgrn_norm, ev_w_out, ev_ln1_g, ev_ln1_b, ev_w1, ev_w3, ev_w2, ev_ln2_g,
           ev_ln2_b, od_w_in, od_lambda, od_subln, od_w_out, od_ln1_g, od_ln1_b, od_router, od_w1, od_w3, od_w2,
           od_ln2_g, od_ln2_b):
    bsz, seq, d = x.shape
    depth = ev_w_in.shape[0] + od_w_in.shape[0]
    alpha = (2 * depth) ** 0.25
    rope = _rope_tables(seq, LANES)
    lb_soft = jax.nn.softmax(ev_lb_logits.astype(F32), axis=0)
    lower_bounds = jnp.cumsum(lb_soft, axis=0) - lb_soft[0]
    x_f = x.reshape(bsz * seq, d).astype(F32)
    x_b = x_f.astype(BF16)
    for layer in range(depth):
        j = layer // 2
        if layer % 2 == 0:
            x_f, x_b = _even_layer(x_f, x_b, bsz, seq, ev_w_in[j], lower_bounds[j], ev_hgrn_norm[j], ev_w_out[j],
                                   (ev_ln1_g[j], ev_ln1_b[j]), ev_w1, ev_w3, ev_w2, j,
                                   (ev_ln2_g[j], ev_ln2_b[j]), rope, alpha)
        else:
            lambda_init = 0.8 - 0.6 * math.exp(-0.3 * layer)
            x_f, x_b = _odd_layer(x_f, x_b, bsz, seq, od_w_in[j], od_lambda[j], od_subln[j], od_w_out[j],
                                  (od_ln1_g[j], od_ln1_b[j]), od_router[j], od_w1, od_w3, od_w2, j,
                                  (od_ln2_g[j], od_ln2_b[j]), rope, alpha, lambda_init)
    return x_f.reshape(bsz, seq, d).astype(x.dtype)
```

```python
import functools
import math

import numpy as np
import jax
import jax.numpy as jnp
from jax import lax
from jax.experimental import pallas as pl
from jax.experimental.pallas import tpu as pltpu

F32 = jnp.float32
BF16 = jnp.bfloat16

A_HEAD_DIM = 128
B_HEAD_DIM = 64
B_CONFIGS = ((128, 1), (512, 4), (2048, 16))
BAND_RADIUS = 64
C_HEAD_DIM = 64
N_EXPERTS = 8
ROPE_THETA = 10000.0
LN_EPS = 1e-5
RMS_EPS = 1e-5
MASK_VALUE = -1e30
MIN_FORGET = 1e-30
LOG2_E = math.log2(math.e)

LANES = 128
SUBLANES = 8
VMEM_LIMIT_BYTES = 56 * 1024 * 1024

ROW_CHAINS = 4
FFN_ROW_CHAINS = 2

HGRN_CHUNK = 128
HGRN_LEVELS = 7
HGRN_HEADS_PER_STEP = 2


def _cparams(*sem):
    return pltpu.CompilerParams(dimension_semantics=sem, vmem_limit_bytes=VMEM_LIMIT_BYTES)


def _tile(n, prefs):
    for p in prefs:
        if n % p == 0:
            return p
    return n


def _dot(a, b):
    return jnp.dot(a, b, preferred_element_type=F32)


def _dot_nt(a, b):
    return lax.dot_general(a, b, (((1,), (1,)), ((), ())), preferred_element_type=F32)


def _dot_tn(a, b):
    return lax.dot_general(a, b, (((0,), (0,)), ((), ())), preferred_element_type=F32)


def _sigmoid(x):
    return 1.0 / (1.0 + jnp.exp(-x))


def _layer_norm(y, g, b):
    mu = jnp.mean(y, axis=-1, keepdims=True)
    d = y - mu
    var = jnp.mean(d * d, axis=-1, keepdims=True)
    return d * lax.rsqrt(var + LN_EPS) * g + b


def _proj_kernel(x_ref, w_ref, o_ref):
    o_ref[...] = _dot(x_ref[...], w_ref[...]).astype(o_ref.dtype)


def _proj_multi_kernel(x_ref, w_ref, *o_refs):
    x = x_ref[...]
    col = 0
    for o_ref in o_refs:
        n = o_ref.shape[1]
        o_ref[...] = _dot(x, w_ref[:, col:col + n]).astype(o_ref.dtype)
        col += n


def _proj_multi(x, w, widths, dtypes):
    m, k = x.shape
    tm = _tile(m, (1024, 512, 256))
    return pl.pallas_call(
        _proj_multi_kernel,
        grid=(m // tm,),
        in_specs=[pl.BlockSpec((tm, k), lambda i: (i, 0)), pl.BlockSpec(w.shape, lambda i: (0, 0))],
        out_specs=[pl.BlockSpec((tm, n), lambda i: (i, 0)) for n in widths],
        out_shape=[jax.ShapeDtypeStruct((m, n), dt) for n, dt in zip(widths, dtypes)],
        compiler_params=_cparams("parallel"),
        name="proj_multi",
    )(x, w)


def _proj(x, w, out_dtype):
    m, k = x.shape
    n = w.shape[1]
    tm = _tile(m, (1024, 512, 256))
    tn = n if n <= 1536 else _tile(n, (1024, 512, 256, 128))
    return pl.pallas_call(
        _proj_kernel,
        grid=(m // tm, n // tn),
        in_specs=[pl.BlockSpec((tm, k), lambda i, j: (i, 0)), pl.BlockSpec((k, tn), lambda i, j: (0, j))],
        out_specs=pl.BlockSpec((tm, tn), lambda i, j: (i, j)),
        out_shape=jax.ShapeDtypeStruct((m, n), out_dtype),
        compiler_params=_cparams("parallel", "arbitrary"),
        name="proj",
    )(x, w)


def _proj_rope_kernel(x_ref, w_ref, cos_ref, sin_ref, scale_ref, o_ref):
    tm, tn = o_ref.shape
    sub = tm // ROW_CHAINS
    w = w_ref[...]
    lane = lax.broadcasted_iota(jnp.int32, (sub, tn), 1)
    first_half = (lane % B_HEAD_DIM) < (B_HEAD_DIM // 2)
    reps = tn // cos_ref.shape[1]
    for k in range(ROW_CHAINS):
        rows = pl.ds(k * sub, sub)
        acc = _dot(x_ref[rows, :], w)
        partner = jnp.where(first_half, pltpu.roll(acc, tn - B_HEAD_DIM // 2, 1), pltpu.roll(acc, B_HEAD_DIM // 2, 1))
        cos = jnp.tile(cos_ref[rows, :], (1, reps))
        sin = jnp.tile(sin_ref[rows, :], (1, reps))
        o_ref[rows, :] = ((acc * cos + partner * sin) * scale_ref[...]).astype(o_ref.dtype)


def _proj_rope(x, w, cos_t, sin_t, scale_row, seq):
    m, k = x.shape
    n = w.shape[1]
    tm = _tile(seq, (1024, 512, 256))
    tn = _tile(n, (1024, 512, 256, 128))
    nsb = seq // tm
    return pl.pallas_call(
        _proj_rope_kernel,
        grid=(m // tm, n // tn),
        in_specs=[
            pl.BlockSpec((tm, k), lambda i, j: (i, 0)),
            pl.BlockSpec((k, tn), lambda i, j: (0, j)),
            pl.BlockSpec((tm, LANES), lambda i, j: (i % nsb, 0)),
            pl.BlockSpec((tm, LANES), lambda i, j: (i % nsb, 0)),
            pl.BlockSpec((1, tn), lambda i, j: (0, j)),
        ],
        out_specs=pl.BlockSpec((tm, tn), lambda i, j: (i, j)),
        out_shape=jax.ShapeDtypeStruct((m, n), BF16),
        compiler_params=_cparams("parallel", "arbitrary"),
        name="proj_rope",
    )(x, w, cos_t, sin_t, scale_row)


def _top2_gates(logits):
    lane = lax.broadcasted_iota(jnp.int32, logits.shape, 1)
    neg = jnp.float32(-jnp.inf)
    logits = jnp.where(lane < N_EXPERTS, logits, neg)
    v1 = jnp.max(logits, axis=-1, keepdims=True)
    i1 = jnp.min(jnp.where(logits == v1, lane, LANES), axis=-1, keepdims=True)
    rest = jnp.where(lane == i1, neg, logits)
    v2 = jnp.max(rest, axis=-1, keepdims=True)
    i2 = jnp.min(jnp.where(rest == v2, lane, LANES), axis=-1, keepdims=True)
    e = jnp.exp(v2 - v1)
    g1 = 1.0 / (1.0 + e)
    g2 = e / (1.0 + e)
    return (jnp.where(lane == 0, g1, jnp.where(lane == 1, g2, 0.0)),
            jnp.where(lane == 0, i1, jnp.where(lane == 1, i2, 0)))


def _out_ln_kernel(*refs, n_in, alpha, with_router):
    xs = refs[:n_in]
    ws = refs[n_in:2 * n_in]
    if with_router:
        resid_ref, g_ref, b_ref, r_ref, of_ref, ob_ref, gate_ref, idx_ref = refs[2 * n_in:]
    else:
        resid_ref, g_ref, b_ref, of_ref, ob_ref = refs[2 * n_in:]
    sub = of_ref.shape[0] // ROW_CHAINS
    for k in range(ROW_CHAINS):
        rows = pl.ds(k * sub, sub)
        acc = _dot(xs[0][rows, :], ws[0][...])
        for x_ref, w_ref in zip(xs[1:], ws[1:]):
            acc = acc + _dot(x_ref[rows, :], w_ref[...])
        z = _layer_norm(alpha * resid_ref[rows, :] + acc, g_ref[...], b_ref[...])
        of_ref[rows, :] = z
        ob_ref[rows, :] = z.astype(BF16)
        if with_router:
            logits = jnp.dot(z, r_ref[...], precision=lax.Precision.HIGHEST, preferred_element_type=F32)
            gate_ref[rows, :], idx_ref[rows, :] = _top2_gates(logits)


def _out_ln(xs, ws, resid, g, b, alpha, router_padded=None):
    m, d = resid.shape
    tm = _tile(m, (1024, 512, 256))
    n_in = len(xs)
    row = lambda i: (i, 0)
    fixed = lambda i: (0, 0)
    in_specs = [pl.BlockSpec((tm, x.shape[1]), row) for x in xs]
    in_specs += [pl.BlockSpec(w.shape, fixed) for w in ws]
    in_specs += [pl.BlockSpec((tm, d), row), pl.BlockSpec((1, d), fixed), pl.BlockSpec((1, d), fixed)]
    operands = [*xs, *ws, resid, g.reshape(1, d), b.reshape(1, d)]
    out_specs = [pl.BlockSpec((tm, d), row), pl.BlockSpec((tm, d), row)]
    out_shape = [jax.ShapeDtypeStruct((m, d), F32), jax.ShapeDtypeStruct((m, d), BF16)]
    if router_padded is not None:
        in_specs.append(pl.BlockSpec((d, LANES), fixed))
        operands.append(router_padded)
        out_specs += [pl.BlockSpec((tm, LANES), row), pl.BlockSpec((tm, LANES), row)]
        out_shape += [jax.ShapeDtypeStruct((m, LANES), F32), jax.ShapeDtypeStruct((m, LANES), jnp.int32)]
    return pl.pallas_call(
        functools.partial(_out_ln_kernel, n_in=n_in, alpha=alpha, with_router=router_padded is not None),
        grid=(m // tm,),
        in_specs=in_specs,
        out_specs=out_specs,
        out_shape=out_shape,
        compiler_params=_cparams("parallel"),
        name="out_ln",
    )(*operands)


def _swiglu_accumulate(x_ref, w1, w3, w2, acc_ref):
    sub = x_ref.shape[0] // FFN_ROW_CHAINS
    for k in range(FFN_ROW_CHAINS):
        rows = pl.ds(k * sub, sub)
        h1 = _dot(x_ref[rows, :], w1)
        h3 = _dot(x_ref[rows, :], w3)
        h = (h1 * _sigmoid(h1)) * h3
        acc_ref[rows, :] += _dot(h.astype(BF16), w2)


def _ffn_ln_kernel(x_ref, w1_ref, w3_ref, w2_ref, resid_ref, g_ref, b_ref, of_ref, ob_ref, acc_ref, *, alpha):
    j = pl.program_id(1)

    @pl.when(j == 0)
    def _():
        acc_ref[...] = jnp.zeros_like(acc_ref)

    _swiglu_accumulate(x_ref, w1_ref[...], w3_ref[...], w2_ref[...], acc_ref)

    @pl.when(j == pl.num_programs(1) - 1)
    def _():
        z = _layer_norm(alpha * resid_ref[...] + acc_ref[...], g_ref[...], b_ref[...])
        of_ref[...] = z
        ob_ref[...] = z.astype(BF16)


def _ffn_ln(x_bf, resid, w1, w3, w2, g, b, alpha):
    m, d = resid.shape
    ff = w1.shape[1]
    tm = _tile(m, (1024, 512, 256))
    tf = _tile(ff, (1408, 256, 128))
    return pl.pallas_call(
        functools.partial(_ffn_ln_kernel, alpha=alpha),
        grid=(m // tm, ff // tf),
        in_specs=[
            pl.BlockSpec((tm, d), lambda i, j: (i, 0)),
            pl.BlockSpec((d, tf), lambda i, j: (0, j)),
            pl.BlockSpec((d, tf), lambda i, j: (0, j)),
            pl.BlockSpec((tf, d), lambda i, j: (j, 0)),
            pl.BlockSpec((tm, d), lambda i, j: (i, 0)),
            pl.BlockSpec((1, d), lambda i, j: (0, 0)),
            pl.BlockSpec((1, d), lambda i, j: (0, 0)),
        ],
        out_specs=[pl.BlockSpec((tm, d), lambda i, j: (i, 0)), pl.BlockSpec((tm, d), lambda i, j: (i, 0))],
        out_shape=[jax.ShapeDtypeStruct((m, d), F32), jax.ShapeDtypeStruct((m, d), BF16)],
        scratch_shapes=[pltpu.VMEM((tm, d), F32)],
        compiler_params=_cparams("parallel", "arbitrary"),
        name="ffn_ln",
    )(x_bf, w1, w3, w2, resid, g.reshape(1, d), b.reshape(1, d))


def _cast_kernel(w_ref, o_ref):
    o_ref[...] = w_ref[...].astype(o_ref.dtype)


def _layer_weights_bf16(w, layer):
    cols = w.shape[-1]
    rows = math.prod(w.shape[1:-1])
    flat = w.reshape(w.shape[0] * rows, cols)
    tr = _tile(rows, (512, 256, 128))
    steps = rows // tr
    out = pl.pallas_call(
        _cast_kernel,
        grid=(steps,),
        in_specs=[pl.BlockSpec((tr, cols), lambda i: (layer * steps + i, 0))],
        out_specs=pl.BlockSpec((tr, cols), lambda i: (i, 0)),
        out_shape=jax.ShapeDtypeStruct((rows, cols), BF16),
        compiler_params=_cparams("parallel"),
        name="cast_weights",
    )(flat)
    return out.reshape(w.shape[1:])


def _moe_ffn_kernel(te_ref, na_ref, x_ref, w1_ref, w3_ref, w2_ref, *rest):
    o_ref, acc_ref = rest[-2:]
    i = pl.program_id(0)
    j = pl.program_id(1)
    active = i < na_ref[0]

    @pl.when(active & (j == 0))
    def _():
        acc_ref[...] = jnp.zeros_like(acc_ref)

    @pl.when(active)
    def _():
        _swiglu_accumulate(x_ref, w1_ref[0], w3_ref[0], w2_ref[0], acc_ref)

    @pl.when(active & (j == pl.num_programs(1) - 1))
    def _():
        o_ref[...] = acc_ref[...].astype(o_ref.dtype)


def _moe_ffn(x_sorted, tile_expert, n_active, w1, w3, w2, tm, tile_offset, total_rows, y_prev=None):
    p, d = x_sorted.shape
    ff = w1.shape[2]
    tf = _tile(ff, (1408, 256, 128))
    nf = ff // tf

    def row(i, na):
        return jnp.maximum(jnp.minimum(i, na[0] - 1), 0)

    def col(i, j, na):
        return jnp.where(i < na[0], j, nf - 1)

    operands = [tile_expert, n_active, x_sorted, w1, w3, w2]
    in_specs = [
        pl.BlockSpec((tm, d), lambda i, j, te, na: (row(i, na), 0)),
        pl.BlockSpec((1, d, tf), lambda i, j, te, na: (te[row(i, na)], 0, col(i, j, na))),
        pl.BlockSpec((1, d, tf), lambda i, j, te, na: (te[row(i, na)], 0, col(i, j, na))),
        pl.BlockSpec((1, tf, d), lambda i, j, te, na: (te[row(i, na)], col(i, j, na), 0)),
    ]
    aliases = {}
    if y_prev is not None:
        in_specs.append(pl.BlockSpec(memory_space=pl.ANY))
        aliases = {len(operands): 0}
        operands.append(y_prev)
    grid_spec = pltpu.PrefetchScalarGridSpec(
        num_scalar_prefetch=2,
        grid=(p // tm, nf),
        in_specs=in_specs,
        out_specs=pl.BlockSpec((tm, d), lambda i, j, te, na: (tile_offset + row(i, na), 0)),
        scratch_shapes=[pltpu.VMEM((tm, d), F32)],
    )
    return pl.pallas_call(
        _moe_ffn_kernel,
        grid_spec=grid_spec,
        out_shape=jax.ShapeDtypeStruct((total_rows, d), BF16),
        input_output_aliases=aliases,
        compiler_params=_cparams("arbitrary", "arbitrary"),
        name="moe_ffn",
    )(*operands)


def _moe_combine_kernel(resid_ref, y1_ref, y2_ref, gate_ref, g_ref, b_ref, of_ref, ob_ref, *, alpha):
    gates = gate_ref[...]
    y = gates[:, 0:1] * y1_ref[...].astype(F32) + gates[:, 1:2] * y2_ref[...].astype(F32)
    z = _layer_norm(alpha * resid_ref[...] + y, g_ref[...], b_ref[...])
    of_ref[...] = z
    ob_ref[...] = z.astype(BF16)


def _moe_combine(resid, y1, y2, gates, g, b, alpha):
    m, d = resid.shape
    tm = _tile(m, (512, 256))
    row = lambda i: (i, 0)
    fixed = lambda i: (0, 0)
    return pl.pallas_call(
        functools.partial(_moe_combine_kernel, alpha=alpha),
        grid=(m // tm,),
        in_specs=[pl.BlockSpec((tm, d), row), pl.BlockSpec((tm, d), row), pl.BlockSpec((tm, d), row),
                  pl.BlockSpec((tm, LANES), row), pl.BlockSpec((1, d), fixed), pl.BlockSpec((1, d), fixed)],
        out_specs=[pl.BlockSpec((tm, d), row), pl.BlockSpec((tm, d), row)],
        out_shape=[jax.ShapeDtypeStruct((m, d), F32), jax.ShapeDtypeStruct((m, d), BF16)],
        compiler_params=_cparams("parallel"),
        name="moe_combine",
    )(resid, y1, y2, gates, g.reshape(1, d), b.reshape(1, d))


def _hgrn_masks(c, levels):
    t = np.arange(c)[:, None]
    s = np.arange(c)[None, :]
    tri = np.stack([(s <= t), (s >= t)]).astype(np.float32)
    fwd, bwd = [], []
    for l in range(levels):
        same = (t >> (l + 1)) == (s >> (l + 1))
        t_up = ((t >> l) & 1) == 1
        s_up = ((s >> l) & 1) == 1
        fwd.append(same & t_up & ~s_up)
        bwd.append(same & ~t_up & s_up)
    fwd.append(t == s)
    bwd.append(t == s)
    up = np.stack([np.broadcast_to(((t >> l) & 1) == 1, (c, A_HEAD_DIM)) for l in range(levels)])
    sign = np.stack([np.where(up, 1.0, -1.0), np.where(up, -1.0, 1.0)]).astype(np.float32)
    return tri, np.stack([np.stack(fwd), np.stack(bwd)]).astype(np.float32), sign


def _segment_reference(x, level, forward):
    c, w = x.shape
    half = 1 << level
    seg = 2 * half
    idx = half - 1 if forward else half
    if seg >= SUBLANES:
        xr = x.reshape(c // seg, seg, w)
        return jnp.broadcast_to(xr[:, idx:idx + 1, :], xr.shape).reshape(c, w)
    x3 = x.reshape(c // SUBLANES, SUBLANES, w)
    sub = lax.broadcasted_iota(jnp.int32, x3.shape, 1)
    r3 = jnp.broadcast_to(x3[:, idx:idx + 1, :], x3.shape)
    for j in range(1, SUBLANES // seg):
        row = j * seg + idx
        r3 = jnp.where(sub >= j * seg, jnp.broadcast_to(x3[:, row:row + 1, :], x3.shape), r3)
    return r3.reshape(c, w)


def _hgrn_chunk(q, k, v, log_f, state_t, tri, masks, signs, forward):
    c = q.shape[0]
    x = jnp.dot(tri, log_f, precision=lax.Precision.HIGHEST, preferred_element_type=F32)
    scores = masks[HGRN_LEVELS] * _dot_nt(q.astype(BF16), k.astype(BF16))
    for level in range(HGRN_LEVELS):
        ref = _segment_reference(x, level, forward)
        sign = signs[level]
        decay = jnp.exp(sign * (x - ref))
        z = (jnp.where(sign > 0, q, k) * decay).astype(BF16)
        scores = scores + masks[level] * _dot_nt(z, z)
    x_end = x[c - 1:c, :] if forward else x[0:1, :]
    q_dec = (q * jnp.exp(x)).astype(BF16)
    o = _dot(scores.astype(BF16), v.astype(BF16)) + _dot_nt(q_dec, state_t.astype(BF16))
    k_dec = (k * jnp.exp(x_end - x)).astype(BF16)
    new_state_t = state_t * jnp.exp(x_end) + _dot_tn(v.astype(BF16), k_dec)
    return o, new_state_t


def _hgrn_kernel(q_ref, v_ref, g_ref, zf_ref, zb_ref, lb_ref, nw_ref, tri_ref, msk_ref, sgn_ref, o_ref, acc_ref, *,
                 n_chunks):
    c = HGRN_CHUNK
    hd = A_HEAD_DIM

    def gates(z, lb):
        e = jnp.exp(-jnp.abs(z))
        inv = 1.0 / (1.0 + e)
        pos = z >= 0
        sig = jnp.where(pos, inv, e * inv)
        sig_neg = jnp.where(pos, e * inv, inv)
        f = lb + (1.0 - lb) * sig
        return jnp.log(jnp.maximum(f, MIN_FORGET)), (1.0 - lb) * sig_neg

    def load(ref, c0, h):
        return ref[0, pl.ds(c0, c), h * hd:(h + 1) * hd].astype(F32)

    def emit(c0, h, tot):
        ms = jnp.mean(tot * tot, axis=-1, keepdims=True)
        g = load(g_ref, c0, h)
        out = tot * lax.rsqrt(ms + RMS_EPS) * nw_ref[...] * (g * _sigmoid(g))
        o_ref[0, pl.ds(c0, c), h * hd:(h + 1) * hd] = out.astype(o_ref.dtype)

    def one_direction(h, c0, z_ref, state_t, direction):
        lb = lb_ref[:, h * hd:(h + 1) * hd]
        log_f, k = gates(load(z_ref, c0, h), lb)
        masks = [msk_ref[direction, l] for l in range(HGRN_LEVELS + 1)]
        signs = [sgn_ref[direction, l] for l in range(HGRN_LEVELS)]
        return _hgrn_chunk(load(q_ref, c0, h), k, load(v_ref, c0, h), log_f, state_t, tri_ref[direction], masks,
                           signs, direction == 0)

    def step(i, states, finalize):
        cf = pl.multiple_of(i * c, c)
        cb = pl.multiple_of((n_chunks - 1 - i) * c, c)
        new_states = []
        for h in range(HGRN_HEADS_PER_STEP):
            o_f, st_f = one_direction(h, cf, zf_ref, states[2 * h], 0)
            o_b, st_b = one_direction(h, cb, zb_ref, states[2 * h + 1], 1)
            cols = slice(h * hd, (h + 1) * hd)
            if finalize:
                emit(cf, h, acc_ref[pl.ds(cf, c), cols] + o_f)
                emit(cb, h, acc_ref[pl.ds(cb, c), cols] + o_b)
            else:
                acc_ref[pl.ds(cf, c), cols] = o_f
                acc_ref[pl.ds(cb, c), cols] = o_b
            new_states += [st_f, st_b]
        return tuple(new_states)

    zero = jnp.zeros((hd, hd), F32)
    half = n_chunks // 2
    states = lax.fori_loop(0, half, functools.partial(step, finalize=False), (zero,) * (2 * HGRN_HEADS_PER_STEP))
    lax.fori_loop(half, n_chunks, functools.partial(step, finalize=True), states)


def _hgrn(qvg, zz, lb, norm_w, n_heads):
    bsz, s, _ = qvg.shape
    c = HGRN_CHUNK
    assert s % (2 * c) == 0 and n_heads % HGRN_HEADS_PER_STEP == 0
    tri, masks, signs = _hgrn_masks(c, HGRN_LEVELS)
    hd = A_HEAD_DIM
    wide = HGRN_HEADS_PER_STEP * hd
    n_groups = n_heads // HGRN_HEADS_PER_STEP
    col = lambda off: pl.BlockSpec((1, s, wide), lambda b, h: (b, 0, off + h))
    const = lambda a: pl.BlockSpec(a.shape, lambda b, h: (0,) * a.ndim)
    return pl.pallas_call(
        functools.partial(_hgrn_kernel, n_chunks=s // c),
        grid=(bsz, n_groups),
        in_specs=[
            col(0), col(n_groups), col(2 * n_groups), col(0), col(n_groups),
            pl.BlockSpec((1, wide), lambda b, h: (0, h)),
            pl.BlockSpec((1, hd), lambda b, h: (0, 0)),
            const(tri), const(masks), const(signs),
        ],
        out_specs=pl.BlockSpec((1, s, wide), lambda b, h: (b, 0, h)),
        out_shape=jax.ShapeDtypeStruct((bsz, s, n_heads * hd), BF16),
        scratch_shapes=[pltpu.VMEM((s, wide), F32)],
        compiler_params=_cparams("parallel", "parallel"),
        name="hgrn2",
    )(qvg, qvg, qvg, zz, zz, lb.reshape(1, -1), norm_w.reshape(1, hd), jnp.asarray(tri), jnp.asarray(masks),
      jnp.asarray(signs))


def _dilated_kernel(q_ref, k_ref, v_ref, o_ref, qf_ref, kf_ref, vf_ref, oc_ref, lc_ref, *, seq):
    qf_ref[...] = q_ref[0].astype(F32)
    kf_ref[...] = k_ref[0].astype(F32)
    vf_ref[...] = v_ref[0].astype(F32)
    n_cfg = len(B_CONFIGS)
    for ci, (_, dil) in enumerate(B_CONFIGS):
        length = seq // dil
        tq = length if length <= LANES + 2 * BAND_RADIUS else LANES
        win = min(length, tq + 2 * BAND_RADIUS)
        head0 = lax.broadcasted_iota(jnp.int32, (tq, LANES), 1) < B_HEAD_DIM
        rel = lax.broadcasted_iota(jnp.int32, (tq, win), 1) - lax.broadcasted_iota(jnp.int32, (tq, win), 0)

        def rows(first, size, dil=dil):
            return pl.ds(first, size) if dil == 1 else pl.ds(first, size, stride=dil)

        def block(t, carry, ci=ci, dil=dil, length=length, tq=tq, win=win, head0=head0, rel=rel, rows=rows):
            res = t % dil
            q0 = (t // dil) * tq
            start = jnp.clip(q0 - BAND_RADIUS, 0, length - win)
            valid = jnp.abs(rel + (start - q0)) <= BAND_RADIUS
            q_rows = rows(q0 * dil + res, tq)
            k_rows = rows(start * dil + res, win)
            q = qf_ref[q_rows, :].astype(BF16)
            kw = kf_ref[k_rows, :].astype(BF16)
            vw = vf_ref[k_rows, :].astype(BF16)

            def one_head(mask):
                s = _dot_nt(jnp.where(mask, q, jnp.zeros_like(q)), kw)
                s = jnp.where(valid, s, MASK_VALUE)
                m = jnp.max(s, axis=-1, keepdims=True)
                p = jnp.exp2(s - m)
                l = jnp.sum(p, axis=-1, keepdims=True)
                return _dot(p.astype(BF16), vw) / l, m + jnp.log(l) * LOG2_E

            oa, la = one_head(head0)
            ob, lb = one_head(jnp.logical_not(head0))
            oc_ref[ci, q_rows, :] = jnp.where(head0, oa, ob)
            lc_ref[ci, q_rows, :] = jnp.where(head0, la, lb)
            return carry

        n_blocks = dil * (length // tq)
        lax.fori_loop(0, n_blocks, block, 0, unroll=max(1, min(8, 8 * LANES // tq, n_blocks)))

    tmix = min(256, seq)

    def mix(i, carry):
        r0 = pl.multiple_of(i * tmix, tmix)
        lses = [lc_ref[c, pl.ds(r0, tmix), :] for c in range(n_cfg)]
        top = lses[0]
        for l in lses[1:]:
            top = jnp.maximum(top, l)
        num = jnp.zeros((tmix, LANES), F32)
        den = jnp.zeros((tmix, LANES), F32)
        for c in range(n_cfg):
            w = jnp.exp2(lses[c] - top)
            num = num + w * oc_ref[c, pl.ds(r0, tmix), :]
            den = den + w
        o_ref[0, pl.ds(r0, tmix), :] = (num / den).astype(o_ref.dtype)
        return carry

    lax.fori_loop(0, seq // tmix, mix, 0)


def _dilated_attention(qk, v):
    bsz, s, w = v.shape
    n_pairs = w // LANES
    n_cfg = len(B_CONFIGS)
    return pl.pallas_call(
        functools.partial(_dilated_kernel, seq=s),
        grid=(bsz, n_pairs),
        in_specs=[
            pl.BlockSpec((1, s, LANES), lambda b, h: (b, 0, h)),
            pl.BlockSpec((1, s, LANES), lambda b, h: (b, 0, n_pairs + h)),
            pl.BlockSpec((1, s, LANES), lambda b, h: (b, 0, h)),
        ],
        out_specs=pl.BlockSpec((1, s, LANES), lambda b, h: (b, 0, h)),
        out_shape=jax.ShapeDtypeStruct((bsz, s, w), BF16),
        scratch_shapes=[pltpu.VMEM((s, LANES), F32)] * 3 + [pltpu.VMEM((n_cfg, s, LANES), F32)] * 2,
        compiler_params=_cparams("parallel", "parallel"),
        name="dilated_attn",
    )(qk, qk, v)


def _diff_kernel(q_ref, k_ref, v_ref, lam_ref, sub_ref, o_ref, *, tk, lambda_init):
    q = q_ref[0]
    tq = q.shape[0]
    s_len = k_ref.shape[1]
    lane = lax.broadcasted_iota(jnp.int32, q.shape, 1)
    zero = jnp.zeros_like(q)
    qs = (jnp.where(lane < C_HEAD_DIM, q, zero), jnp.where(lane >= C_HEAD_DIM, q, zero))
    tiles = [(t * LANES, (t + 1) * LANES) for t in range(tk // LANES)]
    m = [jnp.full((tq, 1), -jnp.inf, F32)] * 2
    l = [jnp.zeros((tq, 1), F32)] * 2
    acc = [jnp.zeros((tq, LANES), F32)] * 2
    for c in range(s_len // tk):
        lo, hi = c * tk, (c + 1) * tk
        for h in range(2):
            s = _dot_nt(qs[h], k_ref[0, lo:hi, :])
            m_tile = s[:, 0:LANES]
            for a, b in tiles[1:]:
                m_tile = jnp.maximum(m_tile, s[:, a:b])
            m_new = jnp.maximum(m[h], jnp.max(m_tile, axis=-1, keepdims=True))
            alpha = jnp.exp2(m[h] - m_new)
            p = jnp.exp2(s - m_new)
            l_tile = p[:, 0:LANES]
            for a, b in tiles[1:]:
                l_tile = l_tile + p[:, a:b]
            l[h] = alpha * l[h] + jnp.sum(l_tile, axis=-1, keepdims=True)
            acc[h] = alpha * acc[h] + _dot(p.astype(BF16), v_ref[0, lo:hi, :])
            m[h] = m_new
    outs = (acc[0] / l[0], acc[1] / l[1])
    lp = lam_ref[...]
    lam = (jnp.exp(jnp.sum(lp[0:1] * lp[1:2], axis=-1, keepdims=True))
           - jnp.exp(jnp.sum(lp[2:3] * lp[3:4], axis=-1, keepdims=True)) + lambda_init)
    o = outs[0] - lam * outs[1]
    ms_o = jnp.mean(o * o, axis=-1, keepdims=True)
    o_ref[0] = (o * lax.rsqrt(ms_o + RMS_EPS) * sub_ref[...] * (1.0 - lambda_init)).astype(o_ref.dtype)


def _diff_attention(qk, v, lam_params, subln_w, lambda_init):
    bsz, s, w = v.shape
    n_heads = w // LANES
    tq = _tile(s, (1024, 512, 256, 128))
    tk = _tile(s, (2048, 1024, 512, 256, 128))
    return pl.pallas_call(
        functools.partial(_diff_kernel, tk=tk, lambda_init=lambda_init),
        grid=(bsz, n_heads, s // tq),
        in_specs=[
            pl.BlockSpec((1, tq, LANES), lambda b, h, i: (b, i, h)),
            pl.BlockSpec((1, s, LANES), lambda b, h, i: (b, 0, n_heads + h)),
            pl.BlockSpec((1, s, LANES), lambda b, h, i: (b, 0, h)),
            pl.BlockSpec(lam_params.shape, lambda b, h, i: (0, 0)),
            pl.BlockSpec((1, LANES), lambda b, h, i: (0, 0)),
        ],
        out_specs=pl.BlockSpec((1, tq, LANES), lambda b, h, i: (b, i, h)),
        out_shape=jax.ShapeDtypeStruct((bsz, s, w), BF16),
        compiler_params=_cparams("parallel", "parallel", "arbitrary"),
        name="diff_attn",
    )(qk, qk, v, lam_params, subln_w.reshape(1, LANES))


def _rope_tables(seq, width):
    half = B_HEAD_DIM // 2
    inv = ROPE_THETA ** (-jnp.arange(0, B_HEAD_DIM, 2, dtype=F32) / B_HEAD_DIM)
    ang = jnp.arange(seq, dtype=F32)[:, None] * inv[None, :]
    cos, sin = jnp.cos(ang), jnp.sin(ang)
    reps = width // B_HEAD_DIM
    assert half * 2 == B_HEAD_DIM
    return jnp.tile(jnp.concatenate([cos, cos], axis=1), (1, reps)), jnp.tile(jnp.concatenate([-sin, sin], axis=1), (1, reps))


def _even_layer(x_f, x_b, bsz, seq, w_in, lb, norm_w, w_out, ln1, w1, w3, w2, layer, ln2, rope, alpha):
    d = x_f.shape[1]
    aw = d // 2
    n_heads_a = aw // A_HEAD_DIM
    w_in = w_in.astype(BF16)
    cols = lambda a, b: w_in[:, a * aw:b * aw]
    qvg, zz, vb = _proj_multi(x_b, jnp.concatenate([cols(0, 1), cols(3, 5), cols(1, 3), cols(7, 8)], axis=1),
                              (3 * aw, 2 * aw, aw), (BF16, F32, BF16))
    cos_t, sin_t = rope
    scale_row = jnp.concatenate([jnp.full((1, aw), LOG2_E * B_HEAD_DIM ** -0.5, F32), jnp.ones((1, aw), F32)], axis=1)
    qk = _proj_rope(x_b, cols(5, 7), cos_t, sin_t, scale_row, seq)
    oa = _hgrn(qvg.reshape(bsz, seq, -1), zz.reshape(bsz, seq, -1), lb, norm_w, n_heads_a)
    ob = _dilated_attention(qk.reshape(bsz, seq, 2 * aw), vb.reshape(bsz, seq, aw))
    w_out = w_out.astype(BF16)
    x_f, x_b = _out_ln([oa.reshape(bsz * seq, aw), ob.reshape(bsz * seq, aw)], [w_out[:aw], w_out[aw:]], x_f,
                       ln1[0], ln1[1], alpha)
    return _ffn_ln(x_b, x_f, w1[layer].astype(BF16), w3[layer].astype(BF16), w2[layer].astype(BF16), ln2[0], ln2[1],
                   alpha)


def _moe_dispatch(idx, n_tokens, tm):
    e_flat = idx[:, :2].reshape(-1)
    onehot = (e_flat[None, :] == jnp.arange(N_EXPERTS, dtype=jnp.int32)[:, None]).astype(jnp.int32)
    counts = jnp.sum(onehot, axis=1)
    tiles = (counts + tm - 1) // tm
    tile_end = jnp.cumsum(tiles)
    group_start = (tile_end - tiles) * tm
    dest = jnp.sum((jnp.cumsum(onehot, axis=1) - 1 + group_start[:, None]) * onehot, axis=0)
    n_tiles = (2 * n_tokens) // tm + N_EXPERTS
    src_tok = jnp.zeros((n_tiles * tm,), jnp.int32).at[dest].set(jnp.arange(2 * n_tokens, dtype=jnp.int32) // 2,
                                                                 unique_indices=True, mode="promise_in_bounds")
    tile_ids = jnp.arange(n_tiles, dtype=jnp.int32)
    tile_expert = jnp.minimum(jnp.sum((tile_ids[:, None] >= tile_end[None, :]).astype(jnp.int32), axis=1),
                              N_EXPERTS - 1)
    return src_tok, dest.reshape(n_tokens, 2), tile_expert, tile_end[-1:].astype(jnp.int32)


def _odd_layer(x_f, x_b, bsz, seq, w_in, lam_params, subln_w, w_out, ln1, router, w1, w3, w2, layer, ln2, rope,
               alpha, lambda_init):
    d = x_f.shape[1]
    n_tok = bsz * seq
    w_in = w_in.astype(BF16)
    cos_t, sin_t = rope
    scale_row = jnp.concatenate([jnp.full((1, d), LOG2_E * C_HEAD_DIM ** -0.5, F32), jnp.ones((1, d), F32)], axis=1)
    qk = _proj_rope(x_b, w_in[:, :2 * d], cos_t, sin_t, scale_row, seq)
    v = _proj(x_b, w_in[:, 2 * d:], BF16)
    o = _diff_attention(qk.reshape(bsz, seq, 2 * d), v.reshape(bsz, seq, d), lam_params.astype(F32), subln_w,
                        lambda_init)
    router_padded = jnp.pad(router.astype(F32), ((0, 0), (0, LANES - N_EXPERTS)))
    x_f, x_b, gates, idx = _out_ln([o.reshape(n_tok, d)], [w_out.astype(BF16)], x_f, ln1[0], ln1[1], alpha,
                                   router_padded)
    tm = _tile(n_tok, (1024, 512, 256))
    src_tok, pos, tile_expert, n_active = _moe_dispatch(idx, n_tok, tm)
    n_tiles = src_tok.shape[0] // tm
    w1, w3, w2 = (_layer_weights_bf16(w, layer) for w in (w1, w3, w2))
    y = None
    for lo, hi in ((0, n_tiles // 2), (n_tiles // 2, n_tiles)):
        x_rows = jnp.take(x_b, src_tok[lo * tm:hi * tm], axis=0, mode="clip")
        n_act = jnp.clip(n_active - lo, 0, hi - lo)
        y = _moe_ffn(x_rows, tile_expert[lo:hi], n_act, w1, w3, w2, tm, lo, n_tiles * tm, y)
    y1 = jnp.take(y, pos[:, 0], axis=0, mode="clip")
    y2 = jnp.take(y, pos[:, 1], axis=0, mode="clip")
    return _moe_combine(x_f, y1, y2, gates, ln2[0], ln2[1], alpha)


def kernel(x, ev_w_in, ev_lb_logits, ev_hgrn_norm, ev_w_out, ev_ln1_g, ev_ln1_b, ev_w1, ev_w3, ev_w2, ev_ln2_g,
           ev_ln2_b, od_w_in, od_lambda, od_subln, od_w_out, od_ln1_g, od_ln1_b, od_router, od_w1, od_w3, od_w2,
           od_ln2_g, od_ln2_b):
    bsz, seq, d = x.shape
    depth = ev_w_in.shape[0] + od_w_in.shape[0]
    alpha = (2 * depth) ** 0.25
    rope = _rope_tables(seq, LANES)
    lb_soft = jax.nn.softmax(ev_lb_logits.astype(F32), axis=0)
    lower_bounds = jnp.cumsum(lb_soft, axis=0) - lb_soft[0]
    x_f = x.reshape(bsz * seq, d).astype(F32)
    x_b = x_f.astype(BF16)
    for layer in range(depth):
        j = layer // 2
        if layer % 2 == 0:
            x_f, x_b = _even_layer(x_f, x_b, bsz, seq, ev_w_in[j], lower_bounds[j], ev_hgrn_norm[j], ev_w_out[j],
                                   (ev_ln1_g[j], ev_ln1_b[j]), ev_w1, ev_w3, ev_w2, j,
                                   (ev_ln2_g[j], ev_ln2_b[j]), rope, alpha)
        else:
            lambda_init = 0.8 - 0.6 * math.exp(-0.3 * layer)
            x_f, x_b = _odd_layer(x_f, x_b, bsz, seq, od_w_in[j], od_lambda[j], od_subln[j], od_w_out[j],
                                  (od_ln1_g[j], od_ln1_b[j]), od_router[j], od_w1, od_w3, od_w2, j,
                                  (od_ln2_g[j], od_ln2_b[j]), rope, alpha, lambda_init)
    return x_f.reshape(bsz, seq, d).astype(x.dtype)
```

```python
import functools
import math

import numpy as np
import jax
import jax.numpy as jnp
from jax import lax
from jax.experimental import pallas as pl
from jax.experimental.pallas import tpu as pltpu

F32 = jnp.float32
BF16 = jnp.bfloat16

A_HEAD_DIM = 128
B_HEAD_DIM = 64
B_CONFIGS = ((128, 1), (512, 4), (2048, 16))
BAND_RADIUS = 64
C_HEAD_DIM = 64
N_EXPERTS = 8
ROPE_THETA = 10000.0
LN_EPS = 1e-5
RMS_EPS = 1e-5
MASK_VALUE = -1e30
MIN_FORGET = 1e-30
LOG2_E = math.log2(math.e)

LANES = 128
SUBLANES = 8
VMEM_LIMIT_BYTES = 56 * 1024 * 1024

ROW_CHAINS = 4
FFN_ROW_CHAINS = 2

HGRN_CHUNK = 128
HGRN_LEVELS = 7
HGRN_HEADS_PER_STEP = 2


def _cparams(*sem):
    return pltpu.CompilerParams(dimension_semantics=sem, vmem_limit_bytes=VMEM_LIMIT_BYTES)


def _tile(n, prefs):
    for p in prefs:
        if n % p == 0:
            return p
    return n


def _dot(a, b):
    return jnp.dot(a, b, preferred_element_type=F32)


def _dot_nt(a, b):
    return lax.dot_general(a, b, (((1,), (1,)), ((), ())), preferred_element_type=F32)


def _dot_tn(a, b):
    return lax.dot_general(a, b, (((0,), (0,)), ((), ())), preferred_element_type=F32)


def _sigmoid(x):
    return 1.0 / (1.0 + jnp.exp(-x))


def _layer_norm(y, g, b):
    mu = jnp.mean(y, axis=-1, keepdims=True)
    d = y - mu
    var = jnp.mean(d * d, axis=-1, keepdims=True)
    return d * lax.rsqrt(var + LN_EPS) * g + b


def _proj_kernel(x_ref, w_ref, o_ref):
    o_ref[...] = _dot(x_ref[...], w_ref[...]).astype(o_ref.dtype)


def _proj_multi_kernel(x_ref, w_ref, *o_refs):
    x = x_ref[...]
    col = 0
    for o_ref in o_refs:
        n = o_ref.shape[1]
        o_ref[...] = _dot(x, w_ref[:, col:col + n]).astype(o_ref.dtype)
        col += n


def _proj_multi(x, w, widths, dtypes):
    m, k = x.shape
    tm = _tile(m, (1024, 512, 256))
    return pl.pallas_call(
        _proj_multi_kernel,
        grid=(m // tm,),
        in_specs=[pl.BlockSpec((tm, k), lambda i: (i, 0)), pl.BlockSpec(w.shape, lambda i: (0, 0))],
        out_specs=[pl.BlockSpec((tm, n), lambda i: (i, 0)) for n in widths],
        out_shape=[jax.ShapeDtypeStruct((m, n), dt) for n, dt in zip(widths, dtypes)],
        compiler_params=_cparams("parallel"),
        name="proj_multi",
    )(x, w)


def _proj(x, w, out_dtype):
    m, k = x.shape
    n = w.shape[1]
    tm = _tile(m, (1024, 512, 256))
    tn = n if n <= 1536 else _tile(n, (1024, 512, 256, 128))
    return pl.pallas_call(
        _proj_kernel,
        grid=(m // tm, n // tn),
        in_specs=[pl.BlockSpec((tm, k), lambda i, j: (i, 0)), pl.BlockSpec((k, tn), lambda i, j: (0, j))],
        out_specs=pl.BlockSpec((tm, tn), lambda i, j: (i, j)),
        out_shape=jax.ShapeDtypeStruct((m, n), out_dtype),
        compiler_params=_cparams("parallel", "arbitrary"),
        name="proj",
    )(x, w)


def _proj_rope_kernel(x_ref, w_ref, cos_ref, sin_ref, scale_ref, o_ref):
    tm, tn = o_ref.shape
    sub = tm // ROW_CHAINS
    w = w_ref[...]
    lane = lax.broadcasted_iota(jnp.int32, (sub, tn), 1)
    first_half = (lane % B_HEAD_DIM) < (B_HEAD_DIM // 2)
    reps = tn // cos_ref.shape[1]
    for k in range(ROW_CHAINS):
        rows = pl.ds(k * sub, sub)
        acc = _dot(x_ref[rows, :], w)
        partner = jnp.where(first_half, pltpu.roll(acc, tn - B_HEAD_DIM // 2, 1), pltpu.roll(acc, B_HEAD_DIM // 2, 1))
        cos = jnp.tile(cos_ref[rows, :], (1, reps))
        sin = jnp.tile(sin_ref[rows, :], (1, reps))
        o_ref[rows, :] = ((acc * cos + partner * sin) * scale_ref[...]).astype(o_ref.dtype)


def _proj_rope(x, w, cos_t, sin_t, scale_row, seq):
    m, k = x.shape
    n = w.shape[1]
    tm = _tile(seq, (1024, 512, 256))
    tn = _tile(n, (1024, 512, 256, 128))
    nsb = seq // tm
    return pl.pallas_call(
        _proj_rope_kernel,
        grid=(m // tm, n // tn),
        in_specs=[
            pl.BlockSpec((tm, k), lambda i, j: (i, 0)),
            pl.BlockSpec((k, tn), lambda i, j: (0, j)),
            pl.BlockSpec((tm, LANES), lambda i, j: (i % nsb, 0)),
            pl.BlockSpec((tm, LANES), lambda i, j: (i % nsb, 0)),
            pl.BlockSpec((1, tn), lambda i, j: (0, j)),
        ],
        out_specs=pl.BlockSpec((tm, tn), lambda i, j: (i, j)),
        out_shape=jax.ShapeDtypeStruct((m, n), BF16),
        compiler_params=_cparams("parallel", "arbitrary"),
        name="proj_rope",
    )(x, w, cos_t, sin_t, scale_row)


def _top2_gates(logits):
    lane = lax.broadcasted_iota(jnp.int32, logits.shape, 1)
    neg = jnp.float32(-jnp.inf)
    logits = jnp.where(lane < N_EXPERTS, logits, neg)
    v1 = jnp.max(logits, axis=-1, keepdims=True)
    i1 = jnp.min(jnp.where(logits == v1, lane, LANES), axis=-1, keepdims=True)
    rest = jnp.where(lane == i1, neg, logits)
    v2 = jnp.max(rest, axis=-1, keepdims=True)
    i2 = jnp.min(jnp.where(rest == v2, lane, LANES), axis=-1, keepdims=True)
    e = jnp.exp(v2 - v1)
    g1 = 1.0 / (1.0 + e)
    g2 = e / (1.0 + e)
    return (jnp.where(lane == 0, g1, jnp.where(lane == 1, g2, 0.0)),
            jnp.where(lane == 0, i1, jnp.where(lane == 1, i2, 0)))


def _out_ln_kernel(*refs, n_in, alpha, with_router):
    xs = refs[:n_in]
    ws = refs[n_in:2 * n_in]
    if with_router:
        resid_ref, g_ref, b_ref, r_ref, of_ref, ob_ref, gate_ref, idx_ref = refs[2 * n_in:]
    else:
        resid_ref, g_ref, b_ref, of_ref, ob_ref = refs[2 * n_in:]
    sub = of_ref.shape[0] // ROW_CHAINS
    for k in range(ROW_CHAINS):
        rows = pl.ds(k * sub, sub)
        acc = _dot(xs[0][rows, :], ws[0][...])
        for x_ref, w_ref in zip(xs[1:], ws[1:]):
            acc = acc + _dot(x_ref[rows, :], w_ref[...])
        z = _layer_norm(alpha * resid_ref[rows, :] + acc, g_ref[...], b_ref[...])
        of_ref[rows, :] = z
        ob_ref[rows, :] = z.astype(BF16)
        if with_router:
            logits = jnp.dot(z, r_ref[...], precision=lax.Precision.HIGHEST, preferred_element_type=F32)
            gate_ref[rows, :], idx_ref[rows, :] = _top2_gates(logits)


def _out_ln(xs, ws, resid, g, b, alpha, router_padded=None):
    m, d = resid.shape
    tm = _tile(m, (1024, 512, 256))
    n_in = len(xs)
    row = lambda i: (i, 0)
    fixed = lambda i: (0, 0)
    in_specs = [pl.BlockSpec((tm, x.shape[1]), row) for x in xs]
    in_specs += [pl.BlockSpec(w.shape, fixed) for w in ws]
    in_specs += [pl.BlockSpec((tm, d), row), pl.BlockSpec((1, d), fixed), pl.BlockSpec((1, d), fixed)]
    operands = [*xs, *ws, resid, g.reshape(1, d), b.reshape(1, d)]
    out_specs = [pl.BlockSpec((tm, d), row), pl.BlockSpec((tm, d), row)]
    out_shape = [jax.ShapeDtypeStruct((m, d), F32), jax.ShapeDtypeStruct((m, d), BF16)]
    if router_padded is not None:
        in_specs.append(pl.BlockSpec((d, LANES), fixed))
        operands.append(router_padded)
        out_specs += [pl.BlockSpec((tm, LANES), row), pl.BlockSpec((tm, LANES), row)]
        out_shape += [jax.ShapeDtypeStruct((m, LANES), F32), jax.ShapeDtypeStruct((m, LANES), jnp.int32)]
    return pl.pallas_call(
        functools.partial(_out_ln_kernel, n_in=n_in, alpha=alpha, with_router=router_padded is not None),
        grid=(m // tm,),
        in_specs=in_specs,
        out_specs=out_specs,
        out_shape=out_shape,
        compiler_params=_cparams("parallel"),
        name="out_ln",
    )(*operands)


def _swiglu_accumulate(x_ref, w1, w3, w2, acc_ref):
    sub = x_ref.shape[0] // FFN_ROW_CHAINS
    for k in range(FFN_ROW_CHAINS):
        rows = pl.ds(k * sub, sub)
        h1 = _dot(x_ref[rows, :], w1)
        h3 = _dot(x_ref[rows, :], w3)
        h = (h1 * _sigmoid(h1)) * h3
        acc_ref[rows, :] += _dot(h.astype(BF16), w2)


def _ffn_ln_kernel(x_ref, w1_ref, w3_ref, w2_ref, resid_ref, g_ref, b_ref, of_ref, ob_ref, acc_ref, *, alpha):
    j = pl.program_id(1)

    @pl.when(j == 0)
    def _():
        acc_ref[...] = jnp.zeros_like(acc_ref)

    _swiglu_accumulate(x_ref, w1_ref[...], w3_ref[...], w2_ref[...], acc_ref)

    @pl.when(j == pl.num_programs(1) - 1)
    def _():
        z = _layer_norm(alpha * resid_ref[...] + acc_ref[...], g_ref[...], b_ref[...])
        of_ref[...] = z
        ob_ref[...] = z.astype(BF16)


def _ffn_ln(x_bf, resid, w1, w3, w2, g, b, alpha):
    m, d = resid.shape
    ff = w1.shape[1]
    tm = _tile(m, (1024, 512, 256))
    tf = _tile(ff, (1408, 256, 128))
    return pl.pallas_call(
        functools.partial(_ffn_ln_kernel, alpha=alpha),
        grid=(m // tm, ff // tf),
        in_specs=[
            pl.BlockSpec((tm, d), lambda i, j: (i, 0)),
            pl.BlockSpec((d, tf), lambda i, j: (0, j)),
            pl.BlockSpec((d, tf), lambda i, j: (0, j)),
            pl.BlockSpec((tf, d), lambda i, j: (j, 0)),
            pl.BlockSpec((tm, d), lambda i, j: (i, 0)),
            pl.BlockSpec((1, d), lambda i, j: (0, 0)),
            pl.BlockSpec((1, d), lambda i, j: (0, 0)),
        ],
        out_specs=[pl.BlockSpec((tm, d), lambda i, j: (i, 0)), pl.BlockSpec((tm, d), lambda i, j: (i, 0))],
        out_shape=[jax.ShapeDtypeStruct((m, d), F32), jax.ShapeDtypeStruct((m, d), BF16)],
        scratch_shapes=[pltpu.VMEM((tm, d), F32)],
        compiler_params=_cparams("parallel", "arbitrary"),
        name="ffn_ln",
    )(x_bf, w1, w3, w2, resid, g.reshape(1, d), b.reshape(1, d))


def _cast_kernel(w_ref, o_ref):
    o_ref[...] = w_ref[...].astype(o_ref.dtype)


def _layer_weights_bf16(w, layer):
    cols = w.shape[-1]
    rows = math.prod(w.shape[1:-1])
    flat = w.reshape(w.shape[0] * rows, cols)
    tr = _tile(rows, (512, 256, 128))
    steps = rows // tr
    out = pl.pallas_call(
        _cast_kernel,
        grid=(steps,),
        in_specs=[pl.BlockSpec((tr, cols), lambda i: (layer * steps + i, 0))],
        out_specs=pl.BlockSpec((tr, cols), lambda i: (i, 0)),
        out_shape=jax.ShapeDtypeStruct((rows, cols), BF16),
        compiler_params=_cparams("parallel"),
        name="cast_weights",
    )(flat)
    return out.reshape(w.shape[1:])


def _moe_ffn_kernel(te_ref, na_ref, x_ref, w1_ref, w3_ref, w2_ref, *rest):
    o_ref, acc_ref = rest[-2:]
    i = pl.program_id(0)
    j = pl.program_id(1)
    active = i < na_ref[0]

    @pl.when(active & (j == 0))
    def _():
        acc_ref[...] = jnp.zeros_like(acc_ref)

    @pl.when(active)
    def _():
        _swiglu_accumulate(x_ref, w1_ref[0], w3_ref[0], w2_ref[0], acc_ref)

    @pl.when(active & (j == pl.num_programs(1) - 1))
    def _():
        o_ref[...] = acc_ref[...].astype(o_ref.dtype)


def _moe_ffn(x_sorted, tile_expert, n_active, w1, w3, w2, tm, tile_offset, total_rows, y_prev=None):
    p, d = x_sorted.shape
    ff = w1.shape[2]
    tf = _tile(ff, (1408, 256, 128))
    nf = ff // tf

    def row(i, na):
        return jnp.maximum(jnp.minimum(i, na[0] - 1), 0)

    def col(i, j, na):
        return jnp.where(i < na[0], j, nf - 1)

    operands = [tile_expert, n_active, x_sorted, w1, w3, w2]
    in_specs = [
        pl.BlockSpec((tm, d), lambda i, j, te, na: (row(i, na), 0)),
        pl.BlockSpec((1, d, tf), lambda i, j, te, na: (te[row(i, na)], 0, col(i, j, na))),
        pl.BlockSpec((1, d, tf), lambda i, j, te, na: (te[row(i, na)], 0, col(i, j, na))),
        pl.BlockSpec((1, tf, d), lambda i, j, te, na: (te[row(i, na)], col(i, j, na), 0)),
    ]
    aliases = {}
    if y_prev is not None:
        in_specs.append(pl.BlockSpec(memory_space=pl.ANY))
        aliases = {len(operands): 0}
        operands.append(y_prev)
    grid_spec = pltpu.PrefetchScalarGridSpec(
        num_scalar_prefetch=2,
        grid=(p // tm, nf),
        in_specs=in_specs,
        out_specs=pl.BlockSpec((tm, d), lambda i, j, te, na: (tile_offset + row(i, na), 0)),
        scratch_shapes=[pltpu.VMEM((tm, d), F32)],
    )
    return pl.pallas_call(
        _moe_ffn_kernel,
        grid_spec=grid_spec,
        out_shape=jax.ShapeDtypeStruct((total_rows, d), BF16),
        input_output_aliases=aliases,
        compiler_params=_cparams("arbitrary", "arbitrary"),
        name="moe_ffn",
    )(*operands)


def _moe_combine_kernel(resid_ref, y_ref, gate_ref, g_ref, b_ref, of_ref, ob_ref, *, alpha):
    gates = gate_ref[...]
    d = resid_ref.shape[1]
    y = gates[:, 0:1] * y_ref[:, 0:d].astype(F32) + gates[:, 1:2] * y_ref[:, d:2 * d].astype(F32)
    z = _layer_norm(alpha * resid_ref[...] + y, g_ref[...], b_ref[...])
    of_ref[...] = z
    ob_ref[...] = z.astype(BF16)


def _moe_combine(resid, y_pair, gates, g, b, alpha):
    m, d = resid.shape
    tm = _tile(m, (512, 256))
    row = lambda i: (i, 0)
    fixed = lambda i: (0, 0)
    return pl.pallas_call(
        functools.partial(_moe_combine_kernel, alpha=alpha),
        grid=(m // tm,),
        in_specs=[pl.BlockSpec((tm, d), row), pl.BlockSpec((tm, 2 * d), row),
                  pl.BlockSpec((tm, LANES), row), pl.BlockSpec((1, d), fixed), pl.BlockSpec((1, d), fixed)],
        out_specs=[pl.BlockSpec((tm, d), row), pl.BlockSpec((tm, d), row)],
        out_shape=[jax.ShapeDtypeStruct((m, d), F32), jax.ShapeDtypeStruct((m, d), BF16)],
        compiler_params=_cparams("parallel"),
        name="moe_combine",
    )(resid, y_pair, gates, g.reshape(1, d), b.reshape(1, d))


def _hgrn_masks(c, levels):
    t = np.arange(c)[:, None]
    s = np.arange(c)[None, :]
    tri = np.stack([(s <= t), (s >= t)]).astype(np.float32)
    fwd, bwd = [], []
    for l in range(levels):
        same = (t >> (l + 1)) == (s >> (l + 1))
        t_up = ((t >> l) & 1) == 1
        s_up = ((s >> l) & 1) == 1
        fwd.append(same & t_up & ~s_up)
        bwd.append(same & ~t_up & s_up)
    fwd.append(t == s)
    bwd.append(t == s)
    up = np.stack([np.broadcast_to(((t >> l) & 1) == 1, (c, A_HEAD_DIM)) for l in range(levels)])
    sign = np.stack([np.where(up, 1.0, -1.0), np.where(up, -1.0, 1.0)]).astype(np.float32)
    return tri, np.stack([np.stack(fwd), np.stack(bwd)]).astype(np.float32), sign


def _segment_reference(x, level, forward):
    c, w = x.shape
    half = 1 << level
    seg = 2 * half
    idx = half - 1 if forward else half
    if seg >= SUBLANES:
        xr = x.reshape(c // seg, seg, w)
        return jnp.broadcast_to(xr[:, idx:idx + 1, :], xr.shape).reshape(c, w)
    x3 = x.reshape(c // SUBLANES, SUBLANES, w)
    sub = lax.broadcasted_iota(jnp.int32, x3.shape, 1)
    r3 = jnp.broadcast_to(x3[:, idx:idx + 1, :], x3.shape)
    for j in range(1, SUBLANES // seg):
        row = j * seg + idx
        r3 = jnp.where(sub >= j * seg, jnp.broadcast_to(x3[:, row:row + 1, :], x3.shape), r3)
    return r3.reshape(c, w)


def _hgrn_chunk(q, k, v, log_f, state_t, tri, masks, signs, forward):
    c = q.shape[0]
    x = jnp.dot(tri, log_f, precision=lax.Precision.HIGHEST, preferred_element_type=F32)
    scores = masks[HGRN_LEVELS] * _dot_nt(q.astype(BF16), k.astype(BF16))
    for level in range(HGRN_LEVELS):
        ref = _segment_reference(x, level, forward)
        sign = signs[level]
        decay = jnp.exp(sign * (x - ref))
        z = (jnp.where(sign > 0, q, k) * decay).astype(BF16)
        scores = scores + masks[level] * _dot_nt(z, z)
    x_end = x[c - 1:c, :] if forward else x[0:1, :]
    q_dec = (q * jnp.exp(x)).astype(BF16)
    o = _dot(scores.astype(BF16), v.astype(BF16)) + _dot_nt(q_dec, state_t.astype(BF16))
    k_dec = (k * jnp.exp(x_end - x)).astype(BF16)
    new_state_t = state_t * jnp.exp(x_end) + _dot_tn(v.astype(BF16), k_dec)
    return o, new_state_t


def _hgrn_kernel(q_ref, v_ref, g_ref, zf_ref, zb_ref, lb_ref, nw_ref, tri_ref, msk_ref, sgn_ref, o_ref, acc_ref, *,
                 n_chunks):
    c = HGRN_CHUNK
    hd = A_HEAD_DIM

    def gates(z, lb):
        e = jnp.exp(-jnp.abs(z))
        inv = 1.0 / (1.0 + e)
        pos = z >= 0
        sig = jnp.where(pos, inv, e * inv)
        sig_neg = jnp.where(pos, e * inv, inv)
        f = lb + (1.0 - lb) * sig
        return jnp.log(jnp.maximum(f, MIN_FORGET)), (1.0 - lb) * sig_neg

    def load(ref, c0, h):
        return ref[0, pl.ds(c0, c), h * hd:(h + 1) * hd].astype(F32)

    def emit(c0, h, tot):
        ms = jnp.mean(tot * tot, axis=-1, keepdims=True)
        g = load(g_ref, c0, h)
        out = tot * lax.rsqrt(ms + RMS_EPS) * nw_ref[...] * (g * _sigmoid(g))
        o_ref[0, pl.ds(c0, c), h * hd:(h + 1) * hd] = out.astype(o_ref.dtype)

    def one_direction(h, c0, z_ref, state_t, direction):
        lb = lb_ref[:, h * hd:(h + 1) * hd]
        log_f, k = gates(load(z_ref, c0, h), lb)
        masks = [msk_ref[direction, l] for l in range(HGRN_LEVELS + 1)]
        signs = [sgn_ref[direction, l] for l in range(HGRN_LEVELS)]
        return _hgrn_chunk(load(q_ref, c0, h), k, load(v_ref, c0, h), log_f, state_t, tri_ref[direction], masks,
                           signs, direction == 0)

    def step(i, states, finalize):
        cf = pl.multiple_of(i * c, c)
        cb = pl.multiple_of((n_chunks - 1 - i) * c, c)
        new_states = []
        for h in range(HGRN_HEADS_PER_STEP):
            o_f, st_f = one_direction(h, cf, zf_ref, states[2 * h], 0)
            o_b, st_b = one_direction(h, cb, zb_ref, states[2 * h + 1], 1)
            cols = slice(h * hd, (h + 1) * hd)
            if finalize:
                emit(cf, h, acc_ref[pl.ds(cf, c), cols] + o_f)
                emit(cb, h, acc_ref[pl.ds(cb, c), cols] + o_b)
            else:
                acc_ref[pl.ds(cf, c), cols] = o_f
                acc_ref[pl.ds(cb, c), cols] = o_b
            new_states += [st_f, st_b]
        return tuple(new_states)

    zero = jnp.zeros((hd, hd), F32)
    half = n_chunks // 2
    states = lax.fori_loop(0, half, functools.partial(step, finalize=False), (zero,) * (2 * HGRN_HEADS_PER_STEP))
    lax.fori_loop(half, n_chunks, functools.partial(step, finalize=True), states)


def _hgrn(qvg, zz, lb, norm_w, n_heads):
    bsz, s, _ = qvg.shape
    c = HGRN_CHUNK
    assert s % (2 * c) == 0 and n_heads % HGRN_HEADS_PER_STEP == 0
    tri, masks, signs = _hgrn_masks(c, HGRN_LEVELS)
    hd = A_HEAD_DIM
    wide = HGRN_HEADS_PER_STEP * hd
    n_groups = n_heads // HGRN_HEADS_PER_STEP
    col = lambda off: pl.BlockSpec((1, s, wide), lambda b, h: (b, 0, off + h))
    const = lambda a: pl.BlockSpec(a.shape, lambda b, h: (0,) * a.ndim)
    return pl.pallas_call(
        functools.partial(_hgrn_kernel, n_chunks=s // c),
        grid=(bsz, n_groups),
        in_specs=[
            col(0), col(n_groups), col(2 * n_groups), col(0), col(n_groups),
            pl.BlockSpec((1, wide), lambda b, h: (0, h)),
            pl.BlockSpec((1, hd), lambda b, h: (0, 0)),
            const(tri), const(masks), const(signs),
        ],
        out_specs=pl.BlockSpec((1, s, wide), lambda b, h: (b, 0, h)),
        out_shape=jax.ShapeDtypeStruct((bsz, s, n_heads * hd), BF16),
        scratch_shapes=[pltpu.VMEM((s, wide), F32)],
        compiler_params=_cparams("parallel", "parallel"),
        name="hgrn2",
    )(qvg, qvg, qvg, zz, zz, lb.reshape(1, -1), norm_w.reshape(1, hd), jnp.asarray(tri), jnp.asarray(masks),
      jnp.asarray(signs))


def _dilated_kernel(q_ref, k_ref, v_ref, o_ref, qf_ref, kf_ref, vf_ref, oc_ref, lc_ref, *, seq):
    qf_ref[...] = q_ref[0].astype(F32)
    kf_ref[...] = k_ref[0].astype(F32)
    vf_ref[...] = v_ref[0].astype(F32)
    n_cfg = len(B_CONFIGS)
    for ci, (_, dil) in enumerate(B_CONFIGS):
        length = seq // dil
        tq = length if length <= LANES + 2 * BAND_RADIUS else LANES
        win = min(length, tq + 2 * BAND_RADIUS)
        head0 = lax.broadcasted_iota(jnp.int32, (tq, LANES), 1) < B_HEAD_DIM
        rel = lax.broadcasted_iota(jnp.int32, (tq, win), 1) - lax.broadcasted_iota(jnp.int32, (tq, win), 0)

        def rows(first, size, dil=dil):
            return pl.ds(first, size) if dil == 1 else pl.ds(first, size, stride=dil)

        def block(t, carry, ci=ci, dil=dil, length=length, tq=tq, win=win, head0=head0, rel=rel, rows=rows):
            res = t % dil
            q0 = (t // dil) * tq
            start = jnp.clip(q0 - BAND_RADIUS, 0, length - win)
            valid = jnp.abs(rel + (start - q0)) <= BAND_RADIUS
            q_rows = rows(q0 * dil + res, tq)
            k_rows = rows(start * dil + res, win)
            q = qf_ref[q_rows, :].astype(BF16)
            kw = kf_ref[k_rows, :].astype(BF16)
            vw = vf_ref[k_rows, :].astype(BF16)

            def one_head(mask):
                s = _dot_nt(jnp.where(mask, q, jnp.zeros_like(q)), kw)
                s = jnp.where(valid, s, MASK_VALUE)
                m = jnp.max(s, axis=-1, keepdims=True)
                p = jnp.exp2(s - m)
                l = jnp.sum(p, axis=-1, keepdims=True)
                return _dot(p.astype(BF16), vw) / l, m + jnp.log(l) * LOG2_E

            oa, la = one_head(head0)
            ob, lb = one_head(jnp.logical_not(head0))
            oc_ref[ci, q_rows, :] = jnp.where(head0, oa, ob)
            lc_ref[ci, q_rows, :] = jnp.where(head0, la, lb)
            return carry

        n_blocks = dil * (length // tq)
        lax.fori_loop(0, n_blocks, block, 0, unroll=max(1, min(8, 8 * LANES // tq, n_blocks)))

    tmix = min(256, seq)

    def mix(i, carry):
        r0 = pl.multiple_of(i * tmix, tmix)
        lses = [lc_ref[c, pl.ds(r0, tmix), :] for c in range(n_cfg)]
        top = lses[0]
        for l in lses[1:]:
            top = jnp.maximum(top, l)
        num = jnp.zeros((tmix, LANES), F32)
        den = jnp.zeros((tmix, LANES), F32)
        for c in range(n_cfg):
            w = jnp.exp2(lses[c] - top)
            num = num + w * oc_ref[c, pl.ds(r0, tmix), :]
            den = den + w
        o_ref[0, pl.ds(r0, tmix), :] = (num / den).astype(o_ref.dtype)
        return carry

    lax.fori_loop(0, seq // tmix, mix, 0)


def _dilated_attention(qk, v):
    bsz, s, w = v.shape
    n_pairs = w // LANES
    n_cfg = len(B_CONFIGS)
    return pl.pallas_call(
        functools.partial(_dilated_kernel, seq=s),
        grid=(bsz, n_pairs),
        in_specs=[
            pl.BlockSpec((1, s, LANES), lambda b, h: (b, 0, h)),
            pl.BlockSpec((1, s, LANES), lambda b, h: (b, 0, n_pairs + h)),
            pl.BlockSpec((1, s, LANES), lambda b, h: (b, 0, h)),
        ],
        out_specs=pl.BlockSpec((1, s, LANES), lambda b, h: (b, 0, h)),
        out_shape=jax.ShapeDtypeStruct((bsz, s, w), BF16),
        scratch_shapes=[pltpu.VMEM((s, LANES), F32)] * 3 + [pltpu.VMEM((n_cfg, s, LANES), F32)] * 2,
        compiler_params=_cparams("parallel", "parallel"),
        name="dilated_attn",
    )(qk, qk, v)


def _diff_kernel(q_ref, k_ref, v_ref, lam_ref, sub_ref, o_ref, *, tk, lambda_init):
    q = q_ref[0]
    tq = q.shape[0]
    s_len = k_ref.shape[1]
    lane = lax.broadcasted_iota(jnp.int32, q.shape, 1)
    zero = jnp.zeros_like(q)
    qs = (jnp.where(lane < C_HEAD_DIM, q, zero), jnp.where(lane >= C_HEAD_DIM, q, zero))
    tiles = [(t * LANES, (t + 1) * LANES) for t in range(tk // LANES)]
    m = [jnp.full((tq, 1), -jnp.inf, F32)] * 2
    l = [jnp.zeros((tq, 1), F32)] * 2
    acc = [jnp.zeros((tq, LANES), F32)] * 2
    for c in range(s_len // tk):
        lo, hi = c * tk, (c + 1) * tk
        for h in range(2):
            s = _dot_nt(qs[h], k_ref[0, lo:hi, :])
            m_tile = s[:, 0:LANES]
            for a, b in tiles[1:]:
                m_tile = jnp.maximum(m_tile, s[:, a:b])
            m_new = jnp.maximum(m[h], jnp.max(m_tile, axis=-1, keepdims=True))
            alpha = jnp.exp2(m[h] - m_new)
            p = jnp.exp2(s - m_new)
            l_tile = p[:, 0:LANES]
            for a, b in tiles[1:]:
                l_tile = l_tile + p[:, a:b]
            l[h] = alpha * l[h] + jnp.sum(l_tile, axis=-1, keepdims=True)
            acc[h] = alpha * acc[h] + _dot(p.astype(BF16), v_ref[0, lo:hi, :])
            m[h] = m_new
    outs = (acc[0] / l[0], acc[1] / l[1])
    lp = lam_ref[...]
    lam = (jnp.exp(jnp.sum(lp[0:1] * lp[1:2], axis=-1, keepdims=True))
           - jnp.exp(jnp.sum(lp[2:3] * lp[3:4], axis=-1, keepdims=True)) + lambda_init)
    o = outs[0] - lam * outs[1]
    ms_o = jnp.mean(o * o, axis=-1, keepdims=True)
    o_ref[0] = (o * lax.rsqrt(ms_o + RMS_EPS) * sub_ref[...] * (1.0 - lambda_init)).astype(o_ref.dtype)


def _diff_attention(qk, v, lam_params, subln_w, lambda_init):
    bsz, s, w = v.shape
    n_heads = w // LANES
    tq = _tile(s, (1024, 512, 256, 128))
    tk = _tile(s, (2048, 1024, 512, 256, 128))
    return pl.pallas_call(
        functools.partial(_diff_kernel, tk=tk, lambda_init=lambda_init),
        grid=(bsz, n_heads, s // tq),
        in_specs=[
            pl.BlockSpec((1, tq, LANES), lambda b, h, i: (b, i, h)),
            pl.BlockSpec((1, s, LANES), lambda b, h, i: (b, 0, n_heads + h)),
            pl.BlockSpec((1, s, LANES), lambda b, h, i: (b, 0, h)),
            pl.BlockSpec(lam_params.shape, lambda b, h, i: (0, 0)),
            pl.BlockSpec((1, LANES), lambda b, h, i: (0, 0)),
        ],
        out_specs=pl.BlockSpec((1, tq, LANES), lambda b, h, i: (b, i, h)),
        out_shape=jax.ShapeDtypeStruct((bsz, s, w), BF16),
        compiler_params=_cparams("parallel", "parallel", "arbitrary"),
        name="diff_attn",
    )(qk, qk, v, lam_params, subln_w.reshape(1, LANES))


def _rope_tables(seq, width):
    half = B_HEAD_DIM // 2
    inv = ROPE_THETA ** (-jnp.arange(0, B_HEAD_DIM, 2, dtype=F32) / B_HEAD_DIM)
    ang = jnp.arange(seq, dtype=F32)[:, None] * inv[None, :]
    cos, sin = jnp.cos(ang), jnp.sin(ang)
    reps = width // B_HEAD_DIM
    assert half * 2 == B_HEAD_DIM
    return jnp.tile(jnp.concatenate([cos, cos], axis=1), (1, reps)), jnp.tile(jnp.concatenate([-sin, sin], axis=1), (1, reps))


def _even_layer(x_f, x_b, bsz, seq, w_in, lb, norm_w, w_out, ln1, w1, w3, w2, layer, ln2, rope, alpha):
    d = x_f.shape[1]
    aw = d // 2
    n_heads_a = aw // A_HEAD_DIM
    w_in = w_in.astype(BF16)
    cols = lambda a, b: w_in[:, a * aw:b * aw]
    qvg, zz, vb = _proj_multi(x_b, jnp.concatenate([cols(0, 1), cols(3, 5), cols(1, 3), cols(7, 8)], axis=1),
                              (3 * aw, 2 * aw, aw), (BF16, F32, BF16))
    cos_t, sin_t = rope
    scale_row = jnp.concatenate([jnp.full((1, aw), LOG2_E * B_HEAD_DIM ** -0.5, F32), jnp.ones((1, aw), F32)], axis=1)
    qk = _proj_rope(x_b, cols(5, 7), cos_t, sin_t, scale_row, seq)
    oa = _hgrn(qvg.reshape(bsz, seq, -1), zz.reshape(bsz, seq, -1), lb, norm_w, n_heads_a)
    ob = _dilated_attention(qk.reshape(bsz, seq, 2 * aw), vb.reshape(bsz, seq, aw))
    w_out = w_out.astype(BF16)
    x_f, x_b = _out_ln([oa.reshape(bsz * seq, aw), ob.reshape(bsz * seq, aw)], [w_out[:aw], w_out[aw:]], x_f,
                       ln1[0], ln1[1], alpha)
    return _ffn_ln(x_b, x_f, w1[layer].astype(BF16), w3[layer].astype(BF16), w2[layer].astype(BF16), ln2[0], ln2[1],
                   alpha)


def _moe_dispatch(idx, n_tokens, tm):
    e_flat = idx[:, :2].reshape(-1)
    onehot = (e_flat[None, :] == jnp.arange(N_EXPERTS, dtype=jnp.int32)[:, None]).astype(jnp.int32)
    counts = jnp.sum(onehot, axis=1)
    tiles = (counts + tm - 1) // tm
    tile_end = jnp.cumsum(tiles)
    group_start = (tile_end - tiles) * tm
    dest = jnp.sum((jnp.cumsum(onehot, axis=1) - 1 + group_start[:, None]) * onehot, axis=0)
    n_tiles = (2 * n_tokens) // tm + N_EXPERTS
    src_tok = jnp.zeros((n_tiles * tm,), jnp.int32).at[dest].set(jnp.arange(2 * n_tokens, dtype=jnp.int32) // 2,
                                                                 unique_indices=True, mode="promise_in_bounds")
    tile_ids = jnp.arange(n_tiles, dtype=jnp.int32)
    tile_expert = jnp.minimum(jnp.sum((tile_ids[:, None] >= tile_end[None, :]).astype(jnp.int32), axis=1),
                              N_EXPERTS - 1)
    return src_tok, dest.reshape(n_tokens, 2), tile_expert, tile_end[-1:].astype(jnp.int32)


def _odd_layer(x_f, x_b, bsz, seq, w_in, lam_params, subln_w, w_out, ln1, router, w1, w3, w2, layer, ln2, rope,
               alpha, lambda_init):
    d = x_f.shape[1]
    n_tok = bsz * seq
    w_in = w_in.astype(BF16)
    cos_t, sin_t = rope
    scale_row = jnp.concatenate([jnp.full((1, d), LOG2_E * C_HEAD_DIM ** -0.5, F32), jnp.ones((1, d), F32)], axis=1)
    qk = _proj_rope(x_b, w_in[:, :2 * d], cos_t, sin_t, scale_row, seq)
    v = _proj(x_b, w_in[:, 2 * d:], BF16)
    o = _diff_attention(qk.reshape(bsz, seq, 2 * d), v.reshape(bsz, seq, d), lam_params.astype(F32), subln_w,
                        lambda_init)
    router_padded = jnp.pad(router.astype(F32), ((0, 0), (0, LANES - N_EXPERTS)))
    x_f, x_b, gates, idx = _out_ln([o.reshape(n_tok, d)], [w_out.astype(BF16)], x_f, ln1[0], ln1[1], alpha,
                                   router_padded)
    tm = _tile(n_tok, (1024, 512, 256))
    src_tok, pos, tile_expert, n_active = _moe_dispatch(idx, n_tok, tm)
    n_tiles = src_tok.shape[0] // tm
    w1, w3, w2 = (_layer_weights_bf16(w, layer) for w in (w1, w3, w2))
    y = None
    for lo, hi in ((0, n_tiles // 2), (n_tiles // 2, n_tiles)):
        x_rows = jnp.take(x_b, src_tok[lo * tm:hi * tm], axis=0, mode="clip")
        n_act = jnp.clip(n_active - lo, 0, hi - lo)
        y = _moe_ffn(x_rows, tile_expert[lo:hi], n_act, w1, w3, w2, tm, lo, n_tiles * tm, y)
    y_pair = jnp.take(y, pos.reshape(-1), axis=0, mode="clip").reshape(n_tok, 2 * d)
    return _moe_combine(x_f, y_pair, gates, ln2[0], ln2[1], alpha)


def kernel(x, ev_w_in, ev_lb_logits, ev_hgrn_norm, ev_w_out, ev_ln1_g, ev_ln1_b, ev_w1, ev_w3, ev_w2, ev_ln2_g,
           ev_ln2_b, od_w_in, od_lambda, od_subln, od_w_out, od_ln1_g, od_ln1_b, od_router, od_w1, od_w3, od_w2,
           od_ln2_g, od_ln2_b):
    bsz, seq, d = x.shape
    depth = ev_w_in.shape[0] + od_w_in.shape[0]
    alpha = (2 * depth) ** 0.25
    rope = _rope_tables(seq, LANES)
    lb_soft = jax.nn.softmax(ev_lb_logits.astype(F32), axis=0)
    lower_bounds = jnp.cumsum(lb_soft, axis=0) - lb_soft[0]
    x_f = x.reshape(bsz * seq, d).astype(F32)
    x_b = x_f.astype(BF16)
    for layer in range(depth):
        j = layer // 2
        if layer % 2 == 0:
            x_f, x_b = _even_layer(x_f, x_b, bsz, seq, ev_w_in[j], lower_bounds[j], ev_hgrn_norm[j], ev_w_out[j],
                                   (ev_ln1_g[j], ev_ln1_b[j]), ev_w1, ev_w3, ev_w2, j,
                                   (ev_ln2_g[j], ev_ln2_b[j]), rope, alpha)
        else:
            lambda_init = 0.8 - 0.6 * math.exp(-0.3 * layer)
            x_f, x_b = _odd_layer(x_f, x_b, bsz, seq, od_w_in[j], od_lambda[j], od_subln[j], od_w_out[j],
                                  (od_ln1_g[j], od_ln1_b[j]), od_router[j], od_w1, od_w3, od_w2, j,
                                  (od_ln2_g[j], od_ln2_b[j]), rope, alpha, lambda_init)
    return x_f.reshape(bsz, seq, d).astype(x.dtype)
```

```python
import functools
import math

import numpy as np
import jax
import jax.numpy as jnp
from jax import lax
from jax.experimental import pallas as pl
from jax.experimental.pallas import tpu as pltpu

F32 = jnp.float32
BF16 = jnp.bfloat16

A_HEAD_DIM = 128
B_HEAD_DIM = 64
B_CONFIGS = ((128, 1), (512, 4), (2048, 16))
BAND_RADIUS = 64
C_HEAD_DIM = 64
N_EXPERTS = 8
ROPE_THETA = 10000.0
LN_EPS = 1e-5
RMS_EPS = 1e-5
MASK_VALUE = -1e30
MIN_FORGET = 1e-30
LOG2_E = math.log2(math.e)

LANES = 128
SUBLANES = 8
VMEM_LIMIT_BYTES = 56 * 1024 * 1024

ROW_CHAINS = 4
FFN_ROW_CHAINS = 2

HGRN_CHUNK = 128
HGRN_LEVELS = 7
HGRN_HEADS_PER_STEP = 2


def _cparams(*sem):
    return pltpu.CompilerParams(dimension_semantics=sem, vmem_limit_bytes=VMEM_LIMIT_BYTES)


def _tile(n, prefs):
    for p in prefs:
        if n % p == 0:
            return p
    return n


def _dot(a, b):
    return jnp.dot(a, b, preferred_element_type=F32)


def _dot_nt(a, b):
    return lax.dot_general(a, b, (((1,), (1,)), ((), ())), preferred_element_type=F32)


def _dot_tn(a, b):
    return lax.dot_general(a, b, (((0,), (0,)), ((), ())), preferred_element_type=F32)


def _sigmoid(x):
    return 1.0 / (1.0 + jnp.exp(-x))


def _layer_norm(y, g, b):
    mu = jnp.mean(y, axis=-1, keepdims=True)
    d = y - mu
    var = jnp.mean(d * d, axis=-1, keepdims=True)
    return d * lax.rsqrt(var + LN_EPS) * g + b


def _proj_kernel(x_ref, w_ref, o_ref):
    o_ref[...] = _dot(x_ref[...], w_ref[...]).astype(o_ref.dtype)


def _proj_multi_kernel(x_ref, w_ref, *o_refs):
    x = x_ref[...]
    col = 0
    for o_ref in o_refs:
        n = o_ref.shape[1]
        o_ref[...] = _dot(x, w_ref[:, col:col + n]).astype(o_ref.dtype)
        col += n


def _proj_multi(x, w, widths, dtypes):
    m, k = x.shape
    tm = _tile(m, (1024, 512, 256))
    return pl.pallas_call(
        _proj_multi_kernel,
        grid=(m // tm,),
        in_specs=[pl.BlockSpec((tm, k), lambda i: (i, 0)), pl.BlockSpec(w.shape, lambda i: (0, 0))],
        out_specs=[pl.BlockSpec((tm, n), lambda i: (i, 0)) for n in widths],
        out_shape=[jax.ShapeDtypeStruct((m, n), dt) for n, dt in zip(widths, dtypes)],
        compiler_params=_cparams("parallel"),
        name="proj_multi",
    )(x, w)


def _proj(x, w, out_dtype):
    m, k = x.shape
    n = w.shape[1]
    tm = _tile(m, (1024, 512, 256))
    tn = n if n <= 1536 else _tile(n, (1024, 512, 256, 128))
    return pl.pallas_call(
        _proj_kernel,
        grid=(m // tm, n // tn),
        in_specs=[pl.BlockSpec((tm, k), lambda i, j: (i, 0)), pl.BlockSpec((k, tn), lambda i, j: (0, j))],
        out_specs=pl.BlockSpec((tm, tn), lambda i, j: (i, j)),
        out_shape=jax.ShapeDtypeStruct((m, n), out_dtype),
        compiler_params=_cparams("parallel", "arbitrary"),
        name="proj",
    )(x, w)


def _proj_rope_kernel(x_ref, w_ref, cos_ref, sin_ref, scale_ref, o_ref):
    tm, tn = o_ref.shape
    sub = tm // ROW_CHAINS
    w = w_ref[...]
    lane = lax.broadcasted_iota(jnp.int32, (sub, tn), 1)
    first_half = (lane % B_HEAD_DIM) < (B_HEAD_DIM // 2)
    reps = tn // cos_ref.shape[1]
    for k in range(ROW_CHAINS):
        rows = pl.ds(k * sub, sub)
        acc = _dot(x_ref[rows, :], w)
        partner = jnp.where(first_half, pltpu.roll(acc, tn - B_HEAD_DIM // 2, 1), pltpu.roll(acc, B_HEAD_DIM // 2, 1))
        cos = jnp.tile(cos_ref[rows, :], (1, reps))
        sin = jnp.tile(sin_ref[rows, :], (1, reps))
        o_ref[rows, :] = ((acc * cos + partner * sin) * scale_ref[...]).astype(o_ref.dtype)


def _proj_rope(x, w, cos_t, sin_t, scale_row, seq):
    m, k = x.shape
    n = w.shape[1]
    tm = _tile(seq, (1024, 512, 256))
    tn = _tile(n, (1024, 512, 256, 128))
    nsb = seq // tm
    return pl.pallas_call(
        _proj_rope_kernel,
        grid=(m // tm, n // tn),
        in_specs=[
            pl.BlockSpec((tm, k), lambda i, j: (i, 0)),
            pl.BlockSpec((k, tn), lambda i, j: (0, j)),
            pl.BlockSpec((tm, LANES), lambda i, j: (i % nsb, 0)),
            pl.BlockSpec((tm, LANES), lambda i, j: (i % nsb, 0)),
            pl.BlockSpec((1, tn), lambda i, j: (0, j)),
        ],
        out_specs=pl.BlockSpec((tm, tn), lambda i, j: (i, j)),
        out_shape=jax.ShapeDtypeStruct((m, n), BF16),
        compiler_params=_cparams("parallel", "arbitrary"),
        name="proj_rope",
    )(x, w, cos_t, sin_t, scale_row)


def _top2_gates(logits):
    lane = lax.broadcasted_iota(jnp.int32, logits.shape, 1)
    neg = jnp.float32(-jnp.inf)
    logits = jnp.where(lane < N_EXPERTS, logits, neg)
    v1 = jnp.max(logits, axis=-1, keepdims=True)
    i1 = jnp.min(jnp.where(logits == v1, lane, LANES), axis=-1, keepdims=True)
    rest = jnp.where(lane == i1, neg, logits)
    v2 = jnp.max(rest, axis=-1, keepdims=True)
    i2 = jnp.min(jnp.where(rest == v2, lane, LANES), axis=-1, keepdims=True)
    e = jnp.exp(v2 - v1)
    g1 = 1.0 / (1.0 + e)
    g2 = e / (1.0 + e)
    return (jnp.where(lane == 0, g1, jnp.where(lane == 1, g2, 0.0)),
            jnp.where(lane == 0, i1, jnp.where(lane == 1, i2, 0)))


def _out_ln_kernel(*refs, n_in, alpha, with_router):
    xs = refs[:n_in]
    ws = refs[n_in:2 * n_in]
    if with_router:
        resid_ref, g_ref, b_ref, r_ref, of_ref, ob_ref, gate_ref, idx_ref = refs[2 * n_in:]
    else:
        resid_ref, g_ref, b_ref, of_ref, ob_ref = refs[2 * n_in:]
    sub = of_ref.shape[0] // ROW_CHAINS
    for k in range(ROW_CHAINS):
        rows = pl.ds(k * sub, sub)
        acc = _dot(xs[0][rows, :], ws[0][...])
        for x_ref, w_ref in zip(xs[1:], ws[1:]):
            acc = acc + _dot(x_ref[rows, :], w_ref[...])
        z = _layer_norm(alpha * resid_ref[rows, :] + acc, g_ref[...], b_ref[...])
        of_ref[rows, :] = z
        ob_ref[rows, :] = z.astype(BF16)
        if with_router:
            logits = jnp.dot(z, r_ref[...], precision=lax.Precision.HIGHEST, preferred_element_type=F32)
            gate_ref[rows, :], idx_ref[rows, :] = _top2_gates(logits)


def _out_ln(xs, ws, resid, g, b, alpha, router_padded=None):
    m, d = resid.shape
    tm = _tile(m, (1024, 512, 256))
    n_in = len(xs)
    row = lambda i: (i, 0)
    fixed = lambda i: (0, 0)
    in_specs = [pl.BlockSpec((tm, x.shape[1]), row) for x in xs]
    in_specs += [pl.BlockSpec(w.shape, fixed) for w in ws]
    in_specs += [pl.BlockSpec((tm, d), row), pl.BlockSpec((1, d), fixed), pl.BlockSpec((1, d), fixed)]
    operands = [*xs, *ws, resid, g.reshape(1, d), b.reshape(1, d)]
    out_specs = [pl.BlockSpec((tm, d), row), pl.BlockSpec((tm, d), row)]
    out_shape = [jax.ShapeDtypeStruct((m, d), F32), jax.ShapeDtypeStruct((m, d), BF16)]
    if router_padded is not None:
        in_specs.append(pl.BlockSpec((d, LANES), fixed))
        operands.append(router_padded)
        out_specs += [pl.BlockSpec((tm, LANES), row), pl.BlockSpec((tm, LANES), row)]
        out_shape += [jax.ShapeDtypeStruct((m, LANES), F32), jax.ShapeDtypeStruct((m, LANES), jnp.int32)]
    return pl.pallas_call(
        functools.partial(_out_ln_kernel, n_in=n_in, alpha=alpha, with_router=router_padded is not None),
        grid=(m // tm,),
        in_specs=in_specs,
        out_specs=out_specs,
        out_shape=out_shape,
        compiler_params=_cparams("parallel"),
        name="out_ln",
    )(*operands)


def _swiglu_accumulate(x_ref, w1, w3, w2, acc_ref):
    sub = x_ref.shape[0] // FFN_ROW_CHAINS
    for k in range(FFN_ROW_CHAINS):
        rows = pl.ds(k * sub, sub)
        h1 = _dot(x_ref[rows, :], w1)
        h3 = _dot(x_ref[rows, :], w3)
        h = (h1 * _sigmoid(h1)) * h3
        acc_ref[rows, :] += _dot(h.astype(BF16), w2)


def _ffn_ln_kernel(x_ref, w1_ref, w3_ref, w2_ref, resid_ref, g_ref, b_ref, of_ref, ob_ref, acc_ref, *, alpha):
    j = pl.program_id(1)

    @pl.when(j == 0)
    def _():
        acc_ref[...] = jnp.zeros_like(acc_ref)

    _swiglu_accumulate(x_ref, w1_ref[...], w3_ref[...], w2_ref[...], acc_ref)

    @pl.when(j == pl.num_programs(1) - 1)
    def _():
        z = _layer_norm(alpha * resid_ref[...] + acc_ref[...], g_ref[...], b_ref[...])
        of_ref[...] = z
        ob_ref[...] = z.astype(BF16)


def _ffn_ln(x_bf, resid, w1, w3, w2, g, b, alpha):
    m, d = resid.shape
    ff = w1.shape[1]
    tm = _tile(m, (1024, 512, 256))
    tf = _tile(ff, (1408, 256, 128))
    return pl.pallas_call(
        functools.partial(_ffn_ln_kernel, alpha=alpha),
        grid=(m // tm, ff // tf),
        in_specs=[
            pl.BlockSpec((tm, d), lambda i, j: (i, 0)),
            pl.BlockSpec((d, tf), lambda i, j: (0, j)),
            pl.BlockSpec((d, tf), lambda i, j: (0, j)),
            pl.BlockSpec((tf, d), lambda i, j: (j, 0)),
            pl.BlockSpec((tm, d), lambda i, j: (i, 0)),
            pl.BlockSpec((1, d), lambda i, j: (0, 0)),
            pl.BlockSpec((1, d), lambda i, j: (0, 0)),
        ],
        out_specs=[pl.BlockSpec((tm, d), lambda i, j: (i, 0)), pl.BlockSpec((tm, d), lambda i, j: (i, 0))],
        out_shape=[jax.ShapeDtypeStruct((m, d), F32), jax.ShapeDtypeStruct((m, d), BF16)],
        scratch_shapes=[pltpu.VMEM((tm, d), F32)],
        compiler_params=_cparams("parallel", "arbitrary"),
        name="ffn_ln",
    )(x_bf, w1, w3, w2, resid, g.reshape(1, d), b.reshape(1, d))


def _cast_kernel(w_ref, o_ref):
    o_ref[...] = w_ref[...].astype(o_ref.dtype)


def _layer_weights_bf16(w, layer):
    cols = w.shape[-1]
    rows = math.prod(w.shape[1:-1])
    flat = w.reshape(w.shape[0] * rows, cols)
    tr = _tile(rows, (512, 256, 128))
    steps = rows // tr
    out = pl.pallas_call(
        _cast_kernel,
        grid=(steps,),
        in_specs=[pl.BlockSpec((tr, cols), lambda i: (layer * steps + i, 0))],
        out_specs=pl.BlockSpec((tr, cols), lambda i: (i, 0)),
        out_shape=jax.ShapeDtypeStruct((rows, cols), BF16),
        compiler_params=_cparams("parallel"),
        name="cast_weights",
    )(flat)
    return out.reshape(w.shape[1:])


def _moe_ffn_kernel(te_ref, na_ref, x_ref, w1_ref, w3_ref, w2_ref, *rest):
    o_ref, acc_ref = rest[-2:]
    i = pl.program_id(0)
    j = pl.program_id(1)
    active = i < na_ref[0]

    @pl.when(active & (j == 0))
    def _():
        acc_ref[...] = jnp.zeros_like(acc_ref)

    @pl.when(active)
    def _():
        _swiglu_accumulate(x_ref, w1_ref[0], w3_ref[0], w2_ref[0], acc_ref)

    @pl.when(active & (j == pl.num_programs(1) - 1))
    def _():
        o_ref[...] = acc_ref[...].astype(o_ref.dtype)


def _moe_ffn(x_sorted, tile_expert, n_active, w1, w3, w2, tm, tile_offset, total_rows, y_prev=None):
    p, d = x_sorted.shape
    ff = w1.shape[2]
    tf = _tile(ff, (1408, 256, 128))
    nf = ff // tf

    def row(i, na):
        return jnp.maximum(jnp.minimum(i, na[0] - 1), 0)

    def col(i, j, na):
        return jnp.where(i < na[0], j, nf - 1)

    operands = [tile_expert, n_active, x_sorted, w1, w3, w2]
    in_specs = [
        pl.BlockSpec((tm, d), lambda i, j, te, na: (row(i, na), 0)),
        pl.BlockSpec((1, d, tf), lambda i, j, te, na: (te[row(i, na)], 0, col(i, j, na))),
        pl.BlockSpec((1, d, tf), lambda i, j, te, na: (te[row(i, na)], 0, col(i, j, na))),
        pl.BlockSpec((1, tf, d), lambda i, j, te, na: (te[row(i, na)], col(i, j, na), 0)),
    ]
    aliases = {}
    if y_prev is not None:
        in_specs.append(pl.BlockSpec(memory_space=pl.ANY))
        aliases = {len(operands): 0}
        operands.append(y_prev)
    grid_spec = pltpu.PrefetchScalarGridSpec(
        num_scalar_prefetch=2,
        grid=(p // tm, nf),
        in_specs=in_specs,
        out_specs=pl.BlockSpec((tm, d), lambda i, j, te, na: (tile_offset + row(i, na), 0)),
        scratch_shapes=[pltpu.VMEM((tm, d), F32)],
    )
    return pl.pallas_call(
        _moe_ffn_kernel,
        grid_spec=grid_spec,
        out_shape=jax.ShapeDtypeStruct((total_rows, d), BF16),
        input_output_aliases=aliases,
        compiler_params=_cparams("arbitrary", "arbitrary"),
        name="moe_ffn",
    )(*operands)


def _moe_combine_kernel(resid_ref, y1_ref, y2_ref, gate_ref, g_ref, b_ref, of_ref, ob_ref, *, alpha):
    gates = gate_ref[...]
    y = gates[:, 0:1] * y1_ref[...].astype(F32) + gates[:, 1:2] * y2_ref[...].astype(F32)
    z = _layer_norm(alpha * resid_ref[...] + y, g_ref[...], b_ref[...])
    of_ref[...] = z
    ob_ref[...] = z.astype(BF16)


def _moe_combine(resid, y_both, gates, g, b, alpha):
    m, d = resid.shape
    tm = _tile(m, (512, 256))
    n_blocks = m // tm
    row = lambda i: (i, 0)
    fixed = lambda i: (0, 0)
    return pl.pallas_call(
        functools.partial(_moe_combine_kernel, alpha=alpha),
        grid=(n_blocks,),
        in_specs=[pl.BlockSpec((tm, d), row), pl.BlockSpec((tm, d), row),
                  pl.BlockSpec((tm, d), lambda i: (n_blocks + i, 0)),
                  pl.BlockSpec((tm, LANES), row), pl.BlockSpec((1, d), fixed), pl.BlockSpec((1, d), fixed)],
        out_specs=[pl.BlockSpec((tm, d), row), pl.BlockSpec((tm, d), row)],
        out_shape=[jax.ShapeDtypeStruct((m, d), F32), jax.ShapeDtypeStruct((m, d), BF16)],
        compiler_params=_cparams("parallel"),
        name="moe_combine",
    )(resid, y_both, y_both, gates, g.reshape(1, d), b.reshape(1, d))


def _hgrn_masks(c, levels):
    t = np.arange(c)[:, None]
    s = np.arange(c)[None, :]
    tri = np.stack([(s <= t), (s >= t)]).astype(np.float32)
    fwd, bwd = [], []
    for l in range(levels):
        same = (t >> (l + 1)) == (s >> (l + 1))
        t_up = ((t >> l) & 1) == 1
        s_up = ((s >> l) & 1) == 1
        fwd.append(same & t_up & ~s_up)
        bwd.append(same & ~t_up & s_up)
    fwd.append(t == s)
    bwd.append(t == s)
    up = np.stack([np.broadcast_to(((t >> l) & 1) == 1, (c, A_HEAD_DIM)) for l in range(levels)])
    sign = np.stack([np.where(up, 1.0, -1.0), np.where(up, -1.0, 1.0)]).astype(np.float32)
    return tri, np.stack([np.stack(fwd), np.stack(bwd)]).astype(np.float32), sign


def _segment_reference(x, level, forward):
    c, w = x.shape
    half = 1 << level
    seg = 2 * half
    idx = half - 1 if forward else half
    if seg >= SUBLANES:
        xr = x.reshape(c // seg, seg, w)
        return jnp.broadcast_to(xr[:, idx:idx + 1, :], xr.shape).reshape(c, w)
    x3 = x.reshape(c // SUBLANES, SUBLANES, w)
    sub = lax.broadcasted_iota(jnp.int32, x3.shape, 1)
    r3 = jnp.broadcast_to(x3[:, idx:idx + 1, :], x3.shape)
    for j in range(1, SUBLANES // seg):
        row = j * seg + idx
        r3 = jnp.where(sub >= j * seg, jnp.broadcast_to(x3[:, row:row + 1, :], x3.shape), r3)
    return r3.reshape(c, w)


def _hgrn_chunk(q, k, v, log_f, state_t, tri, masks, signs, forward):
    c = q.shape[0]
    x = jnp.dot(tri, log_f, precision=lax.Precision.HIGHEST, preferred_element_type=F32)
    scores = masks[HGRN_LEVELS] * _dot_nt(q.astype(BF16), k.astype(BF16))
    for level in range(HGRN_LEVELS):
        ref = _segment_reference(x, level, forward)
        sign = signs[level]
        decay = jnp.exp(sign * (x - ref))
        z = (jnp.where(sign > 0, q, k) * decay).astype(BF16)
        scores = scores + masks[level] * _dot_nt(z, z)
    x_end = x[c - 1:c, :] if forward else x[0:1, :]
    q_dec = (q * jnp.exp(x)).astype(BF16)
    o = _dot(scores.astype(BF16), v.astype(BF16)) + _dot_nt(q_dec, state_t.astype(BF16))
    k_dec = (k * jnp.exp(x_end - x)).astype(BF16)
    new_state_t = state_t * jnp.exp(x_end) + _dot_tn(v.astype(BF16), k_dec)
    return o, new_state_t


def _hgrn_kernel(q_ref, v_ref, g_ref, zf_ref, zb_ref, lb_ref, nw_ref, tri_ref, msk_ref, sgn_ref, o_ref, acc_ref, *,
                 n_chunks):
    c = HGRN_CHUNK
    hd = A_HEAD_DIM

    def gates(z, lb):
        e = jnp.exp(-jnp.abs(z))
        inv = 1.0 / (1.0 + e)
        pos = z >= 0
        sig = jnp.where(pos, inv, e * inv)
        sig_neg = jnp.where(pos, e * inv, inv)
        f = lb + (1.0 - lb) * sig
        return jnp.log(jnp.maximum(f, MIN_FORGET)), (1.0 - lb) * sig_neg

    def load(ref, c0, h):
        return ref[0, pl.ds(c0, c), h * hd:(h + 1) * hd].astype(F32)

    def emit(c0, h, tot):
        ms = jnp.mean(tot * tot, axis=-1, keepdims=True)
        g = load(g_ref, c0, h)
        out = tot * lax.rsqrt(ms + RMS_EPS) * nw_ref[...] * (g * _sigmoid(g))
        o_ref[0, pl.ds(c0, c), h * hd:(h + 1) * hd] = out.astype(o_ref.dtype)

    def one_direction(h, c0, z_ref, state_t, direction):
        lb = lb_ref[:, h * hd:(h + 1) * hd]
        log_f, k = gates(load(z_ref, c0, h), lb)
        masks = [msk_ref[direction, l] for l in range(HGRN_LEVELS + 1)]
        signs = [sgn_ref[direction, l] for l in range(HGRN_LEVELS)]
        return _hgrn_chunk(load(q_ref, c0, h), k, load(v_ref, c0, h), log_f, state_t, tri_ref[direction], masks,
                           signs, direction == 0)

    def step(i, states, finalize):
        cf = pl.multiple_of(i * c, c)
        cb = pl.multiple_of((n_chunks - 1 - i) * c, c)
        new_states = []
        for h in range(HGRN_HEADS_PER_STEP):
            o_f, st_f = one_direction(h, cf, zf_ref, states[2 * h], 0)
            o_b, st_b = one_direction(h, cb, zb_ref, states[2 * h + 1], 1)
            cols = slice(h * hd, (h + 1) * hd)
            if finalize:
                emit(cf, h, acc_ref[pl.ds(cf, c), cols] + o_f)
                emit(cb, h, acc_ref[pl.ds(cb, c), cols] + o_b)
            else:
                acc_ref[pl.ds(cf, c), cols] = o_f
                acc_ref[pl.ds(cb, c), cols] = o_b
            new_states += [st_f, st_b]
        return tuple(new_states)

    zero = jnp.zeros((hd, hd), F32)
    half = n_chunks // 2
    states = lax.fori_loop(0, half, functools.partial(step, finalize=False), (zero,) * (2 * HGRN_HEADS_PER_STEP))
    lax.fori_loop(half, n_chunks, functools.partial(step, finalize=True), states)


def _hgrn(qvg, zz, lb, norm_w, n_heads):
    bsz, s, _ = qvg.shape
    c = HGRN_CHUNK
    assert s % (2 * c) == 0 and n_heads % HGRN_HEADS_PER_STEP == 0
    tri, masks, signs = _hgrn_masks(c, HGRN_LEVELS)
    hd = A_HEAD_DIM
    wide = HGRN_HEADS_PER_STEP * hd
    n_groups = n_heads // HGRN_HEADS_PER_STEP
    col = lambda off: pl.BlockSpec((1, s, wide), lambda b, h: (b, 0, off + h))
    const = lambda a: pl.BlockSpec(a.shape, lambda b, h: (0,) * a.ndim)
    return pl.pallas_call(
        functools.partial(_hgrn_kernel, n_chunks=s // c),
        grid=(bsz, n_groups),
        in_specs=[
            col(0), col(n_groups), col(2 * n_groups), col(0), col(n_groups),
            pl.BlockSpec((1, wide), lambda b, h: (0, h)),
            pl.BlockSpec((1, hd), lambda b, h: (0, 0)),
            const(tri), const(masks), const(signs),
        ],
        out_specs=pl.BlockSpec((1, s, wide), lambda b, h: (b, 0, h)),
        out_shape=jax.ShapeDtypeStruct((bsz, s, n_heads * hd), BF16),
        scratch_shapes=[pltpu.VMEM((s, wide), F32)],
        compiler_params=_cparams("parallel", "parallel"),
        name="hgrn2",
    )(qvg, qvg, qvg, zz, zz, lb.reshape(1, -1), norm_w.reshape(1, hd), jnp.asarray(tri), jnp.asarray(masks),
      jnp.asarray(signs))


def _dilated_kernel(q_ref, k_ref, v_ref, o_ref, qf_ref, kf_ref, vf_ref, oc_ref, lc_ref, *, seq):
    qf_ref[...] = q_ref[0].astype(F32)
    kf_ref[...] = k_ref[0].astype(F32)
    vf_ref[...] = v_ref[0].astype(F32)
    n_cfg = len(B_CONFIGS)
    for ci, (_, dil) in enumerate(B_CONFIGS):
        length = seq // dil
        tq = length if length <= LANES + 2 * BAND_RADIUS else LANES
        win = min(length, tq + 2 * BAND_RADIUS)
        head0 = lax.broadcasted_iota(jnp.int32, (tq, LANES), 1) < B_HEAD_DIM
        rel = lax.broadcasted_iota(jnp.int32, (tq, win), 1) - lax.broadcasted_iota(jnp.int32, (tq, win), 0)

        def rows(first, size, dil=dil):
            return pl.ds(first, size) if dil == 1 else pl.ds(first, size, stride=dil)

        def block(t, carry, ci=ci, dil=dil, length=length, tq=tq, win=win, head0=head0, rel=rel, rows=rows):
            res = t % dil
            q0 = (t // dil) * tq
            start = jnp.clip(q0 - BAND_RADIUS, 0, length - win)
            valid = jnp.abs(rel + (start - q0)) <= BAND_RADIUS
            q_rows = rows(q0 * dil + res, tq)
            k_rows = rows(start * dil + res, win)
            q = qf_ref[q_rows, :].astype(BF16)
            kw = kf_ref[k_rows, :].astype(BF16)
            vw = vf_ref[k_rows, :].astype(BF16)

            def one_head(mask):
                s = _dot_nt(jnp.where(mask, q, jnp.zeros_like(q)), kw)
                s = jnp.where(valid, s, MASK_VALUE)
                m = jnp.max(s, axis=-1, keepdims=True)
                p = jnp.exp2(s - m)
                l = jnp.sum(p, axis=-1, keepdims=True)
                return _dot(p.astype(BF16), vw) / l, m + jnp.log(l) * LOG2_E

            oa, la = one_head(head0)
            ob, lb = one_head(jnp.logical_not(head0))
            oc_ref[ci, q_rows, :] = jnp.where(head0, oa, ob)
            lc_ref[ci, q_rows, :] = jnp.where(head0, la, lb)
            return carry

        n_blocks = dil * (length // tq)
        lax.fori_loop(0, n_blocks, block, 0, unroll=max(1, min(8, 8 * LANES // tq, n_blocks)))

    tmix = min(256, seq)

    def mix(i, carry):
        r0 = pl.multiple_of(i * tmix, tmix)
        lses = [lc_ref[c, pl.ds(r0, tmix), :] for c in range(n_cfg)]
        top = lses[0]
        for l in lses[1:]:
            top = jnp.maximum(top, l)
        num = jnp.zeros((tmix, LANES), F32)
        den = jnp.zeros((tmix, LANES), F32)
        for c in range(n_cfg):
            w = jnp.exp2(lses[c] - top)
            num = num + w * oc_ref[c, pl.ds(r0, tmix), :]
            den = den + w
        o_ref[0, pl.ds(r0, tmix), :] = (num / den).astype(o_ref.dtype)
        return carry

    lax.fori_loop(0, seq // tmix, mix, 0)


def _dilated_attention(qk, v):
    bsz, s, w = v.shape
    n_pairs = w // LANES
    n_cfg = len(B_CONFIGS)
    return pl.pallas_call(
        functools.partial(_dilated_kernel, seq=s),
        grid=(bsz, n_pairs),
        in_specs=[
            pl.BlockSpec((1, s, LANES), lambda b, h: (b, 0, h)),
            pl.BlockSpec((1, s, LANES), lambda b, h: (b, 0, n_pairs + h)),
            pl.BlockSpec((1, s, LANES), lambda b, h: (b, 0, h)),
        ],
        out_specs=pl.BlockSpec((1, s, LANES), lambda b, h: (b, 0, h)),
        out_shape=jax.ShapeDtypeStruct((bsz, s, w), BF16),
        scratch_shapes=[pltpu.VMEM((s, LANES), F32)] * 3 + [pltpu.VMEM((n_cfg, s, LANES), F32)] * 2,
        compiler_params=_cparams("parallel", "parallel"),
        name="dilated_attn",
    )(qk, qk, v)


def _diff_kernel(q_ref, k_ref, v_ref, lam_ref, sub_ref, o_ref, *, tk, lambda_init):
    q = q_ref[0]
    tq = q.shape[0]
    s_len = k_ref.shape[1]
    lane = lax.broadcasted_iota(jnp.int32, q.shape, 1)
    zero = jnp.zeros_like(q)
    qs = (jnp.where(lane < C_HEAD_DIM, q, zero), jnp.where(lane >= C_HEAD_DIM, q, zero))
    tiles = [(t * LANES, (t + 1) * LANES) for t in range(tk // LANES)]
    m = [jnp.full((tq, 1), -jnp.inf, F32)] * 2
    l = [jnp.zeros((tq, 1), F32)] * 2
    acc = [jnp.zeros((tq, LANES), F32)] * 2
    for c in range(s_len // tk):
        lo, hi = c * tk, (c + 1) * tk
        for h in range(2):
            s = _dot_nt(qs[h], k_ref[0, lo:hi, :])
            m_tile = s[:, 0:LANES]
            for a, b in tiles[1:]:
                m_tile = jnp.maximum(m_tile, s[:, a:b])
            m_new = jnp.maximum(m[h], jnp.max(m_tile, axis=-1, keepdims=True))
            alpha = jnp.exp2(m[h] - m_new)
            p = jnp.exp2(s - m_new)
            l_tile = p[:, 0:LANES]
            for a, b in tiles[1:]:
                l_tile = l_tile + p[:, a:b]
            l[h] = alpha * l[h] + jnp.sum(l_tile, axis=-1, keepdims=True)
            acc[h] = alpha * acc[h] + _dot(p.astype(BF16), v_ref[0, lo:hi, :])
            m[h] = m_new
    outs = (acc[0] / l[0], acc[1] / l[1])
    lp = lam_ref[...]
    lam = (jnp.exp(jnp.sum(lp[0:1] * lp[1:2], axis=-1, keepdims=True))
           - jnp.exp(jnp.sum(lp[2:3] * lp[3:4], axis=-1, keepdims=True)) + lambda_init)
    o = outs[0] - lam * outs[1]
    ms_o = jnp.mean(o * o, axis=-1, keepdims=True)
    o_ref[0] = (o * lax.rsqrt(ms_o + RMS_EPS) * sub_ref[...] * (1.0 - lambda_init)).astype(o_ref.dtype)


def _diff_attention(qk, v, lam_params, subln_w, lambda_init):
    bsz, s, w = v.shape
    n_heads = w // LANES
    tq = _tile(s, (1024, 512, 256, 128))
    tk = _tile(s, (2048, 1024, 512, 256, 128))
    return pl.pallas_call(
        functools.partial(_diff_kernel, tk=tk, lambda_init=lambda_init),
        grid=(bsz, n_heads, s // tq),
        in_specs=[
            pl.BlockSpec((1, tq, LANES), lambda b, h, i: (b, i, h)),
            pl.BlockSpec((1, s, LANES), lambda b, h, i: (b, 0, n_heads + h)),
            pl.BlockSpec((1, s, LANES), lambda b, h, i: (b, 0, h)),
            pl.BlockSpec(lam_params.shape, lambda b, h, i: (0, 0)),
            pl.BlockSpec((1, LANES), lambda b, h, i: (0, 0)),
        ],
        out_specs=pl.BlockSpec((1, tq, LANES), lambda b, h, i: (b, i, h)),
        out_shape=jax.ShapeDtypeStruct((bsz, s, w), BF16),
        compiler_params=_cparams("parallel", "parallel", "arbitrary"),
        name="diff_attn",
    )(qk, qk, v, lam_params, subln_w.reshape(1, LANES))


def _rope_tables(seq, width):
    half = B_HEAD_DIM // 2
    inv = ROPE_THETA ** (-jnp.arange(0, B_HEAD_DIM, 2, dtype=F32) / B_HEAD_DIM)
    ang = jnp.arange(seq, dtype=F32)[:, None] * inv[None, :]
    cos, sin = jnp.cos(ang), jnp.sin(ang)
    reps = width // B_HEAD_DIM
    assert half * 2 == B_HEAD_DIM
    return jnp.tile(jnp.concatenate([cos, cos], axis=1), (1, reps)), jnp.tile(jnp.concatenate([-sin, sin], axis=1), (1, reps))


def _even_layer(x_f, x_b, bsz, seq, w_in, lb, norm_w, w_out, ln1, w1, w3, w2, layer, ln2, rope, alpha):
    d = x_f.shape[1]
    aw = d // 2
    n_heads_a = aw // A_HEAD_DIM
    w_in = w_in.astype(BF16)
    cols = lambda a, b: w_in[:, a * aw:b * aw]
    qvg, zz, vb = _proj_multi(x_b, jnp.concatenate([cols(0, 1), cols(3, 5), cols(1, 3), cols(7, 8)], axis=1),
                              (3 * aw, 2 * aw, aw), (BF16, F32, BF16))
    cos_t, sin_t = rope
    scale_row = jnp.concatenate([jnp.full((1, aw), LOG2_E * B_HEAD_DIM ** -0.5, F32), jnp.ones((1, aw), F32)], axis=1)
    qk = _proj_rope(x_b, cols(5, 7), cos_t, sin_t, scale_row, seq)
    oa = _hgrn(qvg.reshape(bsz, seq, -1), zz.reshape(bsz, seq, -1), lb, norm_w, n_heads_a)
    ob = _dilated_attention(qk.reshape(bsz, seq, 2 * aw), vb.reshape(bsz, seq, aw))
    w_out = w_out.astype(BF16)
    x_f, x_b = _out_ln([oa.reshape(bsz * seq, aw), ob.reshape(bsz * seq, aw)], [w_out[:aw], w_out[aw:]], x_f,
                       ln1[0], ln1[1], alpha)
    return _ffn_ln(x_b, x_f, w1[layer].astype(BF16), w3[layer].astype(BF16), w2[layer].astype(BF16), ln2[0], ln2[1],
                   alpha)


def _moe_dispatch(idx, n_tokens, tm):
    e_flat = idx[:, :2].reshape(-1)
    onehot = (e_flat[None, :] == jnp.arange(N_EXPERTS, dtype=jnp.int32)[:, None]).astype(jnp.int32)
    counts = jnp.sum(onehot, axis=1)
    tiles = (counts + tm - 1) // tm
    tile_end = jnp.cumsum(tiles)
    group_start = (tile_end - tiles) * tm
    dest = jnp.sum((jnp.cumsum(onehot, axis=1) - 1 + group_start[:, None]) * onehot, axis=0)
    n_tiles = (2 * n_tokens) // tm + N_EXPERTS
    tile_ids = jnp.arange(n_tiles, dtype=jnp.int32)
    tile_expert = jnp.minimum(jnp.sum((tile_ids[:, None] >= tile_end[None, :]).astype(jnp.int32), axis=1),
                              N_EXPERTS - 1)
    n_pairs = 2 * n_tokens
    order = jnp.sort(e_flat * n_pairs + jnp.arange(n_pairs, dtype=jnp.int32)) % n_pairs
    unpadded_start = jnp.cumsum(counts) - counts
    tile_shift = (unpadded_start - group_start)[tile_expert]
    tile_rows_left = (counts + group_start)[tile_expert] - tile_ids * tm
    sorted_pos = (tile_ids * tm + tile_shift)[:, None] + jnp.arange(tm, dtype=jnp.int32)[None, :]
    valid = jnp.arange(tm, dtype=jnp.int32)[None, :] < tile_rows_left[:, None]
    src_tok = jnp.where(valid, jnp.take(order, jnp.clip(sorted_pos, 0, n_pairs - 1).reshape(-1), mode="clip")
                        .reshape(n_tiles, tm) // 2, 0).reshape(-1)
    return src_tok, dest.reshape(n_tokens, 2), tile_expert, tile_end[-1:].astype(jnp.int32)


def _odd_layer(x_f, x_b, bsz, seq, w_in, lam_params, subln_w, w_out, ln1, router, w1, w3, w2, layer, ln2, rope,
               alpha, lambda_init):
    d = x_f.shape[1]
    n_tok = bsz * seq
    w_in = w_in.astype(BF16)
    cos_t, sin_t = rope
    scale_row = jnp.concatenate([jnp.full((1, d), LOG2_E * C_HEAD_DIM ** -0.5, F32), jnp.ones((1, d), F32)], axis=1)
    qk = _proj_rope(x_b, w_in[:, :2 * d], cos_t, sin_t, scale_row, seq)
    v = _proj(x_b, w_in[:, 2 * d:], BF16)
    o = _diff_attention(qk.reshape(bsz, seq, 2 * d), v.reshape(bsz, seq, d), lam_params.astype(F32), subln_w,
                        lambda_init)
    router_padded = jnp.pad(router.astype(F32), ((0, 0), (0, LANES - N_EXPERTS)))
    x_f, x_b, gates, idx = _out_ln([o.reshape(n_tok, d)], [w_out.astype(BF16)], x_f, ln1[0], ln1[1], alpha,
                                   router_padded)
    tm = _tile(n_tok, (1024, 512, 256))
    src_tok, pos, tile_expert, n_active = _moe_dispatch(idx, n_tok, tm)
    n_tiles = src_tok.shape[0] // tm
    w1, w3, w2 = (_layer_weights_bf16(w, layer) for w in (w1, w3, w2))
    y = None
    for lo, hi in ((0, n_tiles // 2), (n_tiles // 2, n_tiles)):
        x_rows = jnp.take(x_b, src_tok[lo * tm:hi * tm], axis=0, mode="clip")
        n_act = jnp.clip(n_active - lo, 0, hi - lo)
        y = _moe_ffn(x_rows, tile_expert[lo:hi], n_act, w1, w3, w2, tm, lo, n_tiles * tm, y)
    y_both = jnp.take(y, pos.T.reshape(-1), axis=0, mode="clip")
    return _moe_combine(x_f, y_both, gates, ln2[0], ln2[1], alpha)


def kernel(x, ev_w_in, ev_lb_logits, ev_hgrn_norm, ev_w_out, ev_ln1_g, ev_ln1_b, ev_w1, ev_w3, ev_w2, ev_ln2_g,
           ev_ln2_b, od_w_in, od_lambda, od_subln, od_w_out, od_ln1_g, od_ln1_b, od_router, od_w1, od_w3, od_w2,
           od_ln2_g, od_ln2_b):
    bsz, seq, d = x.shape
    depth = ev_w_in.shape[0] + od_w_in.shape[0]
    alpha = (2 * depth) ** 0.25
    rope = _rope_tables(seq, LANES)
    lb_soft = jax.nn.softmax(ev_lb_logits.astype(F32), axis=0)
    lower_bounds = jnp.cumsum(lb_soft, axis=0) - lb_soft[0]
    x_f = x.reshape(bsz * seq, d).astype(F32)
    x_b = x_f.astype(BF16)
    for layer in range(depth):
        j = layer // 2
        if layer % 2 == 0:
            x_f, x_b = _even_layer(x_f, x_b, bsz, seq, ev_w_in[j], lower_bounds[j], ev_hgrn_norm[j], ev_w_out[j],
                                   (ev_ln1_g[j], ev_ln1_b[j]), ev_w1, ev_w3, ev_w2, j,
                                   (ev_ln2_g[j], ev_ln2_b[j]), rope, alpha)
        else:
            lambda_init = 0.8 - 0.6 * math.exp(-0.3 * layer)
            x_f, x_b = _odd_layer(x_f, x_b, bsz, seq, od_w_in[j], od_lambda[j], od_subln[j], od_w_out[j],
                                  (od_ln1_g[j], od_ln1_b[j]), od_router[j], od_w1, od_w3, od_w2, j,
                                  (od_ln2_g[j], od_ln2_b[j]), rope, alpha, lambda_init)
    return x_f.reshape(bsz, seq, d).astype(x.dtype)
```

```python
import functools
import math

import numpy as np
import jax
import jax.numpy as jnp
from jax import lax
from jax.experimental import pallas as pl
from jax.experimental.pallas import tpu as pltpu

F32 = jnp.float32
BF16 = jnp.bfloat16

A_HEAD_DIM = 128
B_HEAD_DIM = 64
B_CONFIGS = ((128, 1), (512, 4), (2048, 16))
BAND_RADIUS = 64
C_HEAD_DIM = 64
N_EXPERTS = 8
ROPE_THETA = 10000.0
LN_EPS = 1e-5
RMS_EPS = 1e-5
MASK_VALUE = -1e30
MIN_FORGET = 1e-30
LOG2_E = math.log2(math.e)

LANES = 128
SUBLANES = 8
VMEM_LIMIT_BYTES = 56 * 1024 * 1024

ROW_CHAINS = 4
FFN_ROW_CHAINS = 2

HGRN_CHUNK = 128
HGRN_LEVELS = 7
HGRN_HEADS_PER_STEP = 2


def _cparams(*sem):
    return pltpu.CompilerParams(dimension_semantics=sem, vmem_limit_bytes=VMEM_LIMIT_BYTES)


def _tile(n, prefs):
    for p in prefs:
        if n % p == 0:
            return p
    return n


def _dot(a, b):
    return jnp.dot(a, b, preferred_element_type=F32)


def _dot_nt(a, b):
    return lax.dot_general(a, b, (((1,), (1,)), ((), ())), preferred_element_type=F32)


def _dot_tn(a, b):
    return lax.dot_general(a, b, (((0,), (0,)), ((), ())), preferred_element_type=F32)


def _sigmoid(x):
    return 1.0 / (1.0 + jnp.exp(-x))


def _layer_norm(y, g, b):
    mu = jnp.mean(y, axis=-1, keepdims=True)
    d = y - mu
    var = jnp.mean(d * d, axis=-1, keepdims=True)
    return d * lax.rsqrt(var + LN_EPS) * g + b


def _proj_kernel(x_ref, w_ref, o_ref):
    o_ref[...] = _dot(x_ref[...], w_ref[...]).astype(o_ref.dtype)


def _proj_multi_kernel(x_ref, w_ref, *o_refs):
    x = x_ref[...]
    col = 0
    for o_ref in o_refs:
        n = o_ref.shape[1]
        o_ref[...] = _dot(x, w_ref[:, col:col + n]).astype(o_ref.dtype)
        col += n


def _proj_multi(x, w, widths, dtypes):
    m, k = x.shape
    tm = _tile(m, (1024, 512, 256))
    return pl.pallas_call(
        _proj_multi_kernel,
        grid=(m // tm,),
        in_specs=[pl.BlockSpec((tm, k), lambda i: (i, 0)), pl.BlockSpec(w.shape, lambda i: (0, 0))],
        out_specs=[pl.BlockSpec((tm, n), lambda i: (i, 0)) for n in widths],
        out_shape=[jax.ShapeDtypeStruct((m, n), dt) for n, dt in zip(widths, dtypes)],
        compiler_params=_cparams("parallel"),
        name="proj_multi",
    )(x, w)


def _proj(x, w, out_dtype):
    m, k = x.shape
    n = w.shape[1]
    tm = _tile(m, (1024, 512, 256))
    tn = n if n <= 1536 else _tile(n, (1024, 512, 256, 128))
    return pl.pallas_call(
        _proj_kernel,
        grid=(m // tm, n // tn),
        in_specs=[pl.BlockSpec((tm, k), lambda i, j: (i, 0)), pl.BlockSpec((k, tn), lambda i, j: (0, j))],
        out_specs=pl.BlockSpec((tm, tn), lambda i, j: (i, j)),
        out_shape=jax.ShapeDtypeStruct((m, n), out_dtype),
        compiler_params=_cparams("parallel", "arbitrary"),
        name="proj",
    )(x, w)


def _proj_rope_kernel(x_ref, w_ref, cos_ref, sin_ref, scale_ref, o_ref):
    tm, tn = o_ref.shape
    sub = tm // ROW_CHAINS
    w = w_ref[...]
    lane = lax.broadcasted_iota(jnp.int32, (sub, tn), 1)
    first_half = (lane % B_HEAD_DIM) < (B_HEAD_DIM // 2)
    reps = tn // cos_ref.shape[1]
    for k in range(ROW_CHAINS):
        rows = pl.ds(k * sub, sub)
        acc = _dot(x_ref[rows, :], w)
        partner = jnp.where(first_half, pltpu.roll(acc, tn - B_HEAD_DIM // 2, 1), pltpu.roll(acc, B_HEAD_DIM // 2, 1))
        cos = jnp.tile(cos_ref[rows, :], (1, reps))
        sin = jnp.tile(sin_ref[rows, :], (1, reps))
        o_ref[rows, :] = ((acc * cos + partner * sin) * scale_ref[...]).astype(o_ref.dtype)


def _proj_rope(x, w, cos_t, sin_t, scale_row, seq):
    m, k = x.shape
    n = w.shape[1]
    tm = _tile(seq, (1024, 512, 256))
    tn = _tile(n, (1024, 512, 256, 128))
    nsb = seq // tm
    return pl.pallas_call(
        _proj_rope_kernel,
        grid=(m // tm, n // tn),
        in_specs=[
            pl.BlockSpec((tm, k), lambda i, j: (i, 0)),
            pl.BlockSpec((k, tn), lambda i, j: (0, j)),
            pl.BlockSpec((tm, LANES), lambda i, j: (i % nsb, 0)),
            pl.BlockSpec((tm, LANES), lambda i, j: (i % nsb, 0)),
            pl.BlockSpec((1, tn), lambda i, j: (0, j)),
        ],
        out_specs=pl.BlockSpec((tm, tn), lambda i, j: (i, j)),
        out_shape=jax.ShapeDtypeStruct((m, n), BF16),
        compiler_params=_cparams("parallel", "arbitrary"),
        name="proj_rope",
    )(x, w, cos_t, sin_t, scale_row)


def _top2_gates(logits):
    lane = lax.broadcasted_iota(jnp.int32, logits.shape, 1)
    neg = jnp.float32(-jnp.inf)
    logits = jnp.where(lane < N_EXPERTS, logits, neg)
    v1 = jnp.max(logits, axis=-1, keepdims=True)
    i1 = jnp.min(jnp.where(logits == v1, lane, LANES), axis=-1, keepdims=True)
    rest = jnp.where(lane == i1, neg, logits)
    v2 = jnp.max(rest, axis=-1, keepdims=True)
    i2 = jnp.min(jnp.where(rest == v2, lane, LANES), axis=-1, keepdims=True)
    e = jnp.exp(v2 - v1)
    g1 = 1.0 / (1.0 + e)
    g2 = e / (1.0 + e)
    return (jnp.where(lane == 0, g1, jnp.where(lane == 1, g2, 0.0)),
            jnp.where(lane == 0, i1, jnp.where(lane == 1, i2, 0)))


def _out_ln_kernel(*refs, n_in, alpha, with_router):
    xs = refs[:n_in]
    ws = refs[n_in:2 * n_in]
    if with_router:
        resid_ref, g_ref, b_ref, r_ref, of_ref, ob_ref, gate_ref, idx_ref = refs[2 * n_in:]
    else:
        resid_ref, g_ref, b_ref, of_ref, ob_ref = refs[2 * n_in:]
    sub = of_ref.shape[0] // ROW_CHAINS
    for k in range(ROW_CHAINS):
        rows = pl.ds(k * sub, sub)
        acc = _dot(xs[0][rows, :], ws[0][...])
        for x_ref, w_ref in zip(xs[1:], ws[1:]):
            acc = acc + _dot(x_ref[rows, :], w_ref[...])
        z = _layer_norm(alpha * resid_ref[rows, :] + acc, g_ref[...], b_ref[...])
        of_ref[rows, :] = z
        ob_ref[rows, :] = z.astype(BF16)
    if with_router:
        logits = jnp.dot(of_ref[...], r_ref[...], precision=lax.Precision.HIGHEST, preferred_element_type=F32)
        gate_ref[...], idx_ref[...] = _top2_gates(logits)


def _out_ln(xs, ws, resid, g, b, alpha, router_padded=None):
    m, d = resid.shape
    tm = _tile(m, (1024, 512, 256))
    n_in = len(xs)
    row = lambda i: (i, 0)
    fixed = lambda i: (0, 0)
    in_specs = [pl.BlockSpec((tm, x.shape[1]), row) for x in xs]
    in_specs += [pl.BlockSpec(w.shape, fixed) for w in ws]
    in_specs += [pl.BlockSpec((tm, d), row), pl.BlockSpec((1, d), fixed), pl.BlockSpec((1, d), fixed)]
    operands = [*xs, *ws, resid, g.reshape(1, d), b.reshape(1, d)]
    out_specs = [pl.BlockSpec((tm, d), row), pl.BlockSpec((tm, d), row)]
    out_shape = [jax.ShapeDtypeStruct((m, d), F32), jax.ShapeDtypeStruct((m, d), BF16)]
    if router_padded is not None:
        in_specs.append(pl.BlockSpec((d, LANES), fixed))
        operands.append(router_padded)
        out_specs += [pl.BlockSpec((tm, LANES), row), pl.BlockSpec((tm, LANES), row)]
        out_shape += [jax.ShapeDtypeStruct((m, LANES), F32), jax.ShapeDtypeStruct((m, LANES), jnp.int32)]
    return pl.pallas_call(
        functools.partial(_out_ln_kernel, n_in=n_in, alpha=alpha, with_router=router_padded is not None),
        grid=(m // tm,),
        in_specs=in_specs,
        out_specs=out_specs,
        out_shape=out_shape,
        compiler_params=_cparams("parallel"),
        name="out_ln",
    )(*operands)


def _swiglu_accumulate(x_ref, w1, w3, w2, acc_ref):
    sub = x_ref.shape[0] // FFN_ROW_CHAINS
    for k in range(FFN_ROW_CHAINS):
        rows = pl.ds(k * sub, sub)
        h1 = _dot(x_ref[rows, :], w1)
        h3 = _dot(x_ref[rows, :], w3)
        h = (h1 * _sigmoid(h1)) * h3
        acc_ref[rows, :] += _dot(h.astype(BF16), w2)


def _ffn_ln_kernel(x_ref, w1_ref, w3_ref, w2_ref, resid_ref, g_ref, b_ref, of_ref, ob_ref, acc_ref, *, alpha):
    j = pl.program_id(1)

    @pl.when(j == 0)
    def _():
        acc_ref[...] = jnp.zeros_like(acc_ref)

    _swiglu_accumulate(x_ref, w1_ref[...], w3_ref[...], w2_ref[...], acc_ref)

    @pl.when(j == pl.num_programs(1) - 1)
    def _():
        z = _layer_norm(alpha * resid_ref[...] + acc_ref[...], g_ref[...], b_ref[...])
        of_ref[...] = z
        ob_ref[...] = z.astype(BF16)


def _ffn_ln(x_bf, resid, w1, w3, w2, g, b, alpha):
    m, d = resid.shape
    ff = w1.shape[1]
    tm = _tile(m, (1024, 512, 256))
    tf = _tile(ff, (1408, 256, 128))
    return pl.pallas_call(
        functools.partial(_ffn_ln_kernel, alpha=alpha),
        grid=(m // tm, ff // tf),
        in_specs=[
            pl.BlockSpec((tm, d), lambda i, j: (i, 0)),
            pl.BlockSpec((d, tf), lambda i, j: (0, j)),
            pl.BlockSpec((d, tf), lambda i, j: (0, j)),
            pl.BlockSpec((tf, d), lambda i, j: (j, 0)),
            pl.BlockSpec((tm, d), lambda i, j: (i, 0)),
            pl.BlockSpec((1, d), lambda i, j: (0, 0)),
            pl.BlockSpec((1, d), lambda i, j: (0, 0)),
        ],
        out_specs=[pl.BlockSpec((tm, d), lambda i, j: (i, 0)), pl.BlockSpec((tm, d), lambda i, j: (i, 0))],
        out_shape=[jax.ShapeDtypeStruct((m, d), F32), jax.ShapeDtypeStruct((m, d), BF16)],
        scratch_shapes=[pltpu.VMEM((tm, d), F32)],
        compiler_params=_cparams("parallel", "arbitrary"),
        name="ffn_ln",
    )(x_bf, w1, w3, w2, resid, g.reshape(1, d), b.reshape(1, d))


def _cast_kernel(w_ref, o_ref):
    o_ref[...] = w_ref[...].astype(o_ref.dtype)


def _layer_weights_bf16(w, layer):
    cols = w.shape[-1]
    rows = math.prod(w.shape[1:-1])
    flat = w.reshape(w.shape[0] * rows, cols)
    tr = _tile(rows, (512, 256, 128))
    steps = rows // tr
    out = pl.pallas_call(
        _cast_kernel,
        grid=(steps,),
        in_specs=[pl.BlockSpec((tr, cols), lambda i: (layer * steps + i, 0))],
        out_specs=pl.BlockSpec((tr, cols), lambda i: (i, 0)),
        out_shape=jax.ShapeDtypeStruct((rows, cols), BF16),
        compiler_params=_cparams("parallel"),
        name="cast_weights",
    )(flat)
    return out.reshape(w.shape[1:])


def _moe_ffn_kernel(te_ref, na_ref, x_ref, w1_ref, w3_ref, w2_ref, *rest):
    o_ref, acc_ref = rest[-2:]
    i = pl.program_id(0)
    j = pl.program_id(1)
    active = i < na_ref[0]

    @pl.when(active & (j == 0))
    def _():
        acc_ref[...] = jnp.zeros_like(acc_ref)

    @pl.when(active)
    def _():
        _swiglu_accumulate(x_ref, w1_ref[0], w3_ref[0], w2_ref[0], acc_ref)

    @pl.when(active & (j == pl.num_programs(1) - 1))
    def _():
        o_ref[...] = acc_ref[...].astype(o_ref.dtype)


def _moe_ffn(x_sorted, tile_expert, n_active, w1, w3, w2, tm, tile_offset, total_rows, y_prev=None):
    p, d = x_sorted.shape
    ff = w1.shape[2]
    tf = _tile(ff, (1408, 256, 128))
    nf = ff // tf

    def row(i, na):
        return jnp.maximum(jnp.minimum(i, na[0] - 1), 0)

    def col(i, j, na):
        return jnp.where(i < na[0], j, nf - 1)

    operands = [tile_expert, n_active, x_sorted, w1, w3, w2]
    in_specs = [
        pl.BlockSpec((tm, d), lambda i, j, te, na: (row(i, na), 0)),
        pl.BlockSpec((1, d, tf), lambda i, j, te, na: (te[row(i, na)], 0, col(i, j, na))),
        pl.BlockSpec((1, d, tf), lambda i, j, te, na: (te[row(i, na)], 0, col(i, j, na))),
        pl.BlockSpec((1, tf, d), lambda i, j, te, na: (te[row(i, na)], col(i, j, na), 0)),
    ]
    aliases = {}
    if y_prev is not None:
        in_specs.append(pl.BlockSpec(memory_space=pl.ANY))
        aliases = {len(operands): 0}
        operands.append(y_prev)
    grid_spec = pltpu.PrefetchScalarGridSpec(
        num_scalar_prefetch=2,
        grid=(p // tm, nf),
        in_specs=in_specs,
        out_specs=pl.BlockSpec((tm, d), lambda i, j, te, na: (tile_offset + row(i, na), 0)),
        scratch_shapes=[pltpu.VMEM((tm, d), F32)],
    )
    return pl.pallas_call(
        _moe_ffn_kernel,
        grid_spec=grid_spec,
        out_shape=jax.ShapeDtypeStruct((total_rows, d), BF16),
        input_output_aliases=aliases,
        compiler_params=_cparams("arbitrary", "arbitrary"),
        name="moe_ffn",
    )(*operands)


def _moe_combine_kernel(resid_ref, y1_ref, y2_ref, gate_ref, g_ref, b_ref, of_ref, ob_ref, *, alpha):
    gates = gate_ref[...]
    y = gates[:, 0:1] * y1_ref[...].astype(F32) + gates[:, 1:2] * y2_ref[...].astype(F32)
    z = _layer_norm(alpha * resid_ref[...] + y, g_ref[...], b_ref[...])
    of_ref[...] = z
    ob_ref[...] = z.astype(BF16)


def _moe_combine(resid, y_both, gates, g, b, alpha):
    m, d = resid.shape
    tm = _tile(m, (512, 256))
    n_blocks = m // tm
    row = lambda i: (i, 0)
    fixed = lambda i: (0, 0)
    return pl.pallas_call(
        functools.partial(_moe_combine_kernel, alpha=alpha),
        grid=(n_blocks,),
        in_specs=[pl.BlockSpec((tm, d), row), pl.BlockSpec((tm, d), row),
                  pl.BlockSpec((tm, d), lambda i: (n_blocks + i, 0)),
                  pl.BlockSpec((tm, LANES), row), pl.BlockSpec((1, d), fixed), pl.BlockSpec((1, d), fixed)],
        out_specs=[pl.BlockSpec((tm, d), row), pl.BlockSpec((tm, d), row)],
        out_shape=[jax.ShapeDtypeStruct((m, d), F32), jax.ShapeDtypeStruct((m, d), BF16)],
        compiler_params=_cparams("parallel"),
        name="moe_combine",
    )(resid, y_both, y_both, gates, g.reshape(1, d), b.reshape(1, d))


def _hgrn_masks(c, levels):
    t = np.arange(c)[:, None]
    s = np.arange(c)[None, :]
    tri = np.stack([(s <= t), (s >= t)]).astype(np.float32)
    fwd, bwd = [], []
    for l in range(levels):
        same = (t >> (l + 1)) == (s >> (l + 1))
        t_up = ((t >> l) & 1) == 1
        s_up = ((s >> l) & 1) == 1
        fwd.append(same & t_up & ~s_up)
        bwd.append(same & ~t_up & s_up)
    fwd.append(t == s)
    bwd.append(t == s)
    up = np.stack([np.broadcast_to(((t >> l) & 1) == 1, (c, A_HEAD_DIM)) for l in range(levels)])
    sign = np.stack([np.where(up, 1.0, -1.0), np.where(up, -1.0, 1.0)]).astype(np.float32)
    return tri, np.stack([np.stack(fwd), np.stack(bwd)]).astype(np.float32), sign


def _segment_reference(x, level, forward):
    c, w = x.shape
    half = 1 << level
    seg = 2 * half
    idx = half - 1 if forward else half
    if seg >= SUBLANES:
        xr = x.reshape(c // seg, seg, w)
        return jnp.broadcast_to(xr[:, idx:idx + 1, :], xr.shape).reshape(c, w)
    x3 = x.reshape(c // SUBLANES, SUBLANES, w)
    sub = lax.broadcasted_iota(jnp.int32, x3.shape, 1)
    r3 = jnp.broadcast_to(x3[:, idx:idx + 1, :], x3.shape)
    for j in range(1, SUBLANES // seg):
        row = j * seg + idx
        r3 = jnp.where(sub >= j * seg, jnp.broadcast_to(x3[:, row:row + 1, :], x3.shape), r3)
    return r3.reshape(c, w)


def _hgrn_chunk(q, k, v, log_f, state_t, tri, masks, signs, forward):
    c = q.shape[0]
    x = jnp.dot(tri, log_f, precision=lax.Precision.HIGHEST, preferred_element_type=F32)
    scores = masks[HGRN_LEVELS] * _dot_nt(q.astype(BF16), k.astype(BF16))
    for level in range(HGRN_LEVELS):
        ref = _segment_reference(x, level, forward)
        sign = signs[level]
        decay = jnp.exp(sign * (x - ref))
        z = (jnp.where(sign > 0, q, k) * decay).astype(BF16)
        scores = scores + masks[level] * _dot_nt(z, z)
    x_end = x[c - 1:c, :] if forward else x[0:1, :]
    q_dec = (q * jnp.exp(x)).astype(BF16)
    o = _dot(scores.astype(BF16), v.astype(BF16)) + _dot_nt(q_dec, state_t.astype(BF16))
    k_dec = (k * jnp.exp(x_end - x)).astype(BF16)
    new_state_t = state_t * jnp.exp(x_end) + _dot_tn(v.astype(BF16), k_dec)
    return o, new_state_t


def _hgrn_kernel(q_ref, v_ref, g_ref, zf_ref, zb_ref, lb_ref, nw_ref, tri_ref, msk_ref, sgn_ref, o_ref, acc_ref, *,
                 n_chunks):
    c = HGRN_CHUNK
    hd = A_HEAD_DIM

    def gates(z, lb):
        e = jnp.exp(-jnp.abs(z))
        inv = 1.0 / (1.0 + e)
        pos = z >= 0
        sig = jnp.where(pos, inv, e * inv)
        sig_neg = jnp.where(pos, e * inv, inv)
        f = lb + (1.0 - lb) * sig
        return jnp.log(jnp.maximum(f, MIN_FORGET)), (1.0 - lb) * sig_neg

    def load(ref, c0, h):
        return ref[0, pl.ds(c0, c), h * hd:(h + 1) * hd].astype(F32)

    def emit(c0, h, tot):
        ms = jnp.mean(tot * tot, axis=-1, keepdims=True)
        g = load(g_ref, c0, h)
        out = tot * lax.rsqrt(ms + RMS_EPS) * nw_ref[...] * (g * _sigmoid(g))
        o_ref[0, pl.ds(c0, c), h * hd:(h + 1) * hd] = out.astype(o_ref.dtype)

    def one_direction(h, c0, z_ref, state_t, direction):
        lb = lb_ref[:, h * hd:(h + 1) * hd]
        log_f, k = gates(load(z_ref, c0, h), lb)
        masks = [msk_ref[direction, l] for l in range(HGRN_LEVELS + 1)]
        signs = [sgn_ref[direction, l] for l in range(HGRN_LEVELS)]
        return _hgrn_chunk(load(q_ref, c0, h), k, load(v_ref, c0, h), log_f, state_t, tri_ref[direction], masks,
                           signs, direction == 0)

    def step(i, states, finalize):
        cf = pl.multiple_of(i * c, c)
        cb = pl.multiple_of((n_chunks - 1 - i) * c, c)
        new_states = []
        for h in range(HGRN_HEADS_PER_STEP):
            o_f, st_f = one_direction(h, cf, zf_ref, states[2 * h], 0)
            o_b, st_b = one_direction(h, cb, zb_ref, states[2 * h + 1], 1)
            cols = slice(h * hd, (h + 1) * hd)
            if finalize:
                emit(cf, h, acc_ref[pl.ds(cf, c), cols] + o_f)
                emit(cb, h, acc_ref[pl.ds(cb, c), cols] + o_b)
            else:
                acc_ref[pl.ds(cf, c), cols] = o_f
                acc_ref[pl.ds(cb, c), cols] = o_b
            new_states += [st_f, st_b]
        return tuple(new_states)

    zero = jnp.zeros((hd, hd), F32)
    half = n_chunks // 2
    states = lax.fori_loop(0, half, functools.partial(step, finalize=False), (zero,) * (2 * HGRN_HEADS_PER_STEP))
    lax.fori_loop(half, n_chunks, functools.partial(step, finalize=True), states)


def _hgrn(qvg, zz, lb, norm_w, n_heads):
    bsz, s, _ = qvg.shape
    c = HGRN_CHUNK
    assert s % (2 * c) == 0 and n_heads % HGRN_HEADS_PER_STEP == 0
    tri, masks, signs = _hgrn_masks(c, HGRN_LEVELS)
    hd = A_HEAD_DIM
    wide = HGRN_HEADS_PER_STEP * hd
    n_groups = n_heads // HGRN_HEADS_PER_STEP
    col = lambda off: pl.BlockSpec((1, s, wide), lambda b, h: (b, 0, off + h))
    const = lambda a: pl.BlockSpec(a.shape, lambda b, h: (0,) * a.ndim)
    return pl.pallas_call(
        functools.partial(_hgrn_kernel, n_chunks=s // c),
        grid=(bsz, n_groups),
        in_specs=[
            col(0), col(n_groups), col(2 * n_groups), col(0), col(n_groups),
            pl.BlockSpec((1, wide), lambda b, h: (0, h)),
            pl.BlockSpec((1, hd), lambda b, h: (0, 0)),
            const(tri), const(masks), const(signs),
        ],
        out_specs=pl.BlockSpec((1, s, wide), lambda b, h: (b, 0, h)),
        out_shape=jax.ShapeDtypeStruct((bsz, s, n_heads * hd), BF16),
        scratch_shapes=[pltpu.VMEM((s, wide), F32)],
        compiler_params=_cparams("parallel", "parallel"),
        name="hgrn2",
    )(qvg, qvg, qvg, zz, zz, lb.reshape(1, -1), norm_w.reshape(1, hd), jnp.asarray(tri), jnp.asarray(masks),
      jnp.asarray(signs))


def _dilated_kernel(q_ref, k_ref, v_ref, o_ref, qf_ref, kf_ref, vf_ref, oc_ref, lc_ref, *, seq):
    qf_ref[...] = q_ref[0].astype(F32)
    kf_ref[...] = k_ref[0].astype(F32)
    vf_ref[...] = v_ref[0].astype(F32)
    n_cfg = len(B_CONFIGS)
    for ci, (_, dil) in enumerate(B_CONFIGS):
        length = seq // dil
        tq = length if length <= LANES + 2 * BAND_RADIUS else LANES
        win = min(length, tq + 2 * BAND_RADIUS)
        head0 = lax.broadcasted_iota(jnp.int32, (tq, LANES), 1) < B_HEAD_DIM
        rel = lax.broadcasted_iota(jnp.int32, (tq, win), 1) - lax.broadcasted_iota(jnp.int32, (tq, win), 0)

        def rows(first, size, dil=dil):
            return pl.ds(first, size) if dil == 1 else pl.ds(first, size, stride=dil)

        def block(t, carry, ci=ci, dil=dil, length=length, tq=tq, win=win, head0=head0, rel=rel, rows=rows):
            res = t % dil
            q0 = (t // dil) * tq
            start = jnp.clip(q0 - BAND_RADIUS, 0, length - win)
            valid = jnp.abs(rel + (start - q0)) <= BAND_RADIUS
            q_rows = rows(q0 * dil + res, tq)
            k_rows = rows(start * dil + res, win)
            q = qf_ref[q_rows, :].astype(BF16)
            kw = kf_ref[k_rows, :].astype(BF16)
            vw = vf_ref[k_rows, :].astype(BF16)

            def one_head(mask):
                s = _dot_nt(jnp.where(mask, q, jnp.zeros_like(q)), kw)
                s = jnp.where(valid, s, MASK_VALUE)
                m = jnp.max(s, axis=-1, keepdims=True)
                p = jnp.exp2(s - m)
                l = jnp.sum(p, axis=-1, keepdims=True)
                return _dot(p.astype(BF16), vw) / l, m + jnp.log(l) * LOG2_E

            oa, la = one_head(head0)
            ob, lb = one_head(jnp.logical_not(head0))
            oc_ref[ci, q_rows, :] = jnp.where(head0, oa, ob)
            lc_ref[ci, q_rows, :] = jnp.where(head0, la, lb)
            return carry

        n_blocks = dil * (length // tq)
        lax.fori_loop(0, n_blocks, block, 0, unroll=max(1, min(8, 8 * LANES // tq, n_blocks)))

    tmix = min(256, seq)

    def mix(i, carry):
        r0 = pl.multiple_of(i * tmix, tmix)
        lses = [lc_ref[c, pl.ds(r0, tmix), :] for c in range(n_cfg)]
        top = lses[0]
        for l in lses[1:]:
            top = jnp.maximum(top, l)
        num = jnp.zeros((tmix, LANES), F32)
        den = jnp.zeros((tmix, LANES), F32)
        for c in range(n_cfg):
            w = jnp.exp2(lses[c] - top)
            num = num + w * oc_ref[c, pl.ds(r0, tmix), :]
            den = den + w
        o_ref[0, pl.ds(r0, tmix), :] = (num / den).astype(o_ref.dtype)
        return carry

    lax.fori_loop(0, seq // tmix, mix, 0)


def _dilated_attention(qk, v):
    bsz, s, w = v.shape
    n_pairs = w // LANES
    n_cfg = len(B_CONFIGS)
    return pl.pallas_call(
        functools.partial(_dilated_kernel, seq=s),
        grid=(bsz, n_pairs),
        in_specs=[
            pl.BlockSpec((1, s, LANES), lambda b, h: (b, 0, h)),
            pl.BlockSpec((1, s, LANES), lambda b, h: (b, 0, n_pairs + h)),
            pl.BlockSpec((1, s, LANES), lambda b, h: (b, 0, h)),
        ],
        out_specs=pl.BlockSpec((1, s, LANES), lambda b, h: (b, 0, h)),
        out_shape=jax.ShapeDtypeStruct((bsz, s, w), BF16),
        scratch_shapes=[pltpu.VMEM((s, LANES), F32)] * 3 + [pltpu.VMEM((n_cfg, s, LANES), F32)] * 2,
        compiler_params=_cparams("parallel", "parallel"),
        name="dilated_attn",
    )(qk, qk, v)


def _diff_kernel(q_ref, k_ref, v_ref, lam_ref, sub_ref, o_ref, *, tk, lambda_init):
    q = q_ref[0]
    tq = q.shape[0]
    s_len = k_ref.shape[1]
    lane = lax.broadcasted_iota(jnp.int32, q.shape, 1)
    zero = jnp.zeros_like(q)
    qs = (jnp.where(lane < C_HEAD_DIM, q, zero), jnp.where(lane >= C_HEAD_DIM, q, zero))
    tiles = [(t * LANES, (t + 1) * LANES) for t in range(tk // LANES)]
    m = [jnp.full((tq, 1), -jnp.inf, F32)] * 2
    l = [jnp.zeros((tq, 1), F32)] * 2
    acc = [jnp.zeros((tq, LANES), F32)] * 2
    for c in range(s_len // tk):
        lo, hi = c * tk, (c + 1) * tk
        for h in range(2):
            s = _dot_nt(qs[h], k_ref[0, lo:hi, :])
            m_tile = s[:, 0:LANES]
            for a, b in tiles[1:]:
                m_tile = jnp.maximum(m_tile, s[:, a:b])
            m_new = jnp.maximum(m[h], jnp.max(m_tile, axis=-1, keepdims=True))
            alpha = jnp.exp2(m[h] - m_new)
            p = jnp.exp2(s - m_new)
            l_tile = p[:, 0:LANES]
            for a, b in tiles[1:]:
                l_tile = l_tile + p[:, a:b]
            l[h] = alpha * l[h] + jnp.sum(l_tile, axis=-1, keepdims=True)
            acc[h] = alpha * acc[h] + _dot(p.astype(BF16), v_ref[0, lo:hi, :])
            m[h] = m_new
    outs = (acc[0] / l[0], acc[1] / l[1])
    lp = lam_ref[...]
    lam = (jnp.exp(jnp.sum(lp[0:1] * lp[1:2], axis=-1, keepdims=True))
           - jnp.exp(jnp.sum(lp[2:3] * lp[3:4], axis=-1, keepdims=True)) + lambda_init)
    o = outs[0] - lam * outs[1]
    ms_o = jnp.mean(o * o, axis=-1, keepdims=True)
    o_ref[0] = (o * lax.rsqrt(ms_o + RMS_EPS) * sub_ref[...] * (1.0 - lambda_init)).astype(o_ref.dtype)


def _diff_attention(qk, v, lam_params, subln_w, lambda_init):
    bsz, s, w = v.shape
    n_heads = w // LANES
    tq = _tile(s, (1024, 512, 256, 128))
    tk = _tile(s, (2048, 1024, 512, 256, 128))
    return pl.pallas_call(
        functools.partial(_diff_kernel, tk=tk, lambda_init=lambda_init),
        grid=(bsz, n_heads, s // tq),
        in_specs=[
            pl.BlockSpec((1, tq, LANES), lambda b, h, i: (b, i, h)),
            pl.BlockSpec((1, s, LANES), lambda b, h, i: (b, 0, n_heads + h)),
            pl.BlockSpec((1, s, LANES), lambda b, h, i: (b, 0, h)),
            pl.BlockSpec(lam_params.shape, lambda b, h, i: (0, 0)),
            pl.BlockSpec((1, LANES), lambda b, h, i: (0, 0)),
        ],
        out_specs=pl.BlockSpec((1, tq, LANES), lambda b, h, i: (b, i, h)),
        out_shape=jax.ShapeDtypeStruct((bsz, s, w), BF16),
        compiler_params=_cparams("parallel", "parallel", "arbitrary"),
        name="diff_attn",
    )(qk, qk, v, lam_params, subln_w.reshape(1, LANES))


def _rope_tables(seq, width):
    half = B_HEAD_DIM // 2
    inv = ROPE_THETA ** (-jnp.arange(0, B_HEAD_DIM, 2, dtype=F32) / B_HEAD_DIM)
    ang = jnp.arange(seq, dtype=F32)[:, None] * inv[None, :]
    cos, sin = jnp.cos(ang), jnp.sin(ang)
    reps = width // B_HEAD_DIM
    assert half * 2 == B_HEAD_DIM
    return jnp.tile(jnp.concatenate([cos, cos], axis=1), (1, reps)), jnp.tile(jnp.concatenate([-sin, sin], axis=1), (1, reps))


def _even_layer(x_f, x_b, bsz, seq, w_in, lb, norm_w, w_out, ln1, w1, w3, w2, layer, ln2, rope, alpha):
    d = x_f.shape[1]
    aw = d // 2
    n_heads_a = aw // A_HEAD_DIM
    w_in = w_in.astype(BF16)
    cols = lambda a, b: w_in[:, a * aw:b * aw]
    qvg, zz, vb = _proj_multi(x_b, jnp.concatenate([cols(0, 1), cols(3, 5), cols(1, 3), cols(7, 8)], axis=1),
                              (3 * aw, 2 * aw, aw), (BF16, F32, BF16))
    cos_t, sin_t = rope
    scale_row = jnp.concatenate([jnp.full((1, aw), LOG2_E * B_HEAD_DIM ** -0.5, F32), jnp.ones((1, aw), F32)], axis=1)
    qk = _proj_rope(x_b, cols(5, 7), cos_t, sin_t, scale_row, seq)
    oa = _hgrn(qvg.reshape(bsz, seq, -1), zz.reshape(bsz, seq, -1), lb, norm_w, n_heads_a)
    ob = _dilated_attention(qk.reshape(bsz, seq, 2 * aw), vb.reshape(bsz, seq, aw))
    w_out = w_out.astype(BF16)
    x_f, x_b = _out_ln([oa.reshape(bsz * seq, aw), ob.reshape(bsz * seq, aw)], [w_out[:aw], w_out[aw:]], x_f,
                       ln1[0], ln1[1], alpha)
    return _ffn_ln(x_b, x_f, w1[layer].astype(BF16), w3[layer].astype(BF16), w2[layer].astype(BF16), ln2[0], ln2[1],
                   alpha)


def _moe_dispatch(idx, n_tokens, tm):
    e_flat = idx[:, :2].reshape(-1)
    onehot = (e_flat[None, :] == jnp.arange(N_EXPERTS, dtype=jnp.int32)[:, None]).astype(jnp.int32)
    counts = jnp.sum(onehot, axis=1)
    tiles = (counts + tm - 1) // tm
    tile_end = jnp.cumsum(tiles)
    group_start = (tile_end - tiles) * tm
    dest = jnp.sum((jnp.cumsum(onehot, axis=1) - 1 + group_start[:, None]) * onehot, axis=0)
    n_tiles = (2 * n_tokens) // tm + N_EXPERTS
    tile_ids = jnp.arange(n_tiles, dtype=jnp.int32)
    tile_expert = jnp.minimum(jnp.sum((tile_ids[:, None] >= tile_end[None, :]).astype(jnp.int32), axis=1),
                              N_EXPERTS - 1)
    n_pairs = 2 * n_tokens
    order = jnp.sort(e_flat * n_pairs + jnp.arange(n_pairs, dtype=jnp.int32)) % n_pairs
    unpadded_start = jnp.cumsum(counts) - counts
    tile_shift = (unpadded_start - group_start)[tile_expert]
    tile_rows_left = (counts + group_start)[tile_expert] - tile_ids * tm
    sorted_pos = (tile_ids * tm + tile_shift)[:, None] + jnp.arange(tm, dtype=jnp.int32)[None, :]
    valid = jnp.arange(tm, dtype=jnp.int32)[None, :] < tile_rows_left[:, None]
    src_tok = jnp.where(valid, jnp.take(order, jnp.clip(sorted_pos, 0, n_pairs - 1).reshape(-1), mode="clip")
                        .reshape(n_tiles, tm) // 2, 0).reshape(-1)
    return src_tok, dest.reshape(n_tokens, 2), tile_expert, tile_end[-1:].astype(jnp.int32)


def _odd_layer(x_f, x_b, bsz, seq, w_in, lam_params, subln_w, w_out, ln1, router, w1, w3, w2, layer, ln2, rope,
               alpha, lambda_init):
    d = x_f.shape[1]
    n_tok = bsz * seq
    w_in = w_in.astype(BF16)
    cos_t, sin_t = rope
    scale_row = jnp.concatenate([jnp.full((1, d), LOG2_E * C_HEAD_DIM ** -0.5, F32), jnp.ones((1, d), F32)], axis=1)
    qk = _proj_rope(x_b, w_in[:, :2 * d], cos_t, sin_t, scale_row, seq)
    v = _proj(x_b, w_in[:, 2 * d:], BF16)
    o = _diff_attention(qk.reshape(bsz, seq, 2 * d), v.reshape(bsz, seq, d), lam_params.astype(F32), subln_w,
                        lambda_init)
    router_padded = jnp.pad(router.astype(F32), ((0, 0), (0, LANES - N_EXPERTS)))
    x_f, x_b, gates, idx = _out_ln([o.reshape(n_tok, d)], [w_out.astype(BF16)], x_f, ln1[0], ln1[1], alpha,
                                   router_padded)
    tm = _tile(n_tok, (1024, 512, 256))
    src_tok, pos, tile_expert, n_active = _moe_dispatch(idx, n_tok, tm)
    n_tiles = src_tok.shape[0] // tm
    w1, w3, w2 = (_layer_weights_bf16(w, layer) for w in (w1, w3, w2))
    y = None
    for lo, hi in ((0, n_tiles // 2), (n_tiles // 2, n_tiles)):
        x_rows = jnp.take(x_b, src_tok[lo * tm:hi * tm], axis=0, mode="clip")
        n_act = jnp.clip(n_active - lo, 0, hi - lo)
        y = _moe_ffn(x_rows, tile_expert[lo:hi], n_act, w1, w3, w2, tm, lo, n_tiles * tm, y)
    y_both = jnp.take(y, pos.T.reshape(-1), axis=0, mode="clip")
    return _moe_combine(x_f, y_both, gates, ln2[0], ln2[1], alpha)


def kernel(x, ev_w_in, ev_lb_logits, ev_hgrn_norm, ev_w_out, ev_ln1_g, ev_ln1_b, ev_w1, ev_w3, ev_w2, ev_ln2_g,
           ev_ln2_b, od_w_in, od_lambda, od_subln, od_w_out, od_ln1_g, od_ln1_b, od_router, od_w1, od_w3, od_w2,
           od_ln2_g, od_ln2_b):
    bsz, seq, d = x.shape
    depth = ev_w_in.shape[0] + od_w_in.shape[0]
    alpha = (2 * depth) ** 0.25
    rope = _rope_tables(seq, LANES)
    lb_soft = jax.nn.softmax(ev_lb_logits.astype(F32), axis=0)
    lower_bounds = jnp.cumsum(lb_soft, axis=0) - lb_soft[0]
    x_f = x.reshape(bsz * seq, d).astype(F32)
    x_b = x_f.astype(BF16)
    for layer in range(depth):
        j = layer // 2
        if layer % 2 == 0:
            x_f, x_b = _even_layer(x_f, x_b, bsz, seq, ev_w_in[j], lower_bounds[j], ev_hgrn_norm[j], ev_w_out[j],
                                   (ev_ln1_g[j], ev_ln1_b[j]), ev_w1, ev_w3, ev_w2, j,
                                   (ev_ln2_g[j], ev_ln2_b[j]), rope, alpha)
        else:
            lambda_init = 0.8 - 0.6 * math.exp(-0.3 * layer)
            x_f, x_b = _odd_layer(x_f, x_b, bsz, seq, od_w_in[j], od_lambda[j], od_subln[j], od_w_out[j],
                                  (od_ln1_g[j], od_ln1_b[j]), od_router[j], od_w1, od_w3, od_w2, j,
                                  (od_ln2_g[j], od_ln2_b[j]), rope, alpha, lambda_init)
    return x_f.reshape(bsz, seq, d).astype(x.dtype)
```
